```python
import jax
import jax.numpy as jnp
from jax import lax
import numpy as np


D_MODEL = 2048
BATCH = 2
SEQ = 16384
DEPTH = 2

BLOCK = 128
ROPE_THETA = 10000.0
LN_EPS = 1e-5
RMS_EPS = 1e-6
PLE_DIM = 256
SWA_HEADS = 32
SWA_KV_HEADS = 4
SWA_GROUP = SWA_HEADS // SWA_KV_HEADS
SWA_HEAD_DIM = 64
SWA_WINDOW = 128
MLA_HEADS = 16
MLA_NOPE = 128
MLA_ROPE = 64
MLA_V = 128
MLA_LATENT = 512
MLA_SCALE = (MLA_NOPE + MLA_ROPE) ** -0.5
IDX_HEADS = 16
IDX_DIM = 64
IDX_ROPE = 32
DSA_TOPK = 256
DSA_PARTS = (MLA_HEADS * (MLA_NOPE + MLA_ROPE), MLA_LATENT, MLA_ROPE, IDX_HEADS * IDX_DIM, IDX_DIM, IDX_HEADS)
DSA_IN = sum(DSA_PARTS)
DSA_SPLITS = tuple(int(v) for v in np.cumsum(DSA_PARTS)[:-1])
FFN_DIM = 5632
MOE_EXPERTS = 8
MOE_TOP_K = 2
MOE_FFN_DIM = 7168
MOE_BLOCK = 512

kernel_name = 'hybrid_swa_dsa_deepnorm_moe'


def layer_norm(x, g, b):
    xf = x.astype(jnp.float32)
    mu = jnp.mean(xf, axis=-1, keepdims=True)
    var = jnp.mean(jnp.square(xf - mu), axis=-1, keepdims=True)
    return ((xf - mu) * lax.rsqrt(var + LN_EPS) * g + b).astype(x.dtype)


def rms_norm(x, g):
    xf = x.astype(jnp.float32)
    return (xf * lax.rsqrt(jnp.mean(jnp.square(xf), axis=-1, keepdims=True) + RMS_EPS) * g).astype(x.dtype)


def rope(x, pos):
    d = x.shape[-1]
    inv_freq = ROPE_THETA ** (-jnp.arange(0, d, 2, dtype=jnp.float32) / d)
    ang = pos.astype(jnp.float32)[..., None] * inv_freq
    cos = jnp.cos(ang)[:, :, None, :]
    sin = jnp.sin(ang)[:, :, None, :]
    xf = x.astype(jnp.float32)
    x1, x2 = xf[..., : d // 2], xf[..., d // 2:]
    return jnp.concatenate([x1 * cos - x2 * sin, x2 * cos + x1 * sin], axis=-1).astype(x.dtype)


def _to_blocks(a):
    b, s = a.shape[:2]
    return jnp.moveaxis(a.reshape((b, s // BLOCK, BLOCK) + a.shape[2:]), 1, 0)


def _from_blocks(a):
    nb, b, bq = a.shape[:3]
    return jnp.moveaxis(a, 0, 1).reshape((b, nb * bq) + a.shape[3:])


def swa_mixer(x, pos, w_qkv, sinks, w_o):
    b, s, _ = x.shape
    nb = s // BLOCK
    qkv = x @ w_qkv
    q, k, v = jnp.split(qkv, [SWA_HEADS * SWA_HEAD_DIM, (SWA_HEADS + SWA_KV_HEADS) * SWA_HEAD_DIM], axis=-1)
    q = rope(q.reshape(b, s, SWA_HEADS, SWA_HEAD_DIM), pos).reshape(b, s, SWA_KV_HEADS, SWA_GROUP, SWA_HEAD_DIM)
    k = rope(k.reshape(b, s, SWA_KV_HEADS, SWA_HEAD_DIM), pos)
    v = v.reshape(b, s, SWA_KV_HEADS, SWA_HEAD_DIM)

    def band(a):
        ab = a.reshape(b, nb, BLOCK, SWA_KV_HEADS, SWA_HEAD_DIM)
        prev = jnp.concatenate([jnp.zeros_like(ab[:, :1]), ab[:, :-1]], axis=1)
        return jnp.moveaxis(jnp.concatenate([prev, ab], axis=2), 1, 0)

    q_pos = jnp.arange(s).reshape(nb, BLOCK)
    k_pos = q_pos[:, :1] - BLOCK + jnp.arange(2 * BLOCK)[None, :]
    sink = sinks.astype(jnp.float32).reshape(SWA_KV_HEADS, SWA_GROUP)
    scale = SWA_HEAD_DIM ** -0.5

    def attend(args):
        qb, kb, vb, qp, kp = args
        logits = jnp.einsum('bqkgd,bskd->bkgqs', qb, kb).astype(jnp.float32) * scale
        rel = qp[:, None] - kp[None, :]
        allowed = (rel >= 0) & (rel < SWA_WINDOW) & (kp[None, :] >= 0)
        logits = jnp.where(allowed, logits, -jnp.inf)
        sink_col = jnp.broadcast_to(sink[None, :, :, None, None], logits.shape[:-1] + (1,))
        probs = jax.nn.softmax(jnp.concatenate([logits, sink_col], axis=-1), axis=-1)[..., :-1]
        return jnp.einsum('bkgqs,bskd->bqkgd', probs.astype(vb.dtype), vb)

    o = lax.map(attend, (_to_blocks(q), band(k), band(v), q_pos, k_pos))
    o = _from_blocks(o).reshape(b, s, SWA_HEADS * SWA_HEAD_DIM)
    return o @ w_o


def dsa_mixer(x, pos, w_in, kv_norm, w_uk, w_uv, w_o):
    b, s, _ = x.shape
    nb = s // BLOCK
    n_sel = min(DSA_TOPK, s // 4)
    h = x @ w_in
    q, c_kv, k_r, q_i, k_i, w_i = jnp.split(h, DSA_SPLITS, axis=-1)
    q = q.reshape(b, s, MLA_HEADS, MLA_NOPE + MLA_ROPE)
    q_nope = q[..., :MLA_NOPE]
    q_rope = rope(q[..., MLA_NOPE:], pos)
    c_kv = rms_norm(c_kv, kv_norm)
    k_r = rope(k_r[:, :, None, :], pos)[:, :, 0]
    q_i = q_i.reshape(b, s, IDX_HEADS, IDX_DIM)
    q_i = jnp.concatenate([rope(q_i[..., :IDX_ROPE], pos), q_i[..., IDX_ROPE:]], axis=-1)
    k_i = jnp.concatenate([rope(k_i[:, :, None, :IDX_ROPE], pos)[:, :, 0], k_i[..., IDX_ROPE:]], axis=-1)
    w_i = w_i * IDX_HEADS ** -0.5
    key_pos = jnp.arange(s)
    b_ix = jnp.arange(b)[:, None, None]

    def attend(args):
        qn, qr, qi, wi, qp = args
        idx_logits = jnp.einsum('bqhd,bsd->bqhs', qi, k_i).astype(jnp.float32) * IDX_DIM ** -0.5
        score = jnp.einsum('bqh,bqhs->bqs', wi.astype(jnp.float32), jax.nn.relu(idx_logits))
        score = jnp.where(key_pos[None, None, :] <= qp[None, :, None], score, -jnp.inf)
        _, sel = lax.top_k(score, n_sel)
        valid = sel <= qp[None, :, None]
        c_sel = c_kv[b_ix, sel]
        kr_sel = k_r[b_ix, sel]
        q_lat = jnp.einsum('bqhn,hcn->bqhc', qn, w_uk)
        logits = (jnp.einsum('bqhc,bqkc->bqhk', q_lat, c_sel)
                  + jnp.einsum('bqhr,bqkr->bqhk', qr, kr_sel)).astype(jnp.float32) * MLA_SCALE
        logits = jnp.where(valid[:, :, None, :], logits, -jnp.inf)
        probs = jax.nn.softmax(logits, axis=-1).astype(c_sel.dtype)
        o_lat = jnp.einsum('bqhk,bqkc->bqhc', probs, c_sel)
        return jnp.einsum('bqhc,hcv->bqhv', o_lat, w_uv)

    o = lax.map(attend, (_to_blocks(q_nope), _to_blocks(q_rope), _to_blocks(q_i), _to_blocks(w_i),
                         key_pos.reshape(nb, BLOCK)))
    return _from_blocks(o).reshape(b, s, MLA_HEADS * MLA_V) @ w_o


def swiglu(x, w1, w3, w2):
    return (jax.nn.silu(x @ w1) * (x @ w3)) @ w2


def moe_swiglu(h, w_router, w1, w3, w2):
    t, d = h.shape
    n = t * MOE_TOP_K
    logits = (h @ w_router).astype(jnp.float32)
    top_val, top_idx = lax.top_k(logits, MOE_TOP_K)
    gates = jax.nn.softmax(top_val, axis=-1).astype(h.dtype)
    e_flat = top_idx.reshape(n)
    tok_flat = jnp.arange(n, dtype=jnp.int32) // MOE_TOP_K
    g_flat = gates.reshape(n)
    order = jnp.argsort(e_flat)
    e_sorted = e_flat[order]
    counts = jnp.bincount(e_flat, length=MOE_EXPERTS)
    starts = jnp.cumsum(counts) - counts
    padded = (counts + MOE_BLOCK - 1) // MOE_BLOCK * MOE_BLOCK
    pends = jnp.cumsum(padded)
    pstarts = pends - padded
    dest = pstarts[e_sorted] + (jnp.arange(n) - starts[e_sorted])
    n_blocks = (n + MOE_EXPERTS * (MOE_BLOCK - 1) + MOE_BLOCK - 1) // MOE_BLOCK
    n_slots = n_blocks * MOE_BLOCK
    slot_tok = jnp.zeros((n_slots,), jnp.int32).at[dest].set(tok_flat[order])
    slot_gate = jnp.zeros((n_slots,), h.dtype).at[dest].set(g_flat[order])
    blk_start = jnp.arange(n_blocks, dtype=pends.dtype) * MOE_BLOCK
    blk_expert = jnp.minimum(jnp.searchsorted(pends, blk_start, side='right'), MOE_EXPERTS - 1)

    def run_block(args):
        tok, e = args
        xb = h[tok]
        return (jax.nn.silu(xb @ w1[e]) * (xb @ w3[e])) @ w2[e]

    y = lax.map(run_block, (slot_tok.reshape(n_blocks, MOE_BLOCK), blk_expert)).reshape(n_slots, d)
    return jax.ops.segment_sum(y * slot_gate[:, None], slot_tok, num_segments=t)


def setup_inputs(seed: int = 0) -> dict:
    key = jax.random.key(seed)
    ks = jax.random.split(key, 24)
    n_a = (DEPTH + 1) // 2
    n_b = DEPTH // 2
    beta = (8.0 * DEPTH) ** -0.25

    def nrm(k, shape, scale):
        return jax.random.normal(k, shape, jnp.float32) * scale

    qkv_out = (SWA_HEADS + 2 * SWA_KV_HEADS) * SWA_HEAD_DIM
    offset = jax.random.randint(ks[2], (BATCH, 1), 0, 1024, dtype=jnp.int32)
    return {
        'x': nrm(ks[0], (BATCH, SEQ, D_MODEL), 1.0),
        'p': nrm(ks[1], (DEPTH, BATCH, SEQ, PLE_DIM), 1.0),
        'positions': offset + jnp.arange(SEQ, dtype=jnp.int32)[None, :],
        'swa_w_qkv': nrm(ks[3], (n_a, D_MODEL, qkv_out), D_MODEL ** -0.5),
        'swa_sinks': nrm(ks[4], (n_a, SWA_HEADS), 0.5),
        'swa_w_o': nrm(ks[5], (n_a, SWA_HEADS * SWA_HEAD_DIM, D_MODEL), beta * (SWA_HEADS * SWA_HEAD_DIM) ** -0.5),
        'dense_w1': nrm(ks[6], (n_a, D_MODEL, FFN_DIM), D_MODEL ** -0.5),
        'dense_w3': nrm(ks[7], (n_a, D_MODEL, FFN_DIM), D_MODEL ** -0.5),
        'dense_w2': nrm(ks[8], (n_a, FFN_DIM, D_MODEL), beta * FFN_DIM ** -0.5),
        'dsa_w_in': nrm(ks[9], (n_b, D_MODEL, DSA_IN), D_MODEL ** -0.5),
        'dsa_kv_norm': 1.0 + nrm(ks[10], (n_b, MLA_LATENT), 0.02),
        'dsa_w_uk': nrm(ks[11], (n_b, MLA_HEADS, MLA_LATENT, MLA_NOPE), MLA_LATENT ** -0.5),
        'dsa_w_uv': nrm(ks[12], (n_b, MLA_HEADS, MLA_LATENT, MLA_V), MLA_LATENT ** -0.5),
        'dsa_w_o': nrm(ks[13], (n_b, MLA_HEADS * MLA_V, D_MODEL), beta * (MLA_HEADS * MLA_V) ** -0.5),
        'moe_router': nrm(ks[14], (n_b, D_MODEL, MOE_EXPERTS), D_MODEL ** -0.5),
        'moe_w1': nrm(ks[15], (n_b, MOE_EXPERTS, D_MODEL, MOE_FFN_DIM), D_MODEL ** -0.5),
        'moe_w3': nrm(ks[16], (n_b, MOE_EXPERTS, D_MODEL, MOE_FFN_DIM), D_MODEL ** -0.5),
        'moe_w2': nrm(ks[17], (n_b, MOE_EXPERTS, MOE_FFN_DIM, D_MODEL), beta * MOE_FFN_DIM ** -0.5),
        'ln_g': 1.0 + nrm(ks[18], (DEPTH, 2, D_MODEL), 0.02),
        'ln_b': nrm(ks[19], (DEPTH, 2, D_MODEL), 0.02),
        'ple_w_p': nrm(ks[20], (DEPTH, PLE_DIM, D_MODEL), PLE_DIM ** -0.5),
        'ple_w_g': nrm(ks[21], (DEPTH, D_MODEL, D_MODEL), D_MODEL ** -0.5),
    }


def reference(x, p, positions, swa_w_qkv, swa_sinks, swa_w_o, dense_w1, dense_w3, dense_w2,
              dsa_w_in, dsa_kv_norm, dsa_w_uk, dsa_w_uv, dsa_w_o, moe_router, moe_w1, moe_w3, moe_w2,
              ln_g, ln_b, ple_w_p, ple_w_g):
    alpha = (2.0 * DEPTH) ** 0.25
    for i in range(DEPTH):
        j = i // 2
        if i % 2 == 0:
            mix = swa_mixer(x, positions, swa_w_qkv[j], swa_sinks[j], swa_w_o[j])
        else:
            mix = dsa_mixer(x, positions, dsa_w_in[j], dsa_kv_norm[j], dsa_w_uk[j], dsa_w_uv[j], dsa_w_o[j])
        x = layer_norm(alpha * x + mix, ln_g[i, 0], ln_b[i, 0])
        if i % 2 == 0:
            ff = swiglu(x, dense_w1[j], dense_w3[j], dense_w2[j])
        else:
            ff = moe_swiglu(x.reshape(-1, x.shape[-1]), moe_router[j], moe_w1[j], moe_w3[j], moe_w2[j]).reshape(x.shape)
        x = layer_norm(alpha * x + ff, ln_g[i, 1], ln_b[i, 1])
        x = x + jax.nn.sigmoid(x @ ple_w_g[i]) * (p[i] @ ple_w_p[i])
    return x
```

```python
import functools

import jax
import jax.numpy as jnp
from jax import lax
from jax.experimental import pallas as pl
from jax.experimental.pallas import tpu as pltpu

F32 = jnp.float32
BF16 = jnp.bfloat16
I32 = jnp.int32

LANES = 128
ROPE_THETA = 10000.0
LN_EPS = 1e-5
RMS_EPS = 1e-6
SWA_HEADS = 32
SWA_KV_HEADS = 4
SWA_GROUP = SWA_HEADS // SWA_KV_HEADS
SWA_HEAD_DIM = 64
SWA_WINDOW = 128
MLA_HEADS = 16
MLA_NOPE = 128
MLA_ROPE = 64
MLA_V = 128
MLA_LATENT = 512
MLA_SCALE = (MLA_NOPE + MLA_ROPE) ** -0.5
MLA_QK_PAD = 256
IDX_HEADS = 16
IDX_DIM = 64
IDX_ROPE = 32
DSA_TOPK = 256
MOE_EXPERTS = 8
MOE_TOP_K = 2
NEG_BIG = -1e30
M_INIT = -1e20
VMEM_LIMIT = 56 * 1024 * 1024


def _params(sem, vmem=VMEM_LIMIT):
    return pltpu.CompilerParams(dimension_semantics=sem, vmem_limit_bytes=vmem)


def _layer_norm(y, g, b):
    mu = jnp.mean(y, axis=-1, keepdims=True)
    d = y - mu
    var = jnp.mean(d * d, axis=-1, keepdims=True)
    return d * lax.rsqrt(var + LN_EPS) * g + b


def _rope_lanes(a, cs, sn, half):
    lane = lax.broadcasted_iota(I32, a.shape, 1)
    first = (lane % 64) < half
    rot = jnp.where(first, pltpu.roll(a, LANES - half, 1), pltpu.roll(a, half, 1))
    return a * cs + rot * sn


def _rope_table_kernel(pos_ref, invf_ref, sgn_ref, cos_ref, sin_ref):
    ang = pos_ref[...].astype(F32) * invf_ref[...]
    cos_ref[...] = jnp.cos(ang)
    sin_ref[...] = jnp.sin(ang) * sgn_ref[...]


def _rope_tables(positions):
    t = positions.size
    pos = positions.reshape(t, 1).astype(I32)
    lane = jnp.arange(LANES)
    in64 = lane % 64
    f64 = ROPE_THETA ** (-jnp.arange(0, 64, 2, dtype=F32) / 64)
    f32_ = ROPE_THETA ** (-jnp.arange(0, IDX_ROPE, 2, dtype=F32) / IDX_ROPE)
    invf_a = f64[in64 % 32]
    sgn_a = jnp.where(in64 < 32, -1.0, 1.0).astype(F32)
    invf_b = jnp.where(in64 < IDX_ROPE, f32_[in64 % (IDX_ROPE // 2)], 0.0)
    sgn_b = jnp.where(in64 < IDX_ROPE // 2, -1.0, 1.0).astype(F32)
    invf = jnp.concatenate([invf_a, invf_b])[None, :]
    sgn = jnp.concatenate([sgn_a, sgn_b])[None, :]
    tm = min(t, 1024)
    return pl.pallas_call(
        _rope_table_kernel,
        name="rope_tables",
        grid=(t // tm,),
        in_specs=[pl.BlockSpec((tm, 1), lambda i: (i, 0)),
                  pl.BlockSpec((1, 2 * LANES), lambda i: (0, 0)),
                  pl.BlockSpec((1, 2 * LANES), lambda i: (0, 0))],
        out_specs=[pl.BlockSpec((tm, 2 * LANES), lambda i: (i, 0)),
                   pl.BlockSpec((tm, 2 * LANES), lambda i: (i, 0))],
        out_shape=[jax.ShapeDtypeStruct((t, 2 * LANES), F32)] * 2,
        compiler_params=_params(("parallel",)),
    )(pos, invf, sgn)


def _swa_qkv_kernel(x_ref, w_ref, cos_ref, sin_ref, q_ref, k_ref, v_ref, *, nq, nk):
    xb = x_ref[...].astype(BF16)
    acc = jnp.dot(xb, w_ref[...], preferred_element_type=F32)
    cs = cos_ref[:, :LANES]
    sn = sin_ref[:, :LANES]
    scale = SWA_HEAD_DIM ** -0.5
    for c in range(nq // LANES):
        a = acc[:, c * LANES:(c + 1) * LANES]
        q_ref[:, c * LANES:(c + 1) * LANES] = (_rope_lanes(a, cs, sn, 32) * scale).astype(BF16)
    for c in range(nk // LANES):
        a = acc[:, nq + c * LANES:nq + (c + 1) * LANES]
        k_ref[:, c * LANES:(c + 1) * LANES] = _rope_lanes(a, cs, sn, 32).astype(BF16)
    v_ref[...] = acc[:, nq + nk:].astype(BF16)


def _swa_qkv(x2, w_qkv, cos_t, sin_t):
    t, d = x2.shape
    nq = SWA_HEADS * SWA_HEAD_DIM
    nk = SWA_KV_HEADS * SWA_HEAD_DIM
    tm = min(t, 512)
    return pl.pallas_call(
        functools.partial(_swa_qkv_kernel, nq=nq, nk=nk),
        name="swa_qkv",
        grid=(t // tm,),
        in_specs=[pl.BlockSpec((tm, d), lambda i: (i, 0)),
                  pl.BlockSpec((d, nq + 2 * nk), lambda i: (0, 0)),
                  pl.BlockSpec((tm, 2 * LANES), lambda i: (i, 0)),
                  pl.BlockSpec((tm, 2 * LANES), lambda i: (i, 0))],
        out_specs=[pl.BlockSpec((tm, nq), lambda i: (i, 0)),
                   pl.BlockSpec((tm, nk), lambda i: (i, 0)),
                   pl.BlockSpec((tm, nk), lambda i: (i, 0))],
        out_shape=[jax.ShapeDtypeStruct((t, nq), BF16),
                   jax.ShapeDtypeStruct((t, nk), BF16),
                   jax.ShapeDtypeStruct((t, nk), BF16)],
        compiler_params=_params(("parallel",)),
    )(x2, w_qkv.astype(BF16), cos_t, sin_t)


def _swa_attn_kernel(sink_ref, q_ref, kp_ref, kc_ref, vp_ref, vc_ref, o_ref):
    blk = pl.program_id(1)
    bq = q_ref.shape[0]
    dh = SWA_HEAD_DIM
    rows = SWA_GROUP * bq
    qi = lax.broadcasted_iota(I32, (rows, 2 * bq), 0) % bq
    kj = lax.broadcasted_iota(I32, (rows, 2 * bq), 1)
    rel = qi + bq - kj
    allowed = (rel >= 0) & (rel < SWA_WINDOW) & ((kj >= bq) | (blk > 0))
    for kh in range(SWA_KV_HEADS):
        kcat = jnp.concatenate([kp_ref[:, kh * dh:(kh + 1) * dh], kc_ref[:, kh * dh:(kh + 1) * dh]], axis=0)
        vcat = jnp.concatenate([vp_ref[:, kh * dh:(kh + 1) * dh], vc_ref[:, kh * dh:(kh + 1) * dh]], axis=0)
        qs = jnp.concatenate(
            [q_ref[:, (kh * SWA_GROUP + g) * dh:(kh * SWA_GROUP + g + 1) * dh] for g in range(SWA_GROUP)], axis=0)
        sink = jnp.concatenate(
            [jnp.full((bq, 1), sink_ref[kh * SWA_GROUP + g], F32) for g in range(SWA_GROUP)], axis=0)
        logits = lax.dot_general(qs, kcat, (((1,), (1,)), ((), ())), preferred_element_type=F32)
        logits = jnp.where(allowed, logits, NEG_BIG)
        m = jnp.maximum(jnp.max(logits, axis=-1, keepdims=True), sink)
        p = jnp.exp(logits - m)
        denom = jnp.sum(p, axis=-1, keepdims=True) + jnp.exp(sink - m)
        o = jnp.dot(p.astype(BF16), vcat, preferred_element_type=F32) / denom
        for g in range(SWA_GROUP):
            h = kh * SWA_GROUP + g
            o_ref[:, h * dh:(h + 1) * dh] = o[g * bq:(g + 1) * bq].astype(BF16)


def _swa_attn(q, k, v, sinks, batch, seq):
    t, nq = q.shape
    nk = k.shape[1]
    bq = SWA_WINDOW
    nb = seq // bq

    def cur(b, i):
        return (b * nb + i, 0)

    def prev(b, i):
        return (b * nb + jnp.maximum(i - 1, 0), 0)

    return pl.pallas_call(
        _swa_attn_kernel,
        name="swa_attn",
        grid=(batch, nb),
        in_specs=[pl.BlockSpec(memory_space=pltpu.SMEM),
                  pl.BlockSpec((bq, nq), cur),
                  pl.BlockSpec((bq, nk), prev),
                  pl.BlockSpec((bq, nk), cur),
                  pl.BlockSpec((bq, nk), prev),
                  pl.BlockSpec((bq, nk), cur)],
        out_specs=pl.BlockSpec((bq, nq), cur),
        out_shape=jax.ShapeDtypeStruct((t, nq), BF16),
        compiler_params=_params(("parallel", "parallel")),
    )(sinks.astype(F32), q, k, k, v, v)


def _proj_ln_kernel(a_ref, w_ref, res_ref, g_ref, b_ref, o_ref, *, alpha):
    y = alpha * res_ref[...] + jnp.dot(a_ref[...], w_ref[...], preferred_element_type=F32)
    o_ref[...] = _layer_norm(y, g_ref[...], b_ref[...])


def _proj_ln(a, w, res, g, b, alpha):
    t, k = a.shape
    d = w.shape[1]
    tm = min(t, 512)
    return pl.pallas_call(
        functools.partial(_proj_ln_kernel, alpha=alpha),
        name="proj_ln",
        grid=(t // tm,),
        in_specs=[pl.BlockSpec((tm, k), lambda i: (i, 0)),
                  pl.BlockSpec((k, d), lambda i: (0, 0)),
                  pl.BlockSpec((tm, d), lambda i: (i, 0)),
                  pl.BlockSpec((1, d), lambda i: (0, 0)),
                  pl.BlockSpec((1, d), lambda i: (0, 0))],
        out_specs=pl.BlockSpec((tm, d), lambda i: (i, 0)),
        out_shape=jax.ShapeDtypeStruct((t, d), F32),
        compiler_params=_params(("parallel",)),
    )(a, w.astype(BF16), res, g[None, :], b[None, :])


def _ffn_dense_kernel(x_ref, w1_ref, w3_ref, w2_ref, g_ref, b_ref, o_ref, xb_ref, acc_ref, *, alpha):
    f = pl.program_id(1)

    @pl.when(f == 0)
    def _():
        xb_ref[...] = x_ref[...].astype(BF16)
        acc_ref[...] = jnp.zeros_like(acc_ref)

    xb = xb_ref[...]
    gate = jnp.dot(xb, w1_ref[...], preferred_element_type=F32)
    up = jnp.dot(xb, w3_ref[...], preferred_element_type=F32)
    h = (jax.nn.silu(gate) * up).astype(BF16)
    acc_ref[...] += jnp.dot(h, w2_ref[...], preferred_element_type=F32)

    @pl.when(f == pl.num_programs(1) - 1)
    def _():
        o_ref[...] = _layer_norm(alpha * x_ref[...] + acc_ref[...], g_ref[...], b_ref[...])


def _ffn_tile(f_dim):
    for tf in (512, 256, 128):
        if f_dim % tf == 0:
            return tf
    return f_dim


def _ffn_dense(x2, w1, w3, w2, g, b, alpha):
    t, d = x2.shape
    f_dim = w1.shape[1]
    tm = min(t, 512)
    tf = _ffn_tile(f_dim)
    return pl.pallas_call(
        functools.partial(_ffn_dense_kernel, alpha=alpha),
        name="ffn_dense",
        grid=(t // tm, f_dim // tf),
        in_specs=[pl.BlockSpec((tm, d), lambda i, f: (i, 0)),
                  pl.BlockSpec((d, tf), lambda i, f: (0, f)),
                  pl.BlockSpec((d, tf), lambda i, f: (0, f)),
                  pl.BlockSpec((tf, d), lambda i, f: (f, 0)),
                  pl.BlockSpec((1, d), lambda i, f: (0, 0)),
                  pl.BlockSpec((1, d), lambda i, f: (0, 0))],
        out_specs=pl.BlockSpec((tm, d), lambda i, f: (i, 0)),
        out_shape=jax.ShapeDtypeStruct((t, d), F32),
        scratch_shapes=[pltpu.VMEM((tm, d), BF16), pltpu.VMEM((tm, d), F32)],
        compiler_params=_params(("parallel", "arbitrary")),
    )(x2, w1.astype(BF16), w3.astype(BF16), w2.astype(BF16), g[None, :], b[None, :])


def _ple_kernel(x_ref, p_ref, wg_ref, wp_ref, o_ref):
    x = x_ref[...]
    gate = jax.nn.sigmoid(jnp.dot(x.astype(BF16), wg_ref[...], preferred_element_type=F32))
    emb = jnp.dot(p_ref[...].astype(BF16), wp_ref[...], preferred_element_type=F32)
    o_ref[...] = x + gate * emb


def _ple(x2, p2, w_g, w_p):
    t, d = x2.shape
    pd = p2.shape[1]
    tm = min(t, 512)
    return pl.pallas_call(
        _ple_kernel,
        name="ple",
        grid=(t // tm,),
        in_specs=[pl.BlockSpec((tm, d), lambda i: (i, 0)),
                  pl.BlockSpec((tm, pd), lambda i: (i, 0)),
                  pl.BlockSpec((d, d), lambda i: (0, 0)),
                  pl.BlockSpec((pd, d), lambda i: (0, 0))],
        out_specs=pl.BlockSpec((tm, d), lambda i: (i, 0)),
        out_shape=jax.ShapeDtypeStruct((t, d), F32),
        compiler_params=_params(("parallel",)),
    )(x2, p2, w_g.astype(BF16), w_p.astype(BF16))


def _dsa_q_kernel(x_ref, w_ref, cos_ref, sin_ref, q_ref):
    xb = x_ref[...].astype(BF16)
    acc = jnp.dot(xb, w_ref[...], preferred_element_type=F32)
    cs = cos_ref[:, :LANES]
    sn = sin_ref[:, :LANES]
    for h in range(MLA_HEADS):
        base = h * MLA_QK_PAD
        q_ref[:, base:base + LANES] = (acc[:, base:base + LANES] * MLA_SCALE).astype(BF16)
        a = acc[:, base + LANES:base + 2 * LANES]
        q_ref[:, base + LANES:base + 2 * LANES] = (_rope_lanes(a, cs, sn, 32) * MLA_SCALE).astype(BF16)


def _dsa_q(x2, w_q, cos_t, sin_t):
    t, d = x2.shape
    n = w_q.shape[1]
    tm = min(t, 512)
    return pl.pallas_call(
        _dsa_q_kernel,
        name="dsa_q",
        grid=(t // tm,),
        in_specs=[pl.BlockSpec((tm, d), lambda i: (i, 0)),
                  pl.BlockSpec((d, n), lambda i: (0, 0)),
                  pl.BlockSpec((tm, 2 * LANES), lambda i: (i, 0)),
                  pl.BlockSpec((tm, 2 * LANES), lambda i: (i, 0))],
        out_specs=pl.BlockSpec((tm, n), lambda i: (i, 0)),
        out_shape=jax.ShapeDtypeStruct((t, n), BF16),
        compiler_params=_params(("parallel",)),
    )(x2, w_q, cos_t, sin_t)


def _dsa_small_kernel(x_ref, w_ref, cos_ref, sin_ref, kvn_ref, ckv_ref, kr_ref, ki_ref, wi_ref):
    xb = x_ref[...].astype(BF16)
    acc = jnp.dot(xb, w_ref[...], preferred_element_type=F32)
    c = acc[:, :MLA_LATENT]
    ms = jnp.mean(c * c, axis=-1, keepdims=True)
    ckv_ref[...] = (c * lax.rsqrt(ms + RMS_EPS) * kvn_ref[...]).astype(BF16)
    o = MLA_LATENT
    kr_ref[...] = _rope_lanes(acc[:, o:o + LANES], cos_ref[:, :LANES], sin_ref[:, :LANES], 32).astype(BF16)
    ki_ref[...] = _rope_lanes(acc[:, o + LANES:o + 2 * LANES], cos_ref[:, LANES:], sin_ref[:, LANES:],
                              IDX_ROPE // 2).astype(BF16)
    wi_ref[...] = acc[:, o + 2 * LANES:o + 3 * LANES] * (IDX_HEADS ** -0.5)


def _dsa_small(x2, w_small, kv_norm, cos_t, sin_t):
    t, d = x2.shape
    n = w_small.shape[1]
    tm = min(t, 512)
    return pl.pallas_call(
        _dsa_small_kernel,
        name="dsa_small",
        grid=(t // tm,),
        in_specs=[pl.BlockSpec((tm, d), lambda i: (i, 0)),
                  pl.BlockSpec((d, n), lambda i: (0, 0)),
                  pl.BlockSpec((tm, 2 * LANES), lambda i: (i, 0)),
                  pl.BlockSpec((tm, 2 * LANES), lambda i: (i, 0)),
                  pl.BlockSpec((1, MLA_LATENT), lambda i: (0, 0))],
        out_specs=[pl.BlockSpec((tm, MLA_LATENT), lambda i: (i, 0)),
                   pl.BlockSpec((tm, LANES), lambda i: (i, 0)),
                   pl.BlockSpec((tm, LANES), lambda i: (i, 0)),
                   pl.BlockSpec((tm, LANES), lambda i: (i, 0))],
        out_shape=[jax.ShapeDtypeStruct((t, MLA_LATENT), BF16),
                   jax.ShapeDtypeStruct((t, LANES), BF16),
                   jax.ShapeDtypeStruct((t, LANES), BF16),
                   jax.ShapeDtypeStruct((t, LANES), F32)],
        compiler_params=_params(("parallel",)),
    )(x2, w_small, cos_t, sin_t, kv_norm[None, :].astype(F32))


def _dsa_qi_kernel(x_ref, w_ref, cos_ref, sin_ref, q_ref):
    xb = x_ref[...].astype(BF16)
    acc = jnp.dot(xb, w_ref[...], preferred_element_type=F32)
    cs = cos_ref[:, LANES:]
    sn = sin_ref[:, LANES:]
    scale = IDX_DIM ** -0.5
    for c in range(acc.shape[1] // LANES):
        a = acc[:, c * LANES:(c + 1) * LANES]
        q_ref[:, c * LANES:(c + 1) * LANES] = (_rope_lanes(a, cs, sn, IDX_ROPE // 2) * scale).astype(BF16)


def _dsa_qi(x2, w_qi, cos_t, sin_t):
    t, d = x2.shape
    n = w_qi.shape[1]
    tm = min(t, 512)
    return pl.pallas_call(
        _dsa_qi_kernel,
        name="dsa_qi",
        grid=(t // tm,),
        in_specs=[pl.BlockSpec((tm, d), lambda i: (i, 0)),
                  pl.BlockSpec((d, n), lambda i: (0, 0)),
                  pl.BlockSpec((tm, 2 * LANES), lambda i: (i, 0)),
                  pl.BlockSpec((tm, 2 * LANES), lambda i: (i, 0))],
        out_specs=pl.BlockSpec((tm, n), lambda i: (i, 0)),
        out_shape=jax.ShapeDtypeStruct((t, n), BF16),
        compiler_params=_params(("parallel",)),
    )(x2, w_qi, cos_t, sin_t)


def _dsa_kv_kernel(c_ref, kr_ref, wk_ref, wvt_ref, k_ref, vt_ref):
    c = c_ref[...]
    kn = jnp.dot(c, wk_ref[...], preferred_element_type=F32)
    kr = kr_ref[...]
    for h in range(MLA_HEADS):
        base = h * MLA_QK_PAD
        k_ref[:, base:base + LANES] = kn[:, h * LANES:(h + 1) * LANES].astype(BF16)
        k_ref[:, base + LANES:base + 2 * LANES] = kr
    vt_ref[...] = lax.dot_general(wvt_ref[...], c, (((1,), (1,)), ((), ())),
                                  preferred_element_type=F32).astype(BF16)


def _dsa_kv(ckv, kr, w_uk_all, w_uv_t):
    t, c_dim = ckv.shape
    nk = MLA_HEADS * MLA_QK_PAD
    nv = w_uv_t.shape[0]
    tm = min(t, 512)
    return pl.pallas_call(
        _dsa_kv_kernel,
        name="dsa_kv",
        grid=(t // tm,),
        in_specs=[pl.BlockSpec((tm, c_dim), lambda i: (i, 0)),
                  pl.BlockSpec((tm, LANES), lambda i: (i, 0)),
                  pl.BlockSpec((c_dim, MLA_HEADS * MLA_NOPE), lambda i: (0, 0)),
                  pl.BlockSpec((nv, c_dim), lambda i: (0, 0))],
        out_specs=[pl.BlockSpec((tm, nk), lambda i: (i, 0)),
                   pl.BlockSpec((nv, tm), lambda i: (0, i))],
        out_shape=[jax.ShapeDtypeStruct((t, nk), BF16),
                   jax.ShapeDtypeStruct((nv, t), BF16)],
        compiler_params=_params(("parallel",)),
    )(ckv, kr, w_uk_all, w_uv_t)


def _float_sort_key(s):
    bits = lax.bitcast_convert_type(s, I32)
    return bits ^ ((bits >> 31) & jnp.int32(0x7FFFFFFF))


def _dsa_select_kernel(k_ref, q_ref, w_ref, bias_ref, key_ref, *, sc, n_sel):
    j = pl.program_id(1)
    s_len, tq = key_ref.shape
    n_chunks = s_len // sc
    q_lo = j * tq
    n_need = (q_lo + tq + sc - 1) // sc
    qf = q_ref[...].reshape(IDX_HEADS * tq, IDX_DIM)
    w_row = w_ref[...]
    q_pos = q_lo + lax.broadcasted_iota(I32, (sc, tq), 1)
    k_off = lax.broadcasted_iota(I32, (sc, tq), 0)
    min_key = jnp.int32(-2 ** 31)

    def score_chunk(c, carry):
        row0 = pl.multiple_of(c * sc, sc)
        kc = k_ref[pl.ds(row0, sc), :]
        d = lax.dot_general(kc, qf, (((1,), (1,)), ((), ())), preferred_element_type=F32)
        s = jnp.zeros((sc, tq), F32)
        for h in range(IDX_HEADS):
            s = s + jnp.maximum(d[:, h * tq:(h + 1) * tq], 0.0) * w_row[:, h * tq:(h + 1) * tq]
        causal = (row0 + k_off) <= q_pos
        key_ref[pl.ds(row0, sc), :] = jnp.where(causal, _float_sort_key(s), min_key)
        return carry

    lax.fori_loop(0, n_need, score_chunk, 0)

    def count_ge(cand):
        def body(c, cnt):
            row0 = pl.multiple_of(c * sc, sc)
            hit = jnp.where(key_ref[pl.ds(row0, sc), :] >= cand, 1, 0).astype(I32)
            return cnt + jnp.sum(hit.reshape(sc // 8, 8, tq), axis=0)

        cnt = lax.fori_loop(0, n_need, body, jnp.zeros((8, tq), I32))
        return jnp.sum(cnt, axis=0, keepdims=True)

    zero = jnp.zeros((1, tq), I32)
    thr = jnp.where(count_ge(zero) >= n_sel, zero, jnp.full((1, tq), min_key, I32))

    def bit_step(it, thr):
        cand = thr + (jnp.int32(1) << (30 - it))
        return jnp.where(count_ge(cand) >= n_sel, cand, thr)

    thr = lax.fori_loop(0, 31, bit_step, thr)

    def write_chunk(c, carry):
        row0 = pl.multiple_of(c * sc, sc)
        keep = (key_ref[pl.ds(row0, sc), :] >= thr) & ((row0 + k_off) <= q_pos)
        bias_ref[pl.ds(row0, sc), :] = jnp.where(keep, 0.0, NEG_BIG).astype(F32)
        return carry

    def mask_chunk(c, carry):
        row0 = pl.multiple_of(c * sc, sc)
        bias_ref[pl.ds(row0, sc), :] = jnp.full((sc, tq), NEG_BIG, F32)
        return carry

    lax.fori_loop(0, n_need, write_chunk, 0)
    lax.fori_loop(n_need, n_chunks, mask_chunk, 0)


def _dsa_select(k_i, q_i_t, w_rows, batch, seq, n_sel):
    tq = LANES
    nq = seq // tq
    sc = min(seq, 256)
    return pl.pallas_call(
        functools.partial(_dsa_select_kernel, sc=sc, n_sel=n_sel),
        name="dsa_select",
        grid=(batch, nq),
        in_specs=[pl.BlockSpec((None, seq, IDX_DIM), lambda b, j: (b, 0, 0)),
                  pl.BlockSpec((None, IDX_HEADS, tq, IDX_DIM), lambda b, j: (b, 0, j, 0)),
                  pl.BlockSpec((None, None, 1, IDX_HEADS * tq), lambda b, j: (b, j, 0, 0))],
        out_specs=pl.BlockSpec((None, seq, tq), lambda b, j: (b, 0, j)),
        out_shape=jax.ShapeDtypeStruct((batch, seq, seq), F32),
        scratch_shapes=[pltpu.VMEM((seq, tq), I32)],
        compiler_params=_params(("parallel", "parallel")),
    )(k_i, q_i_t, w_rows)


def _dsa_attn_kernel(q_ref, k_ref, vt_ref, bias_ref, o_ref, acc_ref, m_ref, l_ref, *, sc):
    j = pl.program_id(1)
    c = pl.program_id(2)
    tq = q_ref.shape[0]
    last = (j * tq + tq - 1) // sc

    @pl.when(c == 0)
    def _():
        acc_ref[...] = jnp.zeros_like(acc_ref)
        m_ref[...] = jnp.full_like(m_ref, M_INIT)
        l_ref[...] = jnp.zeros_like(l_ref)

    @pl.when(c <= last)
    def _():
        bias = bias_ref[...]
        for h in range(MLA_HEADS):
            kh = k_ref[:, h * MLA_QK_PAD:(h + 1) * MLA_QK_PAD]
            qh = q_ref[:, h * MLA_QK_PAD:(h + 1) * MLA_QK_PAD]
            s = lax.dot_general(kh, qh, (((1,), (1,)), ((), ())), preferred_element_type=F32) + bias
            m_old = m_ref[h:h + 1, :]
            m_new = jnp.maximum(m_old, jnp.max(s, axis=0, keepdims=True))
            a = jnp.exp(m_old - m_new)
            p = jnp.exp(s - m_new)
            l_ref[h:h + 1, :] = a * l_ref[h:h + 1, :] + jnp.sum(p, axis=0, keepdims=True)
            m_ref[h:h + 1, :] = m_new
            pv = jnp.dot(vt_ref[h * MLA_V:(h + 1) * MLA_V, :], p.astype(BF16), preferred_element_type=F32)
            acc_ref[h * MLA_V:(h + 1) * MLA_V, :] = a * acc_ref[h * MLA_V:(h + 1) * MLA_V, :] + pv

    @pl.when(c == pl.num_programs(2) - 1)
    def _():
        for h in range(MLA_HEADS):
            inv = 1.0 / l_ref[h:h + 1, :]
            o_ref[h * MLA_V:(h + 1) * MLA_V, :] = (acc_ref[h * MLA_V:(h + 1) * MLA_V, :] * inv).astype(BF16)


def _dsa_attn(q_cat, k_cat, v_t, bias, batch, seq):
    t, nqk = q_cat.shape
    nv = v_t.shape[0]
    tq = min(seq, 256)
    sc = min(seq, 512)
    nq = seq // tq
    nc = seq // sc

    def last_chunk(j):
        return (j * tq + tq - 1) // sc

    return pl.pallas_call(
        functools.partial(_dsa_attn_kernel, sc=sc),
        name="dsa_attn",
        grid=(batch, nq, nc),
        in_specs=[pl.BlockSpec((tq, nqk), lambda b, j, c: (b * nq + j, 0)),
                  pl.BlockSpec((sc, nqk), lambda b, j, c: (b * nc + jnp.minimum(c, last_chunk(j)), 0)),
                  pl.BlockSpec((nv, sc), lambda b, j, c: (0, b * nc + jnp.minimum(c, last_chunk(j)))),
                  pl.BlockSpec((None, sc, tq), lambda b, j, c: (b, jnp.minimum(c, last_chunk(j)), j))],
        out_specs=pl.BlockSpec((nv, tq), lambda b, j, c: (0, b * nq + j)),
        out_shape=jax.ShapeDtypeStruct((nv, t), BF16),
        scratch_shapes=[pltpu.VMEM((nv, tq), F32),
                        pltpu.VMEM((MLA_HEADS, tq), F32),
                        pltpu.VMEM((MLA_HEADS, tq), F32)],
        compiler_params=_params(("parallel", "parallel", "arbitrary")),
    )(q_cat, k_cat, v_t, bias)


def _dsa_mixer_out(x2, cos_t, sin_t, w_in, kv_norm, w_uk, w_uv, batch, seq):
    t, d = x2.shape
    nq = MLA_HEADS * (MLA_NOPE + MLA_ROPE)
    o_c = nq
    o_kr = o_c + MLA_LATENT
    o_qi = o_kr + MLA_ROPE
    o_ki = o_qi + IDX_HEADS * IDX_DIM
    o_wi = o_ki + IDX_DIM
    wb = w_in.astype(BF16)
    wq = wb[:, :nq].reshape(d, MLA_HEADS, MLA_NOPE + MLA_ROPE)
    wq = jnp.pad(wq, ((0, 0), (0, 0), (0, MLA_QK_PAD - MLA_NOPE - MLA_ROPE))).reshape(d, MLA_HEADS * MLA_QK_PAD)
    pad_to_lanes = lambda w: jnp.pad(w, ((0, 0), (0, LANES - w.shape[1])))
    w_small = jnp.concatenate([wb[:, o_c:o_kr], pad_to_lanes(wb[:, o_kr:o_qi]), pad_to_lanes(wb[:, o_ki:o_wi]),
                               pad_to_lanes(wb[:, o_wi:])], axis=1)
    w_qi = wb[:, o_qi:o_ki]
    w_uk_all = jnp.transpose(w_uk, (1, 0, 2)).reshape(MLA_LATENT, MLA_HEADS * MLA_NOPE).astype(BF16)
    w_uv_t = jnp.transpose(w_uv, (0, 2, 1)).reshape(MLA_HEADS * MLA_V, MLA_LATENT).astype(BF16)

    q_cat = _dsa_q(x2, wq, cos_t, sin_t)
    ckv, kr, ki, wi = _dsa_small(x2, w_small, kv_norm, cos_t, sin_t)
    qi = _dsa_qi(x2, w_qi, cos_t, sin_t)
    k_cat, v_t = _dsa_kv(ckv, kr, w_uk_all, w_uv_t)

    tq = LANES
    nqb = seq // tq
    k_i = ki[:, :IDX_DIM].reshape(batch, seq, IDX_DIM)
    q_i_t = jnp.transpose(qi.reshape(batch, seq, IDX_HEADS, IDX_DIM), (0, 2, 1, 3))
    w_rows = jnp.transpose(wi[:, :IDX_HEADS].reshape(batch, nqb, tq, IDX_HEADS), (0, 1, 3, 2))
    w_rows = w_rows.reshape(batch, nqb, 1, IDX_HEADS * tq)
    n_sel = min(DSA_TOPK, seq // 4)
    bias = _dsa_select(k_i, q_i_t, w_rows, batch, seq, n_sel)
    o_t = _dsa_attn(q_cat, k_cat, v_t, bias, batch, seq)
    return jnp.transpose(o_t)


def _router_kernel(x_ref, w_ref, idx_ref, gate_ref):
    logits = jnp.dot(x_ref[...], w_ref[...], precision=lax.Precision.HIGHEST, preferred_element_type=F32)
    lane = lax.broadcasted_iota(I32, logits.shape, 1)
    lg = jnp.where(lane < MOE_EXPERTS, logits, -jnp.inf)
    v1 = jnp.max(lg, axis=-1, keepdims=True)
    i1 = jnp.min(jnp.where(lg == v1, lane, LANES), axis=-1, keepdims=True)
    lg2 = jnp.where(lane == i1, -jnp.inf, lg)
    v2 = jnp.max(lg2, axis=-1, keepdims=True)
    i2 = jnp.min(jnp.where(lg2 == v2, lane, LANES), axis=-1, keepdims=True)
    e2 = jnp.exp(v2 - v1)
    den = 1.0 + e2
    idx_ref[...] = jnp.where(lane == 0, i1, jnp.where(lane == 1, i2, 0))
    gate_ref[...] = jnp.where(lane == 0, 1.0 / den, jnp.where(lane == 1, e2 / den, 0.0))


def _router(x2, w_router):
    t, d = x2.shape
    wr = jnp.pad(w_router.astype(F32), ((0, 0), (0, LANES - w_router.shape[1])))
    tm = min(t, 512)
    return pl.pallas_call(
        _router_kernel,
        name="moe_router",
        grid=(t // tm,),
        in_specs=[pl.BlockSpec((tm, d), lambda i: (i, 0)),
                  pl.BlockSpec((d, LANES), lambda i: (0, 0))],
        out_specs=[pl.BlockSpec((tm, LANES), lambda i: (i, 0)),
                   pl.BlockSpec((tm, LANES), lambda i: (i, 0))],
        out_shape=[jax.ShapeDtypeStruct((t, LANES), I32),
                   jax.ShapeDtypeStruct((t, LANES), F32)],
        compiler_params=_params(("parallel",)),
    )(x2, wr)


def _row_copy(src_hbm, src_row, dst, dst_row, sem):
    return pltpu.make_async_copy(src_hbm.at[pl.ds(src_row, 1)], dst.at[pl.ds(dst_row, 1)], sem)


def _moe_dispatch_kernel(dest_ref, x_hbm, xs_in_hbm, xs_hbm, sem, *, tb):
    del xs_in_hbm
    row0 = pl.program_id(0) * tb

    def start(n, carry):
        _row_copy(x_hbm, row0 + n // MOE_TOP_K, xs_hbm, dest_ref[n], sem).start()
        return carry

    def wait(n, carry):
        _row_copy(x_hbm, row0 + n // MOE_TOP_K, xs_hbm, dest_ref[n], sem).wait()
        return carry

    lax.fori_loop(0, tb * MOE_TOP_K, start, 0)
    lax.fori_loop(0, tb * MOE_TOP_K, wait, 0)


def _moe_dispatch(x2, dest, n_slots):
    t, d = x2.shape
    tb = min(t, 512)
    return pl.pallas_call(
        functools.partial(_moe_dispatch_kernel, tb=tb),
        name="moe_dispatch",
        grid=(t // tb,),
        in_specs=[pl.BlockSpec((tb * MOE_TOP_K,), lambda i: (i,), memory_space=pltpu.SMEM),
                  pl.BlockSpec(memory_space=pl.ANY),
                  pl.BlockSpec(memory_space=pl.ANY)],
        out_specs=pl.BlockSpec(memory_space=pl.ANY),
        out_shape=jax.ShapeDtypeStruct((n_slots, d), F32),
        scratch_shapes=[pltpu.SemaphoreType.DMA],
        input_output_aliases={2: 0},
        compiler_params=pltpu.CompilerParams(dimension_semantics=("arbitrary",), has_side_effects=True),
    )(dest, x2, jnp.zeros((n_slots, d), F32))


def _moe_ffn_kernel(be_ref, nu_ref, x_ref, w1_ref, w3_ref, w2_ref, o_ref, xb_ref, acc_ref):
    i = pl.program_id(0)
    f = pl.program_id(1)
    nf = pl.num_programs(1)
    used = i < nu_ref[0]

    @pl.when(used & (f == 0))
    def _():
        xb_ref[...] = x_ref[...].astype(BF16)
        acc_ref[...] = jnp.zeros_like(acc_ref)

    @pl.when(used)
    def _():
        xb = xb_ref[...]
        gate = jnp.dot(xb, w1_ref[...], preferred_element_type=F32)
        up = jnp.dot(xb, w3_ref[...], preferred_element_type=F32)
        h = (jax.nn.silu(gate) * up).astype(BF16)
        acc_ref[...] += jnp.dot(h, w2_ref[...], preferred_element_type=F32)

    @pl.when(used & (f == nf - 1))
    def _():
        o_ref[...] = acc_ref[...]

    @pl.when(jnp.logical_not(used) & (f == nf - 1))
    def _():
        o_ref[...] = jnp.zeros_like(o_ref)


def _moe_ffn(xs, w1, w3, w2, blk_expert, n_used, tm):
    n_slots, d = xs.shape
    f_dim = w1.shape[2]
    tf = _ffn_tile(f_dim)
    nf = f_dim // tf

    def f_sel(i, f, nu):
        return jnp.where(i < nu[0], f, nf - 1)

    grid_spec = pltpu.PrefetchScalarGridSpec(
        num_scalar_prefetch=2,
        grid=(n_slots // tm, nf),
        in_specs=[pl.BlockSpec((tm, d), lambda i, f, be, nu: (i, 0)),
                  pl.BlockSpec((None, d, tf), lambda i, f, be, nu: (be[i], 0, f_sel(i, f, nu))),
                  pl.BlockSpec((None, d, tf), lambda i, f, be, nu: (be[i], 0, f_sel(i, f, nu))),
                  pl.BlockSpec((None, tf, d), lambda i, f, be, nu: (be[i], f_sel(i, f, nu), 0))],
        out_specs=pl.BlockSpec((tm, d), lambda i, f, be, nu: (i, 0)),
        scratch_shapes=[pltpu.VMEM((tm, d), BF16), pltpu.VMEM((tm, d), F32)],
    )
    return pl.pallas_call(
        _moe_ffn_kernel,
        name="moe_ffn",
        grid_spec=grid_spec,
        out_shape=jax.ShapeDtypeStruct((n_slots, d), F32),
        compiler_params=_params(("arbitrary", "arbitrary")),
    )(blk_expert, n_used, xs, w1, w3, w2)


def _moe_combine_kernel(dest_ref, y_hbm, x_ref, gate_ref, g_ref, b_ref, o_ref, y1_ref, y2_ref, sem, *, alpha, tb):
    def copies(tok):
        return (_row_copy(y_hbm, dest_ref[MOE_TOP_K * tok], y1_ref, tok, sem),
                _row_copy(y_hbm, dest_ref[MOE_TOP_K * tok + 1], y2_ref, tok, sem))

    def start(tok, carry):
        for cp in copies(tok):
            cp.start()
        return carry

    def wait(tok, carry):
        for cp in copies(tok):
            cp.wait()
        return carry

    lax.fori_loop(0, tb, start, 0)
    lax.fori_loop(0, tb, wait, 0)
    gates = gate_ref[...]
    ff = gates[:, 0:1] * y1_ref[...] + gates[:, 1:2] * y2_ref[...]
    o_ref[...] = _layer_norm(alpha * x_ref[...] + ff, g_ref[...], b_ref[...])


def _moe_combine(x2, y, dest, gates, g, b, alpha):
    t, d = x2.shape
    tb = min(t, 256)
    return pl.pallas_call(
        functools.partial(_moe_combine_kernel, alpha=alpha, tb=tb),
        name="moe_combine",
        grid=(t // tb,),
        in_specs=[pl.BlockSpec((tb * MOE_TOP_K,), lambda i: (i,), memory_space=pltpu.SMEM),
                  pl.BlockSpec(memory_space=pl.ANY),
                  pl.BlockSpec((tb, d), lambda i: (i, 0)),
                  pl.BlockSpec((tb, LANES), lambda i: (i, 0)),
                  pl.BlockSpec((1, d), lambda i: (0, 0)),
                  pl.BlockSpec((1, d), lambda i: (0, 0))],
        out_specs=pl.BlockSpec((tb, d), lambda i: (i, 0)),
        out_shape=jax.ShapeDtypeStruct((t, d), F32),
        scratch_shapes=[pltpu.VMEM((tb, d), F32), pltpu.VMEM((tb, d), F32), pltpu.SemaphoreType.DMA],
        compiler_params=_params(("arbitrary",)),
    )(dest, y, x2, gates, g[None, :], b[None, :])


def _moe_layer(x2, w_router, w1, w3, w2, g, b, alpha):
    t, d = x2.shape
    n = t * MOE_TOP_K
    tm = min(512, t)
    idx_pad, gates = _router(x2, w_router)
    e_flat = idx_pad[:, :MOE_TOP_K].reshape(n)
    onehot = (e_flat[:, None] == jnp.arange(MOE_EXPERTS, dtype=I32)[None, :]).astype(I32)
    csum = jnp.cumsum(onehot, axis=0)
    rank = jnp.sum((csum - onehot) * onehot, axis=1)
    counts = csum[-1]
    padded = (counts + tm - 1) // tm * tm
    pends = jnp.cumsum(padded)
    pstarts = pends - padded
    dest = (jnp.sum(onehot * pstarts[None, :], axis=1) + rank).astype(I32)
    n_tiles = (n + MOE_EXPERTS * (tm - 1) + tm - 1) // tm
    n_used = (pends[-1] // tm).astype(I32)
    blk_start = jnp.arange(n_tiles, dtype=I32) * tm
    blk_expert = jnp.minimum(jnp.searchsorted(pends, blk_start, side='right'), MOE_EXPERTS - 1).astype(I32)
    last_used = blk_expert[jnp.maximum(n_used - 1, 0)]
    blk_expert = jnp.where(jnp.arange(n_tiles) < n_used, blk_expert, last_used)

    xs = _moe_dispatch(x2, dest, n_tiles * tm)
    y = _moe_ffn(xs, w1.astype(BF16), w3.astype(BF16), w2.astype(BF16), blk_expert, n_used.reshape(1), tm)
    return _moe_combine(x2, y, dest, gates, g, b, alpha)


def kernel(x, p, positions, swa_w_qkv, swa_sinks, swa_w_o, dense_w1, dense_w3, dense_w2, dsa_w_in, dsa_kv_norm,
           dsa_w_uk, dsa_w_uv, dsa_w_o, moe_router, moe_w1, moe_w3, moe_w2, ln_g, ln_b, ple_w_p, ple_w_g):
    batch, seq, d = x.shape
    depth = ln_g.shape[0]
    alpha = (2.0 * depth) ** 0.25
    t = batch * seq
    x2 = x.reshape(t, d)
    cos_t, sin_t = _rope_tables(positions)
    for i in range(depth):
        j = i // 2
        if i % 2 == 0:
            q, k, v = _swa_qkv(x2, swa_w_qkv[j], cos_t, sin_t)
            o = _swa_attn(q, k, v, swa_sinks[j], batch, seq)
            x2 = _proj_ln(o, swa_w_o[j], x2, ln_g[i, 0], ln_b[i, 0], alpha)
            x2 = _ffn_dense(x2, dense_w1[j], dense_w3[j], dense_w2[j], ln_g[i, 1], ln_b[i, 1], alpha)
        else:
            o = _dsa_mixer_out(x2, cos_t, sin_t, dsa_w_in[j], dsa_kv_norm[j], dsa_w_uk[j], dsa_w_uv[j], batch, seq)
            x2 = _proj_ln(o, dsa_w_o[j], x2, ln_g[i, 0], ln_b[i, 0], alpha)
            x2 = _moe_layer(x2, moe_router[j], moe_w1[j], moe_w3[j], moe_w2[j], ln_g[i, 1], ln_b[i, 1], alpha)
        x2 = _ple(x2, p[i].reshape(t, -1), ple_w_g[i], ple_w_p[i])
    return x2.reshape(batch, seq, d)
```

```python
import functools

import jax
import jax.numpy as jnp
from jax import lax
from jax.experimental import pallas as pl
from jax.experimental.pallas import tpu as pltpu

F32 = jnp.float32
BF16 = jnp.bfloat16
I32 = jnp.int32

LANES = 128
ROPE_THETA = 10000.0
LN_EPS = 1e-5
RMS_EPS = 1e-6
SWA_HEADS = 32
SWA_KV_HEADS = 4
SWA_GROUP = SWA_HEADS // SWA_KV_HEADS
SWA_HEAD_DIM = 64
SWA_WINDOW = 128
MLA_HEADS = 16
MLA_NOPE = 128
MLA_ROPE = 64
MLA_V = 128
MLA_LATENT = 512
MLA_SCALE = (MLA_NOPE + MLA_ROPE) ** -0.5
MLA_QK_PAD = 256
LOG2_E = 1.4426950408889634
IDX_HEADS = 16
IDX_DIM = 64
IDX_ROPE = 32
DSA_TOPK = 256
MOE_EXPERTS = 8
MOE_TOP_K = 2
NEG_BIG = -1e30
M_INIT = -1e20
VMEM_LIMIT = 56 * 1024 * 1024


def _params(sem, vmem=VMEM_LIMIT):
    return pltpu.CompilerParams(dimension_semantics=sem, vmem_limit_bytes=vmem)


def _layer_norm(y, g, b):
    mu = jnp.mean(y, axis=-1, keepdims=True)
    d = y - mu
    var = jnp.mean(d * d, axis=-1, keepdims=True)
    return d * lax.rsqrt(var + LN_EPS) * g + b


def _rope_lanes(a, cs, sn, half):
    lane = lax.broadcasted_iota(I32, a.shape, 1)
    first = (lane % 64) < half
    rot = jnp.where(first, pltpu.roll(a, LANES - half, 1), pltpu.roll(a, half, 1))
    return a * cs + rot * sn


def _rope_table_kernel(pos_ref, invf_ref, sgn_ref, cos_ref, sin_ref):
    ang = pos_ref[...].astype(F32) * invf_ref[...]
    cos_ref[...] = jnp.cos(ang)
    sin_ref[...] = jnp.sin(ang) * sgn_ref[...]


def _rope_tables(positions):
    t = positions.size
    pos = positions.reshape(t, 1).astype(I32)
    lane = jnp.arange(LANES)
    in64 = lane % 64
    f64 = ROPE_THETA ** (-jnp.arange(0, 64, 2, dtype=F32) / 64)
    f32_ = ROPE_THETA ** (-jnp.arange(0, IDX_ROPE, 2, dtype=F32) / IDX_ROPE)
    invf_a = f64[in64 % 32]
    sgn_a = jnp.where(in64 < 32, -1.0, 1.0).astype(F32)
    invf_b = jnp.where(in64 < IDX_ROPE, f32_[in64 % (IDX_ROPE // 2)], 0.0)
    sgn_b = jnp.where(in64 < IDX_ROPE // 2, -1.0, 1.0).astype(F32)
    invf = jnp.concatenate([invf_a, invf_b])[None, :]
    sgn = jnp.concatenate([sgn_a, sgn_b])[None, :]
    tm = min(t, 1024)
    return pl.pallas_call(
        _rope_table_kernel,
        name="rope_tables",
        grid=(t // tm,),
        in_specs=[pl.BlockSpec((tm, 1), lambda i: (i, 0)),
                  pl.BlockSpec((1, 2 * LANES), lambda i: (0, 0)),
                  pl.BlockSpec((1, 2 * LANES), lambda i: (0, 0))],
        out_specs=[pl.BlockSpec((tm, 2 * LANES), lambda i: (i, 0)),
                   pl.BlockSpec((tm, 2 * LANES), lambda i: (i, 0))],
        out_shape=[jax.ShapeDtypeStruct((t, 2 * LANES), F32)] * 2,
        compiler_params=_params(("parallel",)),
    )(pos, invf, sgn)


def _swa_qkv_kernel(x_ref, w_ref, cos_ref, sin_ref, q_ref, k_ref, v_ref, *, nq, nk):
    xb = x_ref[...].astype(BF16)
    acc = jnp.dot(xb, w_ref[...], preferred_element_type=F32)
    cs = cos_ref[:, :LANES]
    sn = sin_ref[:, :LANES]
    scale = SWA_HEAD_DIM ** -0.5
    for c in range(nq // LANES):
        a = acc[:, c * LANES:(c + 1) * LANES]
        q_ref[:, c * LANES:(c + 1) * LANES] = (_rope_lanes(a, cs, sn, 32) * scale).astype(BF16)
    for c in range(nk // LANES):
        a = acc[:, nq + c * LANES:nq + (c + 1) * LANES]
        k_ref[:, c * LANES:(c + 1) * LANES] = _rope_lanes(a, cs, sn, 32).astype(BF16)
    v_ref[...] = acc[:, nq + nk:].astype(BF16)


def _swa_qkv(x2, w_qkv, cos_t, sin_t):
    t, d = x2.shape
    nq = SWA_HEADS * SWA_HEAD_DIM
    nk = SWA_KV_HEADS * SWA_HEAD_DIM
    tm = min(t, 512)
    return pl.pallas_call(
        functools.partial(_swa_qkv_kernel, nq=nq, nk=nk),
        name="swa_qkv",
        grid=(t // tm,),
        in_specs=[pl.BlockSpec((tm, d), lambda i: (i, 0)),
                  pl.BlockSpec((d, nq + 2 * nk), lambda i: (0, 0)),
                  pl.BlockSpec((tm, 2 * LANES), lambda i: (i, 0)),
                  pl.BlockSpec((tm, 2 * LANES), lambda i: (i, 0))],
        out_specs=[pl.BlockSpec((tm, nq), lambda i: (i, 0)),
                   pl.BlockSpec((tm, nk), lambda i: (i, 0)),
                   pl.BlockSpec((tm, nk), lambda i: (i, 0))],
        out_shape=[jax.ShapeDtypeStruct((t, nq), BF16),
                   jax.ShapeDtypeStruct((t, nk), BF16),
                   jax.ShapeDtypeStruct((t, nk), BF16)],
        compiler_params=_params(("parallel",)),
    )(x2, w_qkv.astype(BF16), cos_t, sin_t)


def _swa_attn_kernel(sink_ref, q_ref, kp_ref, kc_ref, vp_ref, vc_ref, o_ref):
    blk = pl.program_id(1)
    bq = q_ref.shape[0]
    dh = SWA_HEAD_DIM
    rows = SWA_GROUP * bq
    qi = lax.broadcasted_iota(I32, (rows, 2 * bq), 0) % bq
    kj = lax.broadcasted_iota(I32, (rows, 2 * bq), 1)
    rel = qi + bq - kj
    allowed = (rel >= 0) & (rel < SWA_WINDOW) & ((kj >= bq) | (blk > 0))
    for kh in range(SWA_KV_HEADS):
        kcat = jnp.concatenate([kp_ref[:, kh * dh:(kh + 1) * dh], kc_ref[:, kh * dh:(kh + 1) * dh]], axis=0)
        vcat = jnp.concatenate([vp_ref[:, kh * dh:(kh + 1) * dh], vc_ref[:, kh * dh:(kh + 1) * dh]], axis=0)
        qs = jnp.concatenate(
            [q_ref[:, (kh * SWA_GROUP + g) * dh:(kh * SWA_GROUP + g + 1) * dh] for g in range(SWA_GROUP)], axis=0)
        sink = jnp.concatenate(
            [jnp.full((bq, 1), sink_ref[kh * SWA_GROUP + g], F32) for g in range(SWA_GROUP)], axis=0)
        logits = lax.dot_general(qs, kcat, (((1,), (1,)), ((), ())), preferred_element_type=F32)
        logits = jnp.where(allowed, logits, NEG_BIG)
        m = jnp.maximum(jnp.max(logits, axis=-1, keepdims=True), sink)
        p = jnp.exp(logits - m)
        denom = jnp.sum(p, axis=-1, keepdims=True) + jnp.exp(sink - m)
        o = jnp.dot(p.astype(BF16), vcat, preferred_element_type=F32) / denom
        for g in range(SWA_GROUP):
            h = kh * SWA_GROUP + g
            o_ref[:, h * dh:(h + 1) * dh] = o[g * bq:(g + 1) * bq].astype(BF16)


def _swa_attn(q, k, v, sinks, batch, seq):
    t, nq = q.shape
    nk = k.shape[1]
    bq = SWA_WINDOW
    nb = seq // bq

    def cur(b, i):
        return (b * nb + i, 0)

    def prev(b, i):
        return (b * nb + jnp.maximum(i - 1, 0), 0)

    return pl.pallas_call(
        _swa_attn_kernel,
        name="swa_attn",
        grid=(batch, nb),
        in_specs=[pl.BlockSpec(memory_space=pltpu.SMEM),
                  pl.BlockSpec((bq, nq), cur),
                  pl.BlockSpec((bq, nk), prev),
                  pl.BlockSpec((bq, nk), cur),
                  pl.BlockSpec((bq, nk), prev),
                  pl.BlockSpec((bq, nk), cur)],
        out_specs=pl.BlockSpec((bq, nq), cur),
        out_shape=jax.ShapeDtypeStruct((t, nq), BF16),
        compiler_params=_params(("parallel", "parallel")),
    )(sinks.astype(F32), q, k, k, v, v)


def _proj_ln_kernel(a_ref, w_ref, res_ref, g_ref, b_ref, o_ref, *, alpha):
    y = alpha * res_ref[...] + jnp.dot(a_ref[...], w_ref[...], preferred_element_type=F32)
    o_ref[...] = _layer_norm(y, g_ref[...], b_ref[...])


def _proj_ln(a, w, res, g, b, alpha):
    t, k = a.shape
    d = w.shape[1]
    tm = min(t, 512)
    return pl.pallas_call(
        functools.partial(_proj_ln_kernel, alpha=alpha),
        name="proj_ln",
        grid=(t // tm,),
        in_specs=[pl.BlockSpec((tm, k), lambda i: (i, 0)),
                  pl.BlockSpec((k, d), lambda i: (0, 0)),
                  pl.BlockSpec((tm, d), lambda i: (i, 0)),
                  pl.BlockSpec((1, d), lambda i: (0, 0)),
                  pl.BlockSpec((1, d), lambda i: (0, 0))],
        out_specs=pl.BlockSpec((tm, d), lambda i: (i, 0)),
        out_shape=jax.ShapeDtypeStruct((t, d), F32),
        compiler_params=_params(("parallel",)),
    )(a, w.astype(BF16), res, g[None, :], b[None, :])


def _ffn_dense_kernel(x_ref, w1_ref, w3_ref, w2_ref, g_ref, b_ref, o_ref, xb_ref, acc_ref, *, alpha):
    f = pl.program_id(1)

    @pl.when(f == 0)
    def _():
        xb_ref[...] = x_ref[...].astype(BF16)
        acc_ref[...] = jnp.zeros_like(acc_ref)

    xb = xb_ref[...]
    gate = jnp.dot(xb, w1_ref[...], preferred_element_type=F32)
    up = jnp.dot(xb, w3_ref[...], preferred_element_type=F32)
    h = (jax.nn.silu(gate) * up).astype(BF16)
    acc_ref[...] += jnp.dot(h, w2_ref[...], preferred_element_type=F32)

    @pl.when(f == pl.num_programs(1) - 1)
    def _():
        o_ref[...] = _layer_norm(alpha * x_ref[...] + acc_ref[...], g_ref[...], b_ref[...])


def _ffn_tile(f_dim):
    for tf in (512, 256, 128):
        if f_dim % tf == 0:
            return tf
    return f_dim


def _ffn_dense(x2, w1, w3, w2, g, b, alpha):
    t, d = x2.shape
    f_dim = w1.shape[1]
    tm = min(t, 512)
    tf = _ffn_tile(f_dim)
    return pl.pallas_call(
        functools.partial(_ffn_dense_kernel, alpha=alpha),
        name="ffn_dense",
        grid=(t // tm, f_dim // tf),
        in_specs=[pl.BlockSpec((tm, d), lambda i, f: (i, 0)),
                  pl.BlockSpec((d, tf), lambda i, f: (0, f)),
                  pl.BlockSpec((d, tf), lambda i, f: (0, f)),
                  pl.BlockSpec((tf, d), lambda i, f: (f, 0)),
                  pl.BlockSpec((1, d), lambda i, f: (0, 0)),
                  pl.BlockSpec((1, d), lambda i, f: (0, 0))],
        out_specs=pl.BlockSpec((tm, d), lambda i, f: (i, 0)),
        out_shape=jax.ShapeDtypeStruct((t, d), F32),
        scratch_shapes=[pltpu.VMEM((tm, d), BF16), pltpu.VMEM((tm, d), F32)],
        compiler_params=_params(("parallel", "arbitrary")),
    )(x2, w1.astype(BF16), w3.astype(BF16), w2.astype(BF16), g[None, :], b[None, :])


def _ple_kernel(x_ref, p_ref, wg_ref, wp_ref, o_ref):
    x = x_ref[...]
    gate = jax.nn.sigmoid(jnp.dot(x.astype(BF16), wg_ref[...], preferred_element_type=F32))
    emb = jnp.dot(p_ref[...].astype(BF16), wp_ref[...], preferred_element_type=F32)
    o_ref[...] = x + gate * emb


def _ple(x2, p2, w_g, w_p):
    t, d = x2.shape
    pd = p2.shape[1]
    tm = min(t, 512)
    return pl.pallas_call(
        _ple_kernel,
        name="ple",
        grid=(t // tm,),
        in_specs=[pl.BlockSpec((tm, d), lambda i: (i, 0)),
                  pl.BlockSpec((tm, pd), lambda i: (i, 0)),
                  pl.BlockSpec((d, d), lambda i: (0, 0)),
                  pl.BlockSpec((pd, d), lambda i: (0, 0))],
        out_specs=pl.BlockSpec((tm, d), lambda i: (i, 0)),
        out_shape=jax.ShapeDtypeStruct((t, d), F32),
        compiler_params=_params(("parallel",)),
    )(x2, p2, w_g.astype(BF16), w_p.astype(BF16))


def _dsa_q_kernel(x_ref, w_ref, cos_ref, sin_ref, q_ref):
    xb = x_ref[...].astype(BF16)
    acc = jnp.dot(xb, w_ref[...], preferred_element_type=F32)
    cs = cos_ref[:, :LANES]
    sn = sin_ref[:, :LANES]
    scale = MLA_SCALE * LOG2_E
    for h in range(MLA_HEADS):
        base = h * MLA_QK_PAD
        q_ref[:, base:base + LANES] = (acc[:, base:base + LANES] * scale).astype(BF16)
        a = acc[:, base + LANES:base + 2 * LANES]
        q_ref[:, base + LANES:base + 2 * LANES] = (_rope_lanes(a, cs, sn, 32) * scale).astype(BF16)


def _dsa_q(x2, w_q, cos_t, sin_t):
    t, d = x2.shape
    n = w_q.shape[1]
    tm = min(t, 512)
    return pl.pallas_call(
        _dsa_q_kernel,
        name="dsa_q",
        grid=(t // tm,),
        in_specs=[pl.BlockSpec((tm, d), lambda i: (i, 0)),
                  pl.BlockSpec((d, n), lambda i: (0, 0)),
                  pl.BlockSpec((tm, 2 * LANES), lambda i: (i, 0)),
                  pl.BlockSpec((tm, 2 * LANES), lambda i: (i, 0))],
        out_specs=pl.BlockSpec((tm, n), lambda i: (i, 0)),
        out_shape=jax.ShapeDtypeStruct((t, n), BF16),
        compiler_params=_params(("parallel",)),
    )(x2, w_q, cos_t, sin_t)


def _dsa_small_kernel(x_ref, w_ref, cos_ref, sin_ref, kvn_ref, ckv_ref, kr_ref, ki_ref, wi_ref):
    xb = x_ref[...].astype(BF16)
    acc = jnp.dot(xb, w_ref[...], preferred_element_type=F32)
    c = acc[:, :MLA_LATENT]
    ms = jnp.mean(c * c, axis=-1, keepdims=True)
    ckv_ref[...] = (c * lax.rsqrt(ms + RMS_EPS) * kvn_ref[...]).astype(BF16)
    o = MLA_LATENT
    kr_ref[...] = _rope_lanes(acc[:, o:o + LANES], cos_ref[:, :LANES], sin_ref[:, :LANES], 32).astype(BF16)
    ki_ref[...] = _rope_lanes(acc[:, o + LANES:o + 2 * LANES], cos_ref[:, LANES:], sin_ref[:, LANES:],
                              IDX_ROPE // 2).astype(BF16)
    wi_ref[...] = acc[:, o + 2 * LANES:o + 3 * LANES] * (IDX_HEADS ** -0.5)


def _dsa_small(x2, w_small, kv_norm, cos_t, sin_t):
    t, d = x2.shape
    n = w_small.shape[1]
    tm = min(t, 512)
    return pl.pallas_call(
        _dsa_small_kernel,
        name="dsa_small",
        grid=(t // tm,),
        in_specs=[pl.BlockSpec((tm, d), lambda i: (i, 0)),
                  pl.BlockSpec((d, n), lambda i: (0, 0)),
                  pl.BlockSpec((tm, 2 * LANES), lambda i: (i, 0)),
                  pl.BlockSpec((tm, 2 * LANES), lambda i: (i, 0)),
                  pl.BlockSpec((1, MLA_LATENT), lambda i: (0, 0))],
        out_specs=[pl.BlockSpec((tm, MLA_LATENT), lambda i: (i, 0)),
                   pl.BlockSpec((tm, LANES), lambda i: (i, 0)),
                   pl.BlockSpec((tm, LANES), lambda i: (i, 0)),
                   pl.BlockSpec((tm, LANES), lambda i: (i, 0))],
        out_shape=[jax.ShapeDtypeStruct((t, MLA_LATENT), BF16),
                   jax.ShapeDtypeStruct((t, LANES), BF16),
                   jax.ShapeDtypeStruct((t, LANES), BF16),
                   jax.ShapeDtypeStruct((t, LANES), F32)],
        compiler_params=_params(("parallel",)),
    )(x2, w_small, cos_t, sin_t, kv_norm[None, :].astype(F32))


def _dsa_qi_kernel(x_ref, w_ref, cos_ref, sin_ref, q_ref):
    xb = x_ref[...].astype(BF16)
    acc = jnp.dot(xb, w_ref[...], preferred_element_type=F32)
    cs = cos_ref[:, LANES:]
    sn = sin_ref[:, LANES:]
    scale = IDX_DIM ** -0.5
    for c in range(acc.shape[1] // LANES):
        a = acc[:, c * LANES:(c + 1) * LANES]
        q_ref[:, c * LANES:(c + 1) * LANES] = (_rope_lanes(a, cs, sn, IDX_ROPE // 2) * scale).astype(BF16)


def _dsa_qi(x2, w_qi, cos_t, sin_t):
    t, d = x2.shape
    n = w_qi.shape[1]
    tm = min(t, 512)
    return pl.pallas_call(
        _dsa_qi_kernel,
        name="dsa_qi",
        grid=(t // tm,),
        in_specs=[pl.BlockSpec((tm, d), lambda i: (i, 0)),
                  pl.BlockSpec((d, n), lambda i: (0, 0)),
                  pl.BlockSpec((tm, 2 * LANES), lambda i: (i, 0)),
                  pl.BlockSpec((tm, 2 * LANES), lambda i: (i, 0))],
        out_specs=pl.BlockSpec((tm, n), lambda i: (i, 0)),
        out_shape=jax.ShapeDtypeStruct((t, n), BF16),
        compiler_params=_params(("parallel",)),
    )(x2, w_qi, cos_t, sin_t)


def _dsa_kv_kernel(c_ref, kr_ref, wk_ref, wvt_ref, k_ref, vt_ref):
    c = c_ref[...]
    kn = jnp.dot(c, wk_ref[...], preferred_element_type=F32)
    kr = kr_ref[...]
    for h in range(MLA_HEADS):
        base = h * MLA_QK_PAD
        k_ref[:, base:base + LANES] = kn[:, h * LANES:(h + 1) * LANES].astype(BF16)
        k_ref[:, base + LANES:base + 2 * LANES] = kr
    vt_ref[...] = lax.dot_general(wvt_ref[...], c, (((1,), (1,)), ((), ())),
                                  preferred_element_type=F32).astype(BF16)


def _dsa_kv(ckv, kr, w_uk_all, w_uv_t):
    t, c_dim = ckv.shape
    nk = MLA_HEADS * MLA_QK_PAD
    nv = w_uv_t.shape[0]
    tm = min(t, 512)
    return pl.pallas_call(
        _dsa_kv_kernel,
        name="dsa_kv",
        grid=(t // tm,),
        in_specs=[pl.BlockSpec((tm, c_dim), lambda i: (i, 0)),
                  pl.BlockSpec((tm, LANES), lambda i: (i, 0)),
                  pl.BlockSpec((c_dim, MLA_HEADS * MLA_NOPE), lambda i: (0, 0)),
                  pl.BlockSpec((nv, c_dim), lambda i: (0, 0))],
        out_specs=[pl.BlockSpec((tm, nk), lambda i: (i, 0)),
                   pl.BlockSpec((nv, tm), lambda i: (0, i))],
        out_shape=[jax.ShapeDtypeStruct((t, nk), BF16),
                   jax.ShapeDtypeStruct((nv, t), BF16)],
        compiler_params=_params(("parallel",)),
    )(ckv, kr, w_uk_all, w_uv_t)


def _float_sort_key(s):
    bits = lax.bitcast_convert_type(s, I32)
    return bits ^ ((bits >> 31) & jnp.int32(0x7FFFFFFF))


def _dsa_select_kernel(k_ref, q_ref, w_ref, bias_ref, key_ref, *, sc, cc, n_sel):
    j = pl.program_id(1)
    s_len, tq = key_ref.shape
    n_chunks = s_len // sc
    q_lo = j * tq
    n_need = (q_lo + tq + sc - 1) // sc
    qf = q_ref[...].reshape(IDX_HEADS * tq, IDX_DIM)
    w_row = w_ref[...]
    q_pos = q_lo + lax.broadcasted_iota(I32, (sc, tq), 1)
    k_off = lax.broadcasted_iota(I32, (sc, tq), 0)
    min_key = jnp.int32(-2 ** 31)

    def score_chunk(c, carry):
        row0 = pl.multiple_of(c * sc, sc)
        kc = k_ref[pl.ds(row0, sc), :]
        d = lax.dot_general(kc, qf, (((1,), (1,)), ((), ())), preferred_element_type=F32)
        s = jnp.zeros((sc, tq), F32)
        for h in range(IDX_HEADS):
            s = s + jnp.maximum(d[:, h * tq:(h + 1) * tq], 0.0) * w_row[:, h * tq:(h + 1) * tq]
        causal = (row0 + k_off) <= q_pos
        key_ref[pl.ds(row0, sc), :] = jnp.where(causal, _float_sort_key(s), min_key)
        return carry

    lax.fori_loop(0, n_need, score_chunk, 0)

    n_cnt = (n_need * sc + cc - 1) // cc

    def fill_chunk(c, carry):
        row0 = pl.multiple_of(c * sc, sc)
        key_ref[pl.ds(row0, sc), :] = jnp.full((sc, tq), min_key, I32)
        return carry

    lax.fori_loop(n_need, n_cnt * (cc // sc), fill_chunk, 0)

    def count_ge(cand):
        def body(c, cnt):
            row0 = pl.multiple_of(c * cc, cc)
            hit = jnp.where(key_ref[pl.ds(row0, cc), :] >= cand, 1, 0).astype(I32)
            return cnt + jnp.sum(hit.reshape(cc // 32, 32, tq), axis=0)

        cnt = lax.fori_loop(0, n_cnt, body, jnp.zeros((32, tq), I32))
        return jnp.sum(cnt, axis=0, keepdims=True)

    zero = jnp.zeros((1, tq), I32)
    thr = jnp.where(count_ge(zero) >= n_sel, zero, jnp.full((1, tq), min_key, I32))

    def bit_step(it, thr):
        cand = thr + (jnp.int32(1) << (30 - it))
        return jnp.where(count_ge(cand) >= n_sel, cand, thr)

    thr = lax.fori_loop(0, 31, bit_step, thr)

    def write_chunk(c, carry):
        row0 = pl.multiple_of(c * sc, sc)
        keep = (key_ref[pl.ds(row0, sc), :] >= thr) & ((row0 + k_off) <= q_pos)
        bias_ref[pl.ds(row0, sc), :] = jnp.where(keep, 0.0, NEG_BIG).astype(F32)
        return carry

    def mask_chunk(c, carry):
        row0 = pl.multiple_of(c * sc, sc)
        bias_ref[pl.ds(row0, sc), :] = jnp.full((sc, tq), NEG_BIG, F32)
        return carry

    lax.fori_loop(0, n_need, write_chunk, 0)
    lax.fori_loop(n_need, n_chunks, mask_chunk, 0)


def _dsa_select(k_i, q_i_t, w_rows, batch, seq, n_sel):
    tq = LANES
    nq = seq // tq
    sc = min(seq, 256)
    cc = min(seq, 1024)
    return pl.pallas_call(
        functools.partial(_dsa_select_kernel, sc=sc, cc=cc, n_sel=n_sel),
        name="dsa_select",
        grid=(batch, nq),
        in_specs=[pl.BlockSpec((None, seq, IDX_DIM), lambda b, j: (b, 0, 0)),
                  pl.BlockSpec((None, IDX_HEADS, tq, IDX_DIM), lambda b, j: (b, 0, j, 0)),
                  pl.BlockSpec((None, None, 1, IDX_HEADS * tq), lambda b, j: (b, j, 0, 0))],
        out_specs=pl.BlockSpec((None, seq, tq), lambda b, j: (b, 0, j)),
        out_shape=jax.ShapeDtypeStruct((batch, seq, seq), F32),
        scratch_shapes=[pltpu.VMEM((seq, tq), I32)],
        compiler_params=_params(("parallel", "parallel")),
    )(k_i, q_i_t, w_rows)


def _dsa_attn_kernel(q_ref, k_ref, vt_ref, bias_ref, o_ref, m_ref, l_ref, *head_refs, sc):
    s_refs = head_refs[:MLA_HEADS]
    acc_refs = head_refs[MLA_HEADS:]
    j = pl.program_id(1)
    c = pl.program_id(2)
    tq = q_ref.shape[0]
    last = (j * tq + tq - 1) // sc

    @pl.when(c == 0)
    def _():
        for acc_ref in acc_refs:
            acc_ref[...] = jnp.zeros_like(acc_ref)
        m_ref[...] = jnp.full_like(m_ref, M_INIT)
        l_ref[...] = jnp.zeros_like(l_ref)

    @pl.when(c <= last)
    def _():
        m_old = m_ref[...]
        chunk_max = []
        for h in range(MLA_HEADS):
            kh = k_ref[:, h * MLA_QK_PAD:(h + 1) * MLA_QK_PAD]
            qh = q_ref[:, h * MLA_QK_PAD:(h + 1) * MLA_QK_PAD]
            s = lax.dot_general(kh, qh, (((1,), (1,)), ((), ())), preferred_element_type=F32) + bias_ref[...]
            s_refs[h][...] = s
            chunk_max.append(jnp.max(s, axis=0, keepdims=True))
        m_new = jnp.maximum(m_old, jnp.concatenate(chunk_max, axis=0))
        a = jnp.exp2(m_old - m_new)
        m_ref[...] = m_new
        chunk_sum = []
        for h in range(MLA_HEADS):
            p = jnp.exp2(s_refs[h][...] - m_new[h:h + 1, :])
            chunk_sum.append(jnp.sum(p, axis=0, keepdims=True))
            pv = jnp.dot(vt_ref[h * MLA_V:(h + 1) * MLA_V, :], p.astype(BF16), preferred_element_type=F32)
            acc_refs[h][...] = a[h:h + 1, :] * acc_refs[h][...] + pv
        l_ref[...] = a * l_ref[...] + jnp.concatenate(chunk_sum, axis=0)

    @pl.when(c == pl.num_programs(2) - 1)
    def _():
        inv = 1.0 / l_ref[...]
        for h in range(MLA_HEADS):
            o_ref[h * MLA_V:(h + 1) * MLA_V, :] = (acc_refs[h][...] * inv[h:h + 1, :]).astype(BF16)


def _dsa_attn(q_cat, k_cat, v_t, bias, batch, seq):
    t, nqk = q_cat.shape
    nv = v_t.shape[0]
    tq = min(seq, 256)
    sc = min(seq, 512)
    nq = seq // tq
    nc = seq // sc

    def last_chunk(j):
        return (j * tq + tq - 1) // sc

    return pl.pallas_call(
        functools.partial(_dsa_attn_kernel, sc=sc),
        name="dsa_attn",
        grid=(batch, nq, nc),
        in_specs=[pl.BlockSpec((tq, nqk), lambda b, j, c: (b * nq + j, 0)),
                  pl.BlockSpec((sc, nqk), lambda b, j, c: (b * nc + jnp.minimum(c, last_chunk(j)), 0)),
                  pl.BlockSpec((nv, sc), lambda b, j, c: (0, b * nc + jnp.minimum(c, last_chunk(j)))),
                  pl.BlockSpec((None, sc, tq), lambda b, j, c: (b, jnp.minimum(c, last_chunk(j)), j))],
        out_specs=pl.BlockSpec((nv, tq), lambda b, j, c: (0, b * nq + j)),
        out_shape=jax.ShapeDtypeStruct((nv, t), BF16),
        scratch_shapes=([pltpu.VMEM((MLA_HEADS, tq), F32), pltpu.VMEM((MLA_HEADS, tq), F32)]
                        + [pltpu.VMEM((sc, tq), F32)] * MLA_HEADS
                        + [pltpu.VMEM((MLA_V, tq), F32)] * MLA_HEADS),
        compiler_params=_params(("parallel", "parallel", "arbitrary")),
    )(q_cat, k_cat, v_t, bias)


def _dsa_mixer_out(x2, cos_t, sin_t, w_in, kv_norm, w_uk, w_uv, batch, seq):
    t, d = x2.shape
    nq = MLA_HEADS * (MLA_NOPE + MLA_ROPE)
    o_c = nq
    o_kr = o_c + MLA_LATENT
    o_qi = o_kr + MLA_ROPE
    o_ki = o_qi + IDX_HEADS * IDX_DIM
    o_wi = o_ki + IDX_DIM
    wb = w_in.astype(BF16)
    wq = wb[:, :nq].reshape(d, MLA_HEADS, MLA_NOPE + MLA_ROPE)
    wq = jnp.pad(wq, ((0, 0), (0, 0), (0, MLA_QK_PAD - MLA_NOPE - MLA_ROPE))).reshape(d, MLA_HEADS * MLA_QK_PAD)
    pad_to_lanes = lambda w: jnp.pad(w, ((0, 0), (0, LANES - w.shape[1])))
    w_small = jnp.concatenate([wb[:, o_c:o_kr], pad_to_lanes(wb[:, o_kr:o_qi]), pad_to_lanes(wb[:, o_ki:o_wi]),
                               pad_to_lanes(wb[:, o_wi:])], axis=1)
    w_qi = wb[:, o_qi:o_ki]
    w_uk_all = jnp.transpose(w_uk, (1, 0, 2)).reshape(MLA_LATENT, MLA_HEADS * MLA_NOPE).astype(BF16)
    w_uv_t = jnp.transpose(w_uv, (0, 2, 1)).reshape(MLA_HEADS * MLA_V, MLA_LATENT).astype(BF16)

    q_cat = _dsa_q(x2, wq, cos_t, sin_t)
    ckv, kr, ki, wi = _dsa_small(x2, w_small, kv_norm, cos_t, sin_t)
    qi = _dsa_qi(x2, w_qi, cos_t, sin_t)
    k_cat, v_t = _dsa_kv(ckv, kr, w_uk_all, w_uv_t)

    tq = LANES
    nqb = seq // tq
    k_i = ki[:, :IDX_DIM].reshape(batch, seq, IDX_DIM)
    q_i_t = jnp.transpose(qi.reshape(batch, seq, IDX_HEADS, IDX_DIM), (0, 2, 1, 3))
    w_rows = jnp.transpose(wi[:, :IDX_HEADS].reshape(batch, nqb, tq, IDX_HEADS), (0, 1, 3, 2))
    w_rows = w_rows.reshape(batch, nqb, 1, IDX_HEADS * tq)
    n_sel = min(DSA_TOPK, seq // 4)
    bias = _dsa_select(k_i, q_i_t, w_rows, batch, seq, n_sel)
    o_t = _dsa_attn(q_cat, k_cat, v_t, bias, batch, seq)
    return jnp.transpose(o_t)


def _router_kernel(x_ref, w_ref, idx_ref, gate_ref):
    logits = jnp.dot(x_ref[...], w_ref[...], precision=lax.Precision.HIGHEST, preferred_element_type=F32)
    lane = lax.broadcasted_iota(I32, logits.shape, 1)
    lg = jnp.where(lane < MOE_EXPERTS, logits, -jnp.inf)
    v1 = jnp.max(lg, axis=-1, keepdims=True)
    i1 = jnp.min(jnp.where(lg == v1, lane, LANES), axis=-1, keepdims=True)
    lg2 = jnp.where(lane == i1, -jnp.inf, lg)
    v2 = jnp.max(lg2, axis=-1, keepdims=True)
    i2 = jnp.min(jnp.where(lg2 == v2, lane, LANES), axis=-1, keepdims=True)
    e2 = jnp.exp(v2 - v1)
    den = 1.0 + e2
    idx_ref[...] = jnp.where(lane == 0, i1, jnp.where(lane == 1, i2, 0))
    gate_ref[...] = jnp.where(lane == 0, 1.0 / den, jnp.where(lane == 1, e2 / den, 0.0))


def _router(x2, w_router):
    t, d = x2.shape
    wr = jnp.pad(w_router.astype(F32), ((0, 0), (0, LANES - w_router.shape[1])))
    tm = min(t, 512)
    return pl.pallas_call(
        _router_kernel,
        name="moe_router",
        grid=(t // tm,),
        in_specs=[pl.BlockSpec((tm, d), lambda i: (i, 0)),
                  pl.BlockSpec((d, LANES), lambda i: (0, 0))],
        out_specs=[pl.BlockSpec((tm, LANES), lambda i: (i, 0)),
                   pl.BlockSpec((tm, LANES), lambda i: (i, 0))],
        out_shape=[jax.ShapeDtypeStruct((t, LANES), I32),
                   jax.ShapeDtypeStruct((t, LANES), F32)],
        compiler_params=_params(("parallel",)),
    )(x2, wr)


def _row_copy(src_hbm, src_row, dst, dst_row, sem):
    return pltpu.make_async_copy(src_hbm.at[pl.ds(src_row, 1)], dst.at[pl.ds(dst_row, 1)], sem)


def _moe_ffn_kernel(be_ref, nu_ref, tok_ref, tok_next_ref, x_hbm, w1_ref, w3_ref, w2_ref, o_ref,
                    xg_ref, xb_ref, acc_ref, sem):
    i = pl.program_id(0)
    f = pl.program_id(1)
    nf = pl.num_programs(1)
    n_used = nu_ref[0]
    used = i < n_used
    tm = xb_ref.shape[0]
    slot = i % 2

    def start_gather(ids_ref, buf):
        def body(r, carry):
            _row_copy(x_hbm, ids_ref[r], xg_ref.at[buf], r, sem.at[buf]).start()
            return carry

        lax.fori_loop(0, tm, body, 0, unroll=8)

    @pl.when(used & (f == 0))
    def _():
        @pl.when(i == 0)
        def _():
            start_gather(tok_ref, 0)

        pltpu.make_async_copy(x_hbm.at[pl.ds(0, tm)], xg_ref.at[slot], sem.at[slot]).wait()

        @pl.when(i + 1 < n_used)
        def _():
            start_gather(tok_next_ref, 1 - slot)

        xb_ref[...] = xg_ref[slot].astype(BF16)
        acc_ref[...] = jnp.zeros_like(acc_ref)

    @pl.when(used)
    def _():
        xb = xb_ref[...]
        gate = jnp.dot(xb, w1_ref[...], preferred_element_type=F32)
        up = jnp.dot(xb, w3_ref[...], preferred_element_type=F32)
        h = (jax.nn.silu(gate) * up).astype(BF16)
        acc_ref[...] += jnp.dot(h, w2_ref[...], preferred_element_type=F32)

    @pl.when(used & (f == nf - 1))
    def _():
        o_ref[...] = acc_ref[...]

    @pl.when(jnp.logical_not(used) & (f == nf - 1))
    def _():
        o_ref[...] = jnp.zeros_like(o_ref)


def _moe_ffn(x2, slot_tok, w1, w3, w2, blk_expert, n_used, tm):
    n_slots = slot_tok.shape[0]
    d = x2.shape[1]
    f_dim = w1.shape[2]
    tf = _ffn_tile(f_dim)
    nf = f_dim // tf
    nt = n_slots // tm

    def f_sel(i, f, nu):
        return jnp.where(i < nu[0], f, nf - 1)

    grid_spec = pltpu.PrefetchScalarGridSpec(
        num_scalar_prefetch=2,
        grid=(nt, nf),
        in_specs=[pl.BlockSpec((tm,), lambda i, f, be, nu: (i,), memory_space=pltpu.SMEM),
                  pl.BlockSpec((tm,), lambda i, f, be, nu: (jnp.minimum(i + 1, nt - 1),), memory_space=pltpu.SMEM),
                  pl.BlockSpec(memory_space=pl.ANY),
                  pl.BlockSpec((None, d, tf), lambda i, f, be, nu: (be[i], 0, f_sel(i, f, nu))),
                  pl.BlockSpec((None, d, tf), lambda i, f, be, nu: (be[i], 0, f_sel(i, f, nu))),
                  pl.BlockSpec((None, tf, d), lambda i, f, be, nu: (be[i], f_sel(i, f, nu), 0))],
        out_specs=pl.BlockSpec((tm, d), lambda i, f, be, nu: (i, 0)),
        scratch_shapes=[pltpu.VMEM((2, tm, d), F32), pltpu.VMEM((tm, d), BF16), pltpu.VMEM((tm, d), F32),
                        pltpu.SemaphoreType.DMA((2,))],
    )
    return pl.pallas_call(
        _moe_ffn_kernel,
        name="moe_ffn",
        grid_spec=grid_spec,
        out_shape=jax.ShapeDtypeStruct((n_slots, d), F32),
        compiler_params=_params(("arbitrary", "arbitrary")),
    )(blk_expert, n_used, slot_tok, slot_tok, x2, w1, w3, w2)


def _moe_combine_kernel(dest_ref, dest_next_ref, y_hbm, x_ref, gate_ref, g_ref, b_ref, o_ref, y_ref, sem, *,
                        alpha, tb):
    i = pl.program_id(0)
    slot = i % 2

    def start_gather(ids_ref, buf):
        def body(tok, carry):
            for k in range(MOE_TOP_K):
                _row_copy(y_hbm, ids_ref[MOE_TOP_K * tok + k], y_ref.at[buf, k], tok, sem.at[buf]).start()
            return carry

        lax.fori_loop(0, tb, body, 0, unroll=4)

    @pl.when(i == 0)
    def _():
        start_gather(dest_ref, 0)

    for k in range(MOE_TOP_K):
        pltpu.make_async_copy(y_hbm.at[pl.ds(0, tb)], y_ref.at[slot, k], sem.at[slot]).wait()

    @pl.when(i + 1 < pl.num_programs(0))
    def _():
        start_gather(dest_next_ref, 1 - slot)

    gates = gate_ref[...]
    ff = gates[:, 0:1] * y_ref[slot, 0] + gates[:, 1:2] * y_ref[slot, 1]
    o_ref[...] = _layer_norm(alpha * x_ref[...] + ff, g_ref[...], b_ref[...])


def _moe_combine(x2, y, dest, gates, g, b, alpha):
    t, d = x2.shape
    tb = min(t, 256)
    nt = t // tb
    return pl.pallas_call(
        functools.partial(_moe_combine_kernel, alpha=alpha, tb=tb),
        name="moe_combine",
        grid=(nt,),
        in_specs=[pl.BlockSpec((tb * MOE_TOP_K,), lambda i: (i,), memory_space=pltpu.SMEM),
                  pl.BlockSpec((tb * MOE_TOP_K,), lambda i: (jnp.minimum(i + 1, nt - 1),), memory_space=pltpu.SMEM),
                  pl.BlockSpec(memory_space=pl.ANY),
                  pl.BlockSpec((tb, d), lambda i: (i, 0)),
                  pl.BlockSpec((tb, LANES), lambda i: (i, 0)),
                  pl.BlockSpec((1, d), lambda i: (0, 0)),
                  pl.BlockSpec((1, d), lambda i: (0, 0))],
        out_specs=pl.BlockSpec((tb, d), lambda i: (i, 0)),
        out_shape=jax.ShapeDtypeStruct((t, d), F32),
        scratch_shapes=[pltpu.VMEM((2, MOE_TOP_K, tb, d), F32), pltpu.SemaphoreType.DMA((2,))],
        compiler_params=_params(("arbitrary",)),
    )(dest, dest, y, x2, gates, g[None, :], b[None, :])


def _moe_layer(x2, w_router, w1, w3, w2, g, b, alpha):
    t, d = x2.shape
    n = t * MOE_TOP_K
    tm = min(512, t)
    idx_pad, gates = _router(x2, w_router)
    e_flat = idx_pad[:, :MOE_TOP_K].reshape(n)
    flat_ids = jnp.arange(n, dtype=I32)
    _, order = lax.sort((e_flat, flat_ids), num_keys=1, is_stable=True)
    _, inv_order = lax.sort((order, flat_ids), num_keys=1)
    experts = jnp.arange(MOE_EXPERTS, dtype=I32)
    counts = jnp.sum((e_flat[:, None] == experts[None, :]).astype(I32), axis=0)
    starts = jnp.cumsum(counts) - counts
    padded = (counts + tm - 1) // tm * tm
    pends = jnp.cumsum(padded)
    pstarts = pends - padded
    n_tiles = (n + MOE_EXPERTS * (tm - 1) + tm - 1) // tm
    n_used = (pends[-1] // tm).astype(I32)
    blk_start = jnp.arange(n_tiles, dtype=I32) * tm
    blk_expert = jnp.minimum(jnp.sum((pends[None, :] <= blk_start[:, None]).astype(I32), axis=1), MOE_EXPERTS - 1)
    blk_expert = jnp.where(jnp.arange(n_tiles) < n_used, blk_expert, blk_expert[jnp.maximum(n_used - 1, 0)])
    slot = jnp.arange(n_tiles * tm, dtype=I32)
    slot_e = blk_expert[slot // tm]
    off = slot - pstarts[slot_e]
    slot_valid = (off < counts[slot_e]) & (slot // tm < n_used)
    slot_tok = jnp.where(slot_valid, order[jnp.clip(starts[slot_e] + off, 0, n - 1)] // MOE_TOP_K, 0).astype(I32)
    dest = (pstarts[e_flat] + inv_order - starts[e_flat]).astype(I32)

    y = _moe_ffn(x2, slot_tok, w1.astype(BF16), w3.astype(BF16), w2.astype(BF16), blk_expert.astype(I32),
                 n_used.reshape(1), tm)
    return _moe_combine(x2, y, dest, gates, g, b, alpha)


def kernel(x, p, positions, swa_w_qkv, swa_sinks, swa_w_o, dense_w1, dense_w3, dense_w2, dsa_w_in, dsa_kv_norm,
           dsa_w_uk, dsa_w_uv, dsa_w_o, moe_router, moe_w1, moe_w3, moe_w2, ln_g, ln_b, ple_w_p, ple_w_g):
    batch, seq, d = x.shape
    depth = ln_g.shape[0]
    alpha = (2.0 * depth) ** 0.25
    t = batch * seq
    x2 = x.reshape(t, d)
    cos_t, sin_t = _rope_tables(positions)
    for i in range(depth):
        j = i // 2
        if i % 2 == 0:
            q, k, v = _swa_qkv(x2, swa_w_qkv[j], cos_t, sin_t)
            o = _swa_attn(q, k, v, swa_sinks[j], batch, seq)
            x2 = _proj_ln(o, swa_w_o[j], x2, ln_g[i, 0], ln_b[i, 0], alpha)
            x2 = _ffn_dense(x2, dense_w1[j], dense_w3[j], dense_w2[j], ln_g[i, 1], ln_b[i, 1], alpha)
        else:
            o = _dsa_mixer_out(x2, cos_t, sin_t, dsa_w_in[j], dsa_kv_norm[j], dsa_w_uk[j], dsa_w_uv[j], batch, seq)
            x2 = _proj_ln(o, dsa_w_o[j], x2, ln_g[i, 0], ln_b[i, 0], alpha)
            x2 = _moe_layer(x2, moe_router[j], moe_w1[j], moe_w3[j], moe_w2[j], ln_g[i, 1], ln_b[i, 1], alpha)
        x2 = _ple(x2, p[i].reshape(t, -1), ple_w_g[i], ple_w_p[i])
    return x2.reshape(batch, seq, d)
```

```python
import functools

import jax
import jax.numpy as jnp
from jax import lax
from jax.experimental import pallas as pl
from jax.experimental.pallas import tpu as pltpu

F32 = jnp.float32
BF16 = jnp.bfloat16
I32 = jnp.int32

LANES = 128
ROPE_THETA = 10000.0
LN_EPS = 1e-5
RMS_EPS = 1e-6
SWA_HEADS = 32
SWA_KV_HEADS = 4
SWA_GROUP = SWA_HEADS // SWA_KV_HEADS
SWA_HEAD_DIM = 64
SWA_WINDOW = 128
SWA_V_PAD = SWA_HEAD_DIM + 16
MLA_HEADS = 16
MLA_NOPE = 128
MLA_ROPE = 64
MLA_V = 128
MLA_LATENT = 512
MLA_SCALE = (MLA_NOPE + MLA_ROPE) ** -0.5
MLA_QK_PAD = 256
MLA_V_PAD = MLA_V + 16
LOG2_E = 1.4426950408889634
IDX_HEADS = 16
IDX_DIM = 64
IDX_ROPE = 32
DSA_TOPK = 256
MOE_EXPERTS = 8
MOE_TOP_K = 2
NEG_BIG = -1e30
M_INIT = -1e20
VMEM_LIMIT = 56 * 1024 * 1024


def _params(sem, vmem=VMEM_LIMIT, flags=None):
    return pltpu.CompilerParams(dimension_semantics=sem, vmem_limit_bytes=vmem, flags=flags)


def _layer_norm(y, g, b):
    mu = jnp.mean(y, axis=-1, keepdims=True)
    d = y - mu
    var = jnp.mean(d * d, axis=-1, keepdims=True)
    return d * lax.rsqrt(var + LN_EPS) * g + b


def _rope_lanes(a, cs, sn, half):
    lane = lax.broadcasted_iota(I32, a.shape, 1)
    first = (lane % 64) < half
    rot = jnp.where(first, pltpu.roll(a, LANES - half, 1), pltpu.roll(a, half, 1))
    return a * cs + rot * sn


def _rope_table_kernel(pos_ref, invf_ref, sgn_ref, cos_ref, sin_ref):
    ang = pos_ref[...].astype(F32) * invf_ref[...]
    cos_ref[...] = jnp.cos(ang)
    sin_ref[...] = jnp.sin(ang) * sgn_ref[...]


def _rope_tables(positions):
    t = positions.size
    pos = positions.reshape(t, 1).astype(I32)
    lane = jnp.arange(LANES)
    in64 = lane % 64
    f64 = ROPE_THETA ** (-jnp.arange(0, 64, 2, dtype=F32) / 64)
    f32_ = ROPE_THETA ** (-jnp.arange(0, IDX_ROPE, 2, dtype=F32) / IDX_ROPE)
    invf_a = f64[in64 % 32]
    sgn_a = jnp.where(in64 < 32, -1.0, 1.0).astype(F32)
    invf_b = jnp.where(in64 < IDX_ROPE, f32_[in64 % (IDX_ROPE // 2)], 0.0)
    sgn_b = jnp.where(in64 < IDX_ROPE // 2, -1.0, 1.0).astype(F32)
    invf = jnp.concatenate([invf_a, invf_b])[None, :]
    sgn = jnp.concatenate([sgn_a, sgn_b])[None, :]
    tm = min(t, 1024)
    return pl.pallas_call(
        _rope_table_kernel,
        name="rope_tables",
        grid=(t // tm,),
        in_specs=[pl.BlockSpec((tm, 1), lambda i: (i, 0)),
                  pl.BlockSpec((1, 2 * LANES), lambda i: (0, 0)),
                  pl.BlockSpec((1, 2 * LANES), lambda i: (0, 0))],
        out_specs=[pl.BlockSpec((tm, 2 * LANES), lambda i: (i, 0)),
                   pl.BlockSpec((tm, 2 * LANES), lambda i: (i, 0))],
        out_shape=[jax.ShapeDtypeStruct((t, 2 * LANES), F32)] * 2,
        compiler_params=_params(("parallel",)),
    )(pos, invf, sgn)


def _swa_qkv_kernel(x_ref, w_ref, wvt_ref, cos_ref, sin_ref, q_ref, k_ref, vt_ref, *, nq, nk):
    xb = x_ref[...].astype(BF16)
    acc = jnp.dot(xb, w_ref[...], preferred_element_type=F32)
    cs = cos_ref[:, :LANES]
    sn = sin_ref[:, :LANES]
    scale = SWA_HEAD_DIM ** -0.5 * LOG2_E
    for c in range(nq // LANES):
        a = acc[:, c * LANES:(c + 1) * LANES]
        q_ref[:, c * LANES:(c + 1) * LANES] = (_rope_lanes(a, cs, sn, 32) * scale).astype(BF16)
    low = lax.broadcasted_iota(I32, (acc.shape[0], LANES), 1) < SWA_HEAD_DIM
    for c in range(nk // LANES):
        kr = _rope_lanes(acc[:, nq + c * LANES:nq + (c + 1) * LANES], cs, sn, 32)
        swapped = pltpu.roll(kr, SWA_HEAD_DIM, 1)
        placements = (jnp.where(low, kr, 0.0), jnp.where(low, 0.0, swapped),
                      jnp.where(low, swapped, 0.0), jnp.where(low, 0.0, kr))
        for n, slab in enumerate(placements):
            k_ref[:, (4 * c + n) * LANES:(4 * c + n + 1) * LANES] = slab.astype(BF16)
    vt = lax.dot_general(wvt_ref[...], xb, (((1,), (1,)), ((), ())), preferred_element_type=F32)
    ones = jnp.ones((SWA_V_PAD - SWA_HEAD_DIM, xb.shape[0]), BF16)
    for kh in range(SWA_KV_HEADS):
        vt_ref[kh * SWA_V_PAD:kh * SWA_V_PAD + SWA_HEAD_DIM, :] = (
            vt[kh * SWA_HEAD_DIM:(kh + 1) * SWA_HEAD_DIM, :].astype(BF16))
        vt_ref[kh * SWA_V_PAD + SWA_HEAD_DIM:(kh + 1) * SWA_V_PAD, :] = ones


def _swa_qkv(x2, w_qkv, cos_t, sin_t):
    t, d = x2.shape
    nq = SWA_HEADS * SWA_HEAD_DIM
    nk = SWA_KV_HEADS * SWA_HEAD_DIM
    nk2 = SWA_KV_HEADS * 2 * LANES
    nvt = SWA_KV_HEADS * SWA_V_PAD
    tm = min(t, 512)
    wb = w_qkv.astype(BF16)
    return pl.pallas_call(
        functools.partial(_swa_qkv_kernel, nq=nq, nk=nk),
        name="swa_qkv",
        grid=(t // tm,),
        in_specs=[pl.BlockSpec((tm, d), lambda i: (i, 0)),
                  pl.BlockSpec((d, nq + nk), lambda i: (0, 0)),
                  pl.BlockSpec((nk, d), lambda i: (0, 0)),
                  pl.BlockSpec((tm, 2 * LANES), lambda i: (i, 0)),
                  pl.BlockSpec((tm, 2 * LANES), lambda i: (i, 0))],
        out_specs=[pl.BlockSpec((tm, nq), lambda i: (i, 0)),
                   pl.BlockSpec((tm, nk2), lambda i: (i, 0)),
                   pl.BlockSpec((nvt, tm), lambda i: (0, i))],
        out_shape=[jax.ShapeDtypeStruct((t, nq), BF16),
                   jax.ShapeDtypeStruct((t, nk2), BF16),
                   jax.ShapeDtypeStruct((nvt, t), BF16)],
        compiler_params=_params(("parallel",)),
    )(x2, wb[:, :nq + nk], jnp.transpose(wb[:, nq + nk:]), cos_t, sin_t)


def _swa_attn_kernel(sink_ref, q_ref, kp_ref, kc_ref, vp_ref, vc_ref, o_ref):
    blk = pl.program_id(1)
    bq = q_ref.shape[0]
    dh = SWA_HEAD_DIM
    slabs = SWA_GROUP // 2
    cols = slabs * bq
    kj = lax.broadcasted_iota(I32, (2 * bq, cols), 0)
    qi = lax.broadcasted_iota(I32, (2 * bq, cols), 1) % bq
    rel = qi + bq - kj
    allowed = (rel >= 0) & (rel < SWA_WINDOW) & ((kj >= bq) | (blk > 0))
    for kh in range(SWA_KV_HEADS):
        base = kh * SWA_GROUP * dh
        qs = jnp.concatenate([q_ref[:, base + s * LANES:base + (s + 1) * LANES] for s in range(slabs)], axis=0)
        vcat = jnp.concatenate([vp_ref[kh * SWA_V_PAD:(kh + 1) * SWA_V_PAD, :],
                                vc_ref[kh * SWA_V_PAD:(kh + 1) * SWA_V_PAD, :]], axis=1)
        for half in range(2):
            ks = (2 * kh + half) * LANES
            kcat = jnp.concatenate([kp_ref[:, ks:ks + LANES], kc_ref[:, ks:ks + LANES]], axis=0)
            heads = [kh * SWA_GROUP + 2 * s + half for s in range(slabs)]
            sink = jnp.concatenate([jnp.full((1, bq), sink_ref[h] * LOG2_E, F32) for h in heads], axis=1)
            logits = lax.dot_general(kcat, qs, (((1,), (1,)), ((), ())), preferred_element_type=F32)
            logits = jnp.where(allowed, logits, NEG_BIG)
            m = jnp.maximum(jnp.max(logits, axis=0, keepdims=True), sink)
            p = jnp.exp2(logits - m).astype(BF16)
            pv = jnp.dot(vcat, p, preferred_element_type=F32)
            inv = 1.0 / (pv[dh:dh + 1, :] + jnp.exp2(sink - m))
            o = (pv[:dh, :] * inv).astype(BF16)
            for s, h in enumerate(heads):
                o_ref[h * dh:(h + 1) * dh, :] = o[:, s * bq:(s + 1) * bq]


def _swa_attn(q, k2, v_t, sinks, batch, seq):
    t, nq = q.shape
    nk2 = k2.shape[1]
    nvt = v_t.shape[0]
    bq = SWA_WINDOW
    nb = seq // bq

    def cur(b, i):
        return (b * nb + i, 0)

    def prev(b, i):
        return (b * nb + jnp.maximum(i - 1, 0), 0)

    def cur_t(b, i):
        return (0, b * nb + i)

    def prev_t(b, i):
        return (0, b * nb + jnp.maximum(i - 1, 0))

    return pl.pallas_call(
        _swa_attn_kernel,
        name="swa_attn",
        grid=(batch, nb),
        in_specs=[pl.BlockSpec(memory_space=pltpu.SMEM),
                  pl.BlockSpec((bq, nq), cur),
                  pl.BlockSpec((bq, nk2), prev),
                  pl.BlockSpec((bq, nk2), cur),
                  pl.BlockSpec((nvt, bq), prev_t),
                  pl.BlockSpec((nvt, bq), cur_t)],
        out_specs=pl.BlockSpec((nq, bq), cur_t),
        out_shape=jax.ShapeDtypeStruct((nq, t), BF16),
        compiler_params=_params(("parallel", "parallel")),
    )(sinks.astype(F32), q, k2, k2, v_t, v_t)


def _proj_ln_kernel(a_ref, w_ref, res_ref, g_ref, b_ref, o_ref, *, alpha):
    y = alpha * res_ref[...] + jnp.dot(a_ref[...], w_ref[...], preferred_element_type=F32)
    o_ref[...] = _layer_norm(y, g_ref[...], b_ref[...])


def _proj_ln(a, w, res, g, b, alpha):
    t, k = a.shape
    d = w.shape[1]
    tm = min(t, 512)
    return pl.pallas_call(
        functools.partial(_proj_ln_kernel, alpha=alpha),
        name="proj_ln",
        grid=(t // tm,),
        in_specs=[pl.BlockSpec((tm, k), lambda i: (i, 0)),
                  pl.BlockSpec((k, d), lambda i: (0, 0)),
                  pl.BlockSpec((tm, d), lambda i: (i, 0)),
                  pl.BlockSpec((1, d), lambda i: (0, 0)),
                  pl.BlockSpec((1, d), lambda i: (0, 0))],
        out_specs=pl.BlockSpec((tm, d), lambda i: (i, 0)),
        out_shape=jax.ShapeDtypeStruct((t, d), F32),
        compiler_params=_params(("parallel",)),
    )(a, w.astype(BF16), res, g[None, :], b[None, :])


def _ffn_dense_kernel(x_ref, w1_ref, w3_ref, w2_ref, g_ref, b_ref, o_ref, xb_ref, acc_ref, *, alpha):
    f = pl.program_id(1)

    @pl.when(f == 0)
    def _():
        xb_ref[...] = x_ref[...].astype(BF16)
        acc_ref[...] = jnp.zeros_like(acc_ref)

    xb = xb_ref[...]
    gate = jnp.dot(xb, w1_ref[...], preferred_element_type=F32)
    up = jnp.dot(xb, w3_ref[...], preferred_element_type=F32)
    h = (jax.nn.silu(gate) * up).astype(BF16)
    acc_ref[...] += jnp.dot(h, w2_ref[...], preferred_element_type=F32)

    @pl.when(f == pl.num_programs(1) - 1)
    def _():
        o_ref[...] = _layer_norm(alpha * x_ref[...] + acc_ref[...], g_ref[...], b_ref[...])


def _ffn_tile(f_dim):
    for tf in (512, 256, 128):
        if f_dim % tf == 0:
            return tf
    return f_dim


def _ffn_dense(x2, w1, w3, w2, g, b, alpha):
    t, d = x2.shape
    f_dim = w1.shape[1]
    tm = min(t, 512)
    tf = _ffn_tile(f_dim)
    return pl.pallas_call(
        functools.partial(_ffn_dense_kernel, alpha=alpha),
        name="ffn_dense",
        grid=(t // tm, f_dim // tf),
        in_specs=[pl.BlockSpec((tm, d), lambda i, f: (i, 0)),
                  pl.BlockSpec((d, tf), lambda i, f: (0, f)),
                  pl.BlockSpec((d, tf), lambda i, f: (0, f)),
                  pl.BlockSpec((tf, d), lambda i, f: (f, 0)),
                  pl.BlockSpec((1, d), lambda i, f: (0, 0)),
                  pl.BlockSpec((1, d), lambda i, f: (0, 0))],
        out_specs=pl.BlockSpec((tm, d), lambda i, f: (i, 0)),
        out_shape=jax.ShapeDtypeStruct((t, d), F32),
        scratch_shapes=[pltpu.VMEM((tm, d), BF16), pltpu.VMEM((tm, d), F32)],
        compiler_params=_params(("parallel", "arbitrary")),
    )(x2, w1.astype(BF16), w3.astype(BF16), w2.astype(BF16), g[None, :], b[None, :])


def _ple_kernel(x_ref, p_ref, wg_ref, wp_ref, o_ref):
    x = x_ref[...]
    gate = jax.nn.sigmoid(jnp.dot(x.astype(BF16), wg_ref[...], preferred_element_type=F32))
    emb = jnp.dot(p_ref[...].astype(BF16), wp_ref[...], preferred_element_type=F32)
    o_ref[...] = x + gate * emb


def _ple(x2, p2, w_g, w_p):
    t, d = x2.shape
    pd = p2.shape[1]
    tm = min(t, 512)
    return pl.pallas_call(
        _ple_kernel,
        name="ple",
        grid=(t // tm,),
        in_specs=[pl.BlockSpec((tm, d), lambda i: (i, 0)),
                  pl.BlockSpec((tm, pd), lambda i: (i, 0)),
                  pl.BlockSpec((d, d), lambda i: (0, 0)),
                  pl.BlockSpec((pd, d), lambda i: (0, 0))],
        out_specs=pl.BlockSpec((tm, d), lambda i: (i, 0)),
        out_shape=jax.ShapeDtypeStruct((t, d), F32),
        compiler_params=_params(("parallel",)),
    )(x2, p2, w_g.astype(BF16), w_p.astype(BF16))


def _dsa_q_kernel(x_ref, w_ref, cos_ref, sin_ref, q_ref):
    xb = x_ref[...].astype(BF16)
    acc = jnp.dot(xb, w_ref[...], preferred_element_type=F32)
    cs = cos_ref[:, :LANES]
    sn = sin_ref[:, :LANES]
    scale = MLA_SCALE * LOG2_E
    for h in range(MLA_HEADS):
        base = h * MLA_QK_PAD
        q_ref[:, base:base + LANES] = (acc[:, base:base + LANES] * scale).astype(BF16)
        a = acc[:, base + LANES:base + 2 * LANES]
        q_ref[:, base + LANES:base + 2 * LANES] = (_rope_lanes(a, cs, sn, 32) * scale).astype(BF16)


def _dsa_q(x2, w_q, cos_t, sin_t):
    t, d = x2.shape
    n = w_q.shape[1]
    tm = min(t, 512)
    return pl.pallas_call(
        _dsa_q_kernel,
        name="dsa_q",
        grid=(t // tm,),
        in_specs=[pl.BlockSpec((tm, d), lambda i: (i, 0)),
                  pl.BlockSpec((d, n), lambda i: (0, 0)),
                  pl.BlockSpec((tm, 2 * LANES), lambda i: (i, 0)),
                  pl.BlockSpec((tm, 2 * LANES), lambda i: (i, 0))],
        out_specs=pl.BlockSpec((tm, n), lambda i: (i, 0)),
        out_shape=jax.ShapeDtypeStruct((t, n), BF16),
        compiler_params=_params(("parallel",)),
    )(x2, w_q, cos_t, sin_t)


def _dsa_small_kernel(x_ref, w_ref, cos_ref, sin_ref, kvn_ref, ckv_ref, kr_ref, ki_ref, wi_ref):
    xb = x_ref[...].astype(BF16)
    acc = jnp.dot(xb, w_ref[...], preferred_element_type=F32)
    c = acc[:, :MLA_LATENT]
    ms = jnp.mean(c * c, axis=-1, keepdims=True)
    ckv_ref[...] = (c * lax.rsqrt(ms + RMS_EPS) * kvn_ref[...]).astype(BF16)
    o = MLA_LATENT
    kr_ref[...] = _rope_lanes(acc[:, o:o + LANES], cos_ref[:, :LANES], sin_ref[:, :LANES], 32).astype(BF16)
    ki_ref[...] = _rope_lanes(acc[:, o + LANES:o + 2 * LANES], cos_ref[:, LANES:], sin_ref[:, LANES:],
                              IDX_ROPE // 2).astype(BF16)
    wi_ref[...] = acc[:, o + 2 * LANES:o + 3 * LANES] * (IDX_HEADS ** -0.5)


def _dsa_small(x2, w_small, kv_norm, cos_t, sin_t):
    t, d = x2.shape
    n = w_small.shape[1]
    tm = min(t, 512)
    return pl.pallas_call(
        _dsa_small_kernel,
        name="dsa_small",
        grid=(t // tm,),
        in_specs=[pl.BlockSpec((tm, d), lambda i: (i, 0)),
                  pl.BlockSpec((d, n), lambda i: (0, 0)),
                  pl.BlockSpec((tm, 2 * LANES), lambda i: (i, 0)),
                  pl.BlockSpec((tm, 2 * LANES), lambda i: (i, 0)),
                  pl.BlockSpec((1, MLA_LATENT), lambda i: (0, 0))],
        out_specs=[pl.BlockSpec((tm, MLA_LATENT), lambda i: (i, 0)),
                   pl.BlockSpec((tm, LANES), lambda i: (i, 0)),
                   pl.BlockSpec((tm, LANES), lambda i: (i, 0)),
                   pl.BlockSpec((tm, LANES), lambda i: (i, 0))],
        out_shape=[jax.ShapeDtypeStruct((t, MLA_LATENT), BF16),
                   jax.ShapeDtypeStruct((t, LANES), BF16),
                   jax.ShapeDtypeStruct((t, LANES), BF16),
                   jax.ShapeDtypeStruct((t, LANES), F32)],
        compiler_params=_params(("parallel",)),
    )(x2, w_small, cos_t, sin_t, kv_norm[None, :].astype(F32))


def _dsa_qi_kernel(x_ref, w_ref, cos_ref, sin_ref, q_ref):
    xb = x_ref[...].astype(BF16)
    acc = jnp.dot(xb, w_ref[...], preferred_element_type=F32)
    cs = cos_ref[:, LANES:]
    sn = sin_ref[:, LANES:]
    scale = IDX_DIM ** -0.5
    for c in range(acc.shape[1] // LANES):
        a = acc[:, c * LANES:(c + 1) * LANES]
        q_ref[:, c * LANES:(c + 1) * LANES] = (_rope_lanes(a, cs, sn, IDX_ROPE // 2) * scale).astype(BF16)


def _dsa_qi(x2, w_qi, cos_t, sin_t):
    t, d = x2.shape
    n = w_qi.shape[1]
    tm = min(t, 512)
    return pl.pallas_call(
        _dsa_qi_kernel,
        name="dsa_qi",
        grid=(t // tm,),
        in_specs=[pl.BlockSpec((tm, d), lambda i: (i, 0)),
                  pl.BlockSpec((d, n), lambda i: (0, 0)),
                  pl.BlockSpec((tm, 2 * LANES), lambda i: (i, 0)),
                  pl.BlockSpec((tm, 2 * LANES), lambda i: (i, 0))],
        out_specs=pl.BlockSpec((tm, n), lambda i: (i, 0)),
        out_shape=jax.ShapeDtypeStruct((t, n), BF16),
        compiler_params=_params(("parallel",)),
    )(x2, w_qi, cos_t, sin_t)


def _dsa_kv_kernel(c_ref, kr_ref, wk_ref, wvt_ref, k_ref, vt_ref):
    c = c_ref[...]
    kn = jnp.dot(c, wk_ref[...], preferred_element_type=F32)
    kr = kr_ref[...]
    for h in range(MLA_HEADS):
        base = h * MLA_QK_PAD
        k_ref[:, base:base + LANES] = kn[:, h * LANES:(h + 1) * LANES].astype(BF16)
        k_ref[:, base + LANES:base + 2 * LANES] = kr
    vt = lax.dot_general(wvt_ref[...], c, (((1,), (1,)), ((), ())), preferred_element_type=F32)
    ones = jnp.ones((MLA_V_PAD - MLA_V, c.shape[0]), BF16)
    for h in range(MLA_HEADS):
        vt_ref[h * MLA_V_PAD:h * MLA_V_PAD + MLA_V, :] = vt[h * MLA_V:(h + 1) * MLA_V, :].astype(BF16)
        vt_ref[h * MLA_V_PAD + MLA_V:(h + 1) * MLA_V_PAD, :] = ones


def _dsa_kv(ckv, kr, w_uk_all, w_uv_t):
    t, c_dim = ckv.shape
    nk = MLA_HEADS * MLA_QK_PAD
    nv = w_uv_t.shape[0]
    nv_pad = MLA_HEADS * MLA_V_PAD
    tm = min(t, 512)
    return pl.pallas_call(
        _dsa_kv_kernel,
        name="dsa_kv",
        grid=(t // tm,),
        in_specs=[pl.BlockSpec((tm, c_dim), lambda i: (i, 0)),
                  pl.BlockSpec((tm, LANES), lambda i: (i, 0)),
                  pl.BlockSpec((c_dim, MLA_HEADS * MLA_NOPE), lambda i: (0, 0)),
                  pl.BlockSpec((nv, c_dim), lambda i: (0, 0))],
        out_specs=[pl.BlockSpec((tm, nk), lambda i: (i, 0)),
                   pl.BlockSpec((nv_pad, tm), lambda i: (0, i))],
        out_shape=[jax.ShapeDtypeStruct((t, nk), BF16),
                   jax.ShapeDtypeStruct((nv_pad, t), BF16)],
        compiler_params=_params(("parallel",)),
    )(ckv, kr, w_uk_all, w_uv_t)


def _float_sort_key(s):
    bits = lax.bitcast_convert_type(s, I32)
    return bits ^ ((bits >> 31) & jnp.int32(0x7FFFFFFF))


def _dsa_select_kernel(k_ref, q_ref, w_ref, bias_ref, key_ref, *, sc, cc, n_sel):
    j = pl.program_id(1)
    s_len, tq = key_ref.shape
    n_chunks = s_len // sc
    q_lo = j * tq
    n_need = (q_lo + tq + sc - 1) // sc
    qf = q_ref[...].reshape(IDX_HEADS * tq, IDX_DIM)
    w_row = w_ref[...]
    q_pos = q_lo + lax.broadcasted_iota(I32, (sc, tq), 1)
    k_off = lax.broadcasted_iota(I32, (sc, tq), 0)
    min_key = jnp.int32(-2 ** 31)

    def score_chunk(c, carry):
        row0 = pl.multiple_of(c * sc, sc)
        kc = k_ref[pl.ds(row0, sc), :]
        d = lax.dot_general(kc, qf, (((1,), (1,)), ((), ())), preferred_element_type=F32)
        s = jnp.zeros((sc, tq), F32)
        for h in range(IDX_HEADS):
            s = s + jnp.maximum(d[:, h * tq:(h + 1) * tq], 0.0) * w_row[:, h * tq:(h + 1) * tq]
        causal = (row0 + k_off) <= q_pos
        key_ref[pl.ds(row0, sc), :] = jnp.where(causal, _float_sort_key(s), min_key)
        return carry

    lax.fori_loop(0, n_need, score_chunk, 0)

    n_cnt = (n_need * sc + cc - 1) // cc

    def fill_chunk(c, carry):
        row0 = pl.multiple_of(c * sc, sc)
        key_ref[pl.ds(row0, sc), :] = jnp.full((sc, tq), min_key, I32)
        return carry

    lax.fori_loop(n_need, n_cnt * (cc // sc), fill_chunk, 0)

    def count_ge(cand):
        def body(c, cnt):
            row0 = pl.multiple_of(c * cc, cc)
            hit = jnp.where(key_ref[pl.ds(row0, cc), :] >= cand, 1, 0).astype(I32)
            return cnt + jnp.sum(hit.reshape(cc // 32, 32, tq), axis=0)

        cnt = lax.fori_loop(0, n_cnt, body, jnp.zeros((32, tq), I32))
        return jnp.sum(cnt, axis=0, keepdims=True)

    zero = jnp.zeros((1, tq), I32)
    thr = jnp.where(count_ge(zero) >= n_sel, zero, jnp.full((1, tq), min_key, I32))

    def bit_step(it, thr):
        cand = thr + (jnp.int32(1) << (30 - it))
        return jnp.where(count_ge(cand) >= n_sel, cand, thr)

    thr = lax.fori_loop(0, 31, bit_step, thr)

    def write_chunk(c, carry):
        row0 = pl.multiple_of(c * sc, sc)
        keep = (key_ref[pl.ds(row0, sc), :] >= thr) & ((row0 + k_off) <= q_pos)
        bias_ref[pl.ds(row0, sc), :] = jnp.where(keep, 0.0, NEG_BIG).astype(F32)
        return carry

    def mask_chunk(c, carry):
        row0 = pl.multiple_of(c * sc, sc)
        bias_ref[pl.ds(row0, sc), :] = jnp.full((sc, tq), NEG_BIG, F32)
        return carry

    lax.fori_loop(0, n_need, write_chunk, 0)
    lax.fori_loop(n_need, n_chunks, mask_chunk, 0)


def _dsa_select(k_i, q_i_t, w_rows, batch, seq, n_sel):
    tq = LANES
    nq = seq // tq
    sc = min(seq, 256)
    cc = min(seq, 1024)
    return pl.pallas_call(
        functools.partial(_dsa_select_kernel, sc=sc, cc=cc, n_sel=n_sel),
        name="dsa_select",
        grid=(batch, nq),
        in_specs=[pl.BlockSpec((None, seq, IDX_DIM), lambda b, j: (b, 0, 0)),
                  pl.BlockSpec((None, IDX_HEADS, tq, IDX_DIM), lambda b, j: (b, 0, j, 0)),
                  pl.BlockSpec((None, None, 1, IDX_HEADS * tq), lambda b, j: (b, j, 0, 0))],
        out_specs=pl.BlockSpec((None, seq, tq), lambda b, j: (b, 0, j)),
        out_shape=jax.ShapeDtypeStruct((batch, seq, seq), F32),
        scratch_shapes=[pltpu.VMEM((seq, tq), I32)],
        compiler_params=_params(("parallel", "parallel")),
    )(k_i, q_i_t, w_rows)


def _dsa_attn_kernel(q_ref, k_ref, vt_ref, bias_ref, o_ref, m_ref, a_ref, *head_refs, sc):
    s_refs = head_refs[:MLA_HEADS]
    acc_refs = head_refs[MLA_HEADS:]
    j = pl.program_id(1)
    c = pl.program_id(2)
    tq = q_ref.shape[0]
    last = (j * tq + tq - 1) // sc
    cur = c % 2
    prev = 1 - cur

    @pl.when(c == 0)
    def _():
        for h in range(MLA_HEADS):
            acc_refs[h][...] = jnp.zeros_like(acc_refs[h])
            s_refs[h][1] = jnp.full((sc, tq), NEG_BIG, F32)
        m_ref[...] = jnp.full_like(m_ref, M_INIT)
        a_ref[...] = jnp.ones_like(a_ref)

    def finish_prev():
        m_prev = m_ref[...]
        a_prev = a_ref[...]
        for h in range(MLA_HEADS):
            p = jnp.exp2(s_refs[h][prev] - m_prev[h:h + 1, :]).astype(BF16)
            pv = jnp.dot(vt_ref[h * MLA_V_PAD:(h + 1) * MLA_V_PAD, :], p, preferred_element_type=F32)
            acc_refs[h][...] = a_prev[h:h + 1, :] * acc_refs[h][...] + pv
        return m_prev

    @pl.when(c <= last)
    def _():
        m_prev = finish_prev()
        chunk_max = []
        for h in range(MLA_HEADS):
            qh = q_ref[:, h * MLA_QK_PAD:(h + 1) * MLA_QK_PAD]
            kh = k_ref[:, h * MLA_QK_PAD:(h + 1) * MLA_QK_PAD]
            s = lax.dot_general(kh, qh, (((1,), (1,)), ((), ())), preferred_element_type=F32) + bias_ref[...]
            s_refs[h][cur] = s
            chunk_max.append(jnp.max(s, axis=0, keepdims=True))
        m_new = jnp.maximum(m_prev, jnp.concatenate(chunk_max, axis=0))
        a_ref[...] = jnp.exp2(m_prev - m_new)
        m_ref[...] = m_new

    @pl.when(c == last + 1)
    def _():
        finish_prev()
        for h in range(MLA_HEADS):
            inv = 1.0 / acc_refs[h][MLA_V:MLA_V + 1, :]
            o_ref[h * MLA_V:(h + 1) * MLA_V, :] = (acc_refs[h][:MLA_V, :] * inv).astype(BF16)


def _dsa_attn(q_cat, k_cat, v_t, bias, batch, seq):
    t, nqk = q_cat.shape
    nv_pad = v_t.shape[0]
    nv = MLA_HEADS * MLA_V
    tq = min(seq, 256)
    sc = min(seq, 512)
    nq = seq // tq
    nc = seq // sc

    def last_chunk(j):
        return (j * tq + tq - 1) // sc

    return pl.pallas_call(
        functools.partial(_dsa_attn_kernel, sc=sc),
        name="dsa_attn",
        grid=(batch, nq, nc + 1),
        in_specs=[pl.BlockSpec((tq, nqk), lambda b, j, c: (b * nq + j, 0)),
                  pl.BlockSpec((sc, nqk), lambda b, j, c: (b * nc + jnp.minimum(c, last_chunk(j)), 0)),
                  pl.BlockSpec((nv_pad, sc), lambda b, j, c: (0, b * nc + jnp.clip(c - 1, 0, last_chunk(j)))),
                  pl.BlockSpec((None, sc, tq), lambda b, j, c: (b, jnp.minimum(c, last_chunk(j)), j))],
        out_specs=pl.BlockSpec((nv, tq), lambda b, j, c: (0, b * nq + j)),
        out_shape=jax.ShapeDtypeStruct((nv, t), BF16),
        scratch_shapes=([pltpu.VMEM((MLA_HEADS, tq), F32), pltpu.VMEM((MLA_HEADS, tq), F32)]
                        + [pltpu.VMEM((2, sc, tq), F32)] * MLA_HEADS
                        + [pltpu.VMEM((MLA_V_PAD, tq), F32)] * MLA_HEADS),
        compiler_params=_params(("parallel", "parallel", "arbitrary")),
    )(q_cat, k_cat, v_t, bias)


def _dsa_mixer_out(x2, cos_t, sin_t, w_in, kv_norm, w_uk, w_uv, batch, seq):
    t, d = x2.shape
    nq = MLA_HEADS * (MLA_NOPE + MLA_ROPE)
    o_c = nq
    o_kr = o_c + MLA_LATENT
    o_qi = o_kr + MLA_ROPE
    o_ki = o_qi + IDX_HEADS * IDX_DIM
    o_wi = o_ki + IDX_DIM
    wb = w_in.astype(BF16)
    wq = wb[:, :nq].reshape(d, MLA_HEADS, MLA_NOPE + MLA_ROPE)
    wq = jnp.pad(wq, ((0, 0), (0, 0), (0, MLA_QK_PAD - MLA_NOPE - MLA_ROPE))).reshape(d, MLA_HEADS * MLA_QK_PAD)
    pad_to_lanes = lambda w: jnp.pad(w, ((0, 0), (0, LANES - w.shape[1])))
    w_small = jnp.concatenate([wb[:, o_c:o_kr], pad_to_lanes(wb[:, o_kr:o_qi]), pad_to_lanes(wb[:, o_ki:o_wi]),
                               pad_to_lanes(wb[:, o_wi:])], axis=1)
    w_qi = wb[:, o_qi:o_ki]
    w_uk_all = jnp.transpose(w_uk, (1, 0, 2)).reshape(MLA_LATENT, MLA_HEADS * MLA_NOPE).astype(BF16)
    w_uv_t = jnp.transpose(w_uv, (0, 2, 1)).reshape(MLA_HEADS * MLA_V, MLA_LATENT).astype(BF16)

    q_cat = _dsa_q(x2, wq, cos_t, sin_t)
    ckv, kr, ki, wi = _dsa_small(x2, w_small, kv_norm, cos_t, sin_t)
    qi = _dsa_qi(x2, w_qi, cos_t, sin_t)
    k_cat, v_t = _dsa_kv(ckv, kr, w_uk_all, w_uv_t)

    tq = LANES
    nqb = seq // tq
    k_i = ki[:, :IDX_DIM].reshape(batch, seq, IDX_DIM)
    q_i_t = jnp.transpose(qi.reshape(batch, seq, IDX_HEADS, IDX_DIM), (0, 2, 1, 3))
    w_rows = jnp.transpose(wi[:, :IDX_HEADS].reshape(batch, nqb, tq, IDX_HEADS), (0, 1, 3, 2))
    w_rows = w_rows.reshape(batch, nqb, 1, IDX_HEADS * tq)
    n_sel = min(DSA_TOPK, seq // 4)
    bias = _dsa_select(k_i, q_i_t, w_rows, batch, seq, n_sel)
    o_t = _dsa_attn(q_cat, k_cat, v_t, bias, batch, seq)
    return jnp.transpose(o_t)


def _router_kernel(x_ref, w_ref, idx_ref, gate_ref):
    logits = jnp.dot(x_ref[...], w_ref[...], precision=lax.Precision.HIGHEST, preferred_element_type=F32)
    lane = lax.broadcasted_iota(I32, logits.shape, 1)
    lg = jnp.where(lane < MOE_EXPERTS, logits, -jnp.inf)
    v1 = jnp.max(lg, axis=-1, keepdims=True)
    i1 = jnp.min(jnp.where(lg == v1, lane, LANES), axis=-1, keepdims=True)
    lg2 = jnp.where(lane == i1, -jnp.inf, lg)
    v2 = jnp.max(lg2, axis=-1, keepdims=True)
    i2 = jnp.min(jnp.where(lg2 == v2, lane, LANES), axis=-1, keepdims=True)
    e2 = jnp.exp(v2 - v1)
    den = 1.0 + e2
    idx_ref[...] = jnp.where(lane == 0, i1, jnp.where(lane == 1, i2, 0))
    gate_ref[...] = jnp.where(lane == 0, 1.0 / den, jnp.where(lane == 1, e2 / den, 0.0))


def _router(x2, w_router):
    t, d = x2.shape
    wr = jnp.pad(w_router.astype(F32), ((0, 0), (0, LANES - w_router.shape[1])))
    tm = min(t, 512)
    return pl.pallas_call(
        _router_kernel,
        name="moe_router",
        grid=(t // tm,),
        in_specs=[pl.BlockSpec((tm, d), lambda i: (i, 0)),
                  pl.BlockSpec((d, LANES), lambda i: (0, 0))],
        out_specs=[pl.BlockSpec((tm, LANES), lambda i: (i, 0)),
                   pl.BlockSpec((tm, LANES), lambda i: (i, 0))],
        out_shape=[jax.ShapeDtypeStruct((t, LANES), I32),
                   jax.ShapeDtypeStruct((t, LANES), F32)],
        compiler_params=_params(("parallel",)),
    )(x2, wr)


def _row_copy(src_hbm, src_row, dst, dst_row, sem):
    return pltpu.make_async_copy(src_hbm.at[pl.ds(src_row, 1)], dst.at[pl.ds(dst_row, 1)], sem)


def _moe_ffn_kernel(be_ref, nu_ref, tok_ref, tok_next_ref, x_hbm, w1_ref, w3_ref, w2_ref, o_ref,
                    xg_ref, xb_ref, acc_ref, sem):
    i = pl.program_id(0)
    f = pl.program_id(1)
    nf = pl.num_programs(1)
    n_used = nu_ref[0]
    used = i < n_used
    tm = xb_ref.shape[0]
    slot = i % 2

    def start_gather(ids_ref, buf):
        def body(r, carry):
            _row_copy(x_hbm, ids_ref[r], xg_ref.at[buf], r, sem.at[buf]).start()
            return carry

        lax.fori_loop(0, tm, body, 0, unroll=8)

    @pl.when(used & (f == 0))
    def _():
        @pl.when(i == 0)
        def _():
            start_gather(tok_ref, 0)

        pltpu.make_async_copy(x_hbm.at[pl.ds(0, tm)], xg_ref.at[slot], sem.at[slot]).wait()

        @pl.when(i + 1 < n_used)
        def _():
            start_gather(tok_next_ref, 1 - slot)

        xb_ref[...] = xg_ref[slot].astype(BF16)
        acc_ref[...] = jnp.zeros_like(acc_ref)

    @pl.when(used)
    def _():
        xb = xb_ref[...]
        gate = jnp.dot(xb, w1_ref[...], preferred_element_type=F32)
        up = jnp.dot(xb, w3_ref[...], preferred_element_type=F32)
        h = (jax.nn.silu(gate) * up).astype(BF16)
        acc_ref[...] += jnp.dot(h, w2_ref[...], preferred_element_type=F32)

    @pl.when(used & (f == nf - 1))
    def _():
        o_ref[...] = acc_ref[...]

    @pl.when(jnp.logical_not(used) & (f == nf - 1))
    def _():
        o_ref[...] = jnp.zeros_like(o_ref)


def _moe_ffn(x2, slot_tok, w1, w3, w2, blk_expert, n_used, tm):
    n_slots = slot_tok.shape[0]
    d = x2.shape[1]
    f_dim = w1.shape[2]
    tf = _ffn_tile(f_dim)
    nf = f_dim // tf
    nt = n_slots // tm

    def f_sel(i, f, nu):
        return jnp.where(i < nu[0], f, nf - 1)

    grid_spec = pltpu.PrefetchScalarGridSpec(
        num_scalar_prefetch=2,
        grid=(nt, nf),
        in_specs=[pl.BlockSpec((tm,), lambda i, f, be, nu: (i,), memory_space=pltpu.SMEM),
                  pl.BlockSpec((tm,), lambda i, f, be, nu: (jnp.minimum(i + 1, nt - 1),), memory_space=pltpu.SMEM),
                  pl.BlockSpec(memory_space=pl.ANY),
                  pl.BlockSpec((None, d, tf), lambda i, f, be, nu: (be[i], 0, f_sel(i, f, nu))),
                  pl.BlockSpec((None, d, tf), lambda i, f, be, nu: (be[i], 0, f_sel(i, f, nu))),
                  pl.BlockSpec((None, tf, d), lambda i, f, be, nu: (be[i], f_sel(i, f, nu), 0))],
        out_specs=pl.BlockSpec((tm, d), lambda i, f, be, nu: (i, 0)),
        scratch_shapes=[pltpu.VMEM((2, tm, d), F32), pltpu.VMEM((tm, d), BF16), pltpu.VMEM((tm, d), F32),
                        pltpu.SemaphoreType.DMA((2,))],
    )
    return pl.pallas_call(
        _moe_ffn_kernel,
        name="moe_ffn",
        grid_spec=grid_spec,
        out_shape=jax.ShapeDtypeStruct((n_slots, d), F32),
        compiler_params=_params(("arbitrary", "arbitrary")),
    )(blk_expert, n_used, slot_tok, slot_tok, x2, w1, w3, w2)


def _moe_combine_kernel(dest_ref, dest_next_ref, y_hbm, x_ref, gate_ref, g_ref, b_ref, o_ref, y_ref, sem, *,
                        alpha, tb):
    i = pl.program_id(0)
    slot = i % 2

    def start_gather(ids_ref, buf):
        def body(tok, carry):
            for k in range(MOE_TOP_K):
                _row_copy(y_hbm, ids_ref[MOE_TOP_K * tok + k], y_ref.at[buf, k], tok, sem.at[buf]).start()
            return carry

        lax.fori_loop(0, tb, body, 0, unroll=4)

    @pl.when(i == 0)
    def _():
        start_gather(dest_ref, 0)

    for k in range(MOE_TOP_K):
        pltpu.make_async_copy(y_hbm.at[pl.ds(0, tb)], y_ref.at[slot, k], sem.at[slot]).wait()

    @pl.when(i + 1 < pl.num_programs(0))
    def _():
        start_gather(dest_next_ref, 1 - slot)

    gates = gate_ref[...]
    ff = gates[:, 0:1] * y_ref[slot, 0] + gates[:, 1:2] * y_ref[slot, 1]
    o_ref[...] = _layer_norm(alpha * x_ref[...] + ff, g_ref[...], b_ref[...])


def _moe_combine(x2, y, dest, gates, g, b, alpha):
    t, d = x2.shape
    tb = min(t, 256)
    nt = t // tb
    return pl.pallas_call(
        functools.partial(_moe_combine_kernel, alpha=alpha, tb=tb),
        name="moe_combine",
        grid=(nt,),
        in_specs=[pl.BlockSpec((tb * MOE_TOP_K,), lambda i: (i,), memory_space=pltpu.SMEM),
                  pl.BlockSpec((tb * MOE_TOP_K,), lambda i: (jnp.minimum(i + 1, nt - 1),), memory_space=pltpu.SMEM),
                  pl.BlockSpec(memory_space=pl.ANY),
                  pl.BlockSpec((tb, d), lambda i: (i, 0)),
                  pl.BlockSpec((tb, LANES), lambda i: (i, 0)),
                  pl.BlockSpec((1, d), lambda i: (0, 0)),
                  pl.BlockSpec((1, d), lambda i: (0, 0))],
        out_specs=pl.BlockSpec((tb, d), lambda i: (i, 0)),
        out_shape=jax.ShapeDtypeStruct((t, d), F32),
        scratch_shapes=[pltpu.VMEM((2, MOE_TOP_K, tb, d), F32), pltpu.SemaphoreType.DMA((2,))],
        compiler_params=_params(("arbitrary",)),
    )(dest, dest, y, x2, gates, g[None, :], b[None, :])


def _moe_layer(x2, w_router, w1, w3, w2, g, b, alpha):
    t, d = x2.shape
    n = t * MOE_TOP_K
    tm = min(512, t)
    idx_pad, gates = _router(x2, w_router)
    e_flat = idx_pad[:, :MOE_TOP_K].reshape(n)
    flat_ids = jnp.arange(n, dtype=I32)
    _, order = lax.sort((e_flat, flat_ids), num_keys=1, is_stable=True)
    _, inv_order = lax.sort((order, flat_ids), num_keys=1)
    experts = jnp.arange(MOE_EXPERTS, dtype=I32)
    counts = jnp.sum((e_flat[:, None] == experts[None, :]).astype(I32), axis=0)
    starts = jnp.cumsum(counts) - counts
    padded = (counts + tm - 1) // tm * tm
    pends = jnp.cumsum(padded)
    pstarts = pends - padded
    n_tiles = (n + MOE_EXPERTS * (tm - 1) + tm - 1) // tm
    n_used = (pends[-1] // tm).astype(I32)
    blk_start = jnp.arange(n_tiles, dtype=I32) * tm
    blk_expert = jnp.minimum(jnp.sum((pends[None, :] <= blk_start[:, None]).astype(I32), axis=1), MOE_EXPERTS - 1)
    blk_expert = jnp.where(jnp.arange(n_tiles) < n_used, blk_expert, blk_expert[jnp.maximum(n_used - 1, 0)])
    n_slots = n_tiles * tm
    slot = jnp.arange(n_slots, dtype=I32)
    slot_e = jnp.repeat(blk_expert, tm)
    shift = pstarts - starts
    order_tok = jnp.pad(order // MOE_TOP_K, (0, n_slots - n))
    slot_tok = jnp.zeros((n_slots,), I32)
    for e in range(MOE_EXPERTS):
        in_group = (slot_e == e) & (slot - pstarts[e] < counts[e]) & (slot < n_used * tm)
        slot_tok = jnp.where(in_group, jnp.roll(order_tok, shift[e]), slot_tok)
    dest = (inv_order + jnp.sum(jnp.where(e_flat[:, None] == experts[None, :], shift[None, :], 0), axis=1)).astype(I32)

    y = _moe_ffn(x2, slot_tok, w1.astype(BF16), w3.astype(BF16), w2.astype(BF16), blk_expert.astype(I32),
                 n_used.reshape(1), tm)
    return _moe_combine(x2, y, dest, gates, g, b, alpha)


def kernel(x, p, positions, swa_w_qkv, swa_sinks, swa_w_o, dense_w1, dense_w3, dense_w2, dsa_w_in, dsa_kv_norm,
           dsa_w_uk, dsa_w_uv, dsa_w_o, moe_router, moe_w1, moe_w3, moe_w2, ln_g, ln_b, ple_w_p, ple_w_g):
    batch, seq, d = x.shape
    depth = ln_g.shape[0]
    alpha = (2.0 * depth) ** 0.25
    t = batch * seq
    x2 = x.reshape(t, d)
    cos_t, sin_t = _rope_tables(positions)
    for i in range(depth):
        j = i // 2
        if i % 2 == 0:
            q, k2, v_t = _swa_qkv(x2, swa_w_qkv[j], cos_t, sin_t)
            o = jnp.transpose(_swa_attn(q, k2, v_t, swa_sinks[j], batch, seq))
            x2 = _proj_ln(o, swa_w_o[j], x2, ln_g[i, 0], ln_b[i, 0], alpha)
            x2 = _ffn_dense(x2, dense_w1[j], dense_w3[j], dense_w2[j], ln_g[i, 1], ln_b[i, 1], alpha)
        else:
            o = _dsa_mixer_out(x2, cos_t, sin_t, dsa_w_in[j], dsa_kv_norm[j], dsa_w_uk[j], dsa_w_uv[j], batch, seq)
            x2 = _proj_ln(o, dsa_w_o[j], x2, ln_g[i, 0], ln_b[i, 0], alpha)
            x2 = _moe_layer(x2, moe_router[j], moe_w1[j], moe_w3[j], moe_w2[j], ln_g[i, 1], ln_b[i, 1], alpha)
        x2 = _ple(x2, p[i].reshape(t, -1), ple_w_g[i], ple_w_p[i])
    return x2.reshape(batch, seq, d)
```

```python
import functools

import jax
import jax.numpy as jnp
from jax import lax
from jax.experimental import pallas as pl
from jax.experimental.pallas import tpu as pltpu

F32 = jnp.float32
BF16 = jnp.bfloat16
I32 = jnp.int32

LANES = 128
ROPE_THETA = 10000.0
LN_EPS = 1e-5
RMS_EPS = 1e-6
SWA_HEADS = 32
SWA_KV_HEADS = 4
SWA_GROUP = SWA_HEADS // SWA_KV_HEADS
SWA_HEAD_DIM = 64
SWA_WINDOW = 128
SWA_V_PAD = SWA_HEAD_DIM + 16
MLA_HEADS = 16
MLA_NOPE = 128
MLA_ROPE = 64
MLA_V = 128
MLA_LATENT = 512
MLA_SCALE = (MLA_NOPE + MLA_ROPE) ** -0.5
MLA_QK_PAD = 256
MLA_V_PAD = MLA_V + 16
LOG2_E = 1.4426950408889634
IDX_HEADS = 16
IDX_DIM = 64
IDX_ROPE = 32
DSA_TOPK = 256
MOE_EXPERTS = 8
MOE_TOP_K = 2
NEG_BIG = -1e30
M_INIT = -1e20
VMEM_LIMIT = 56 * 1024 * 1024


def _params(sem, vmem=VMEM_LIMIT, flags=None):
    return pltpu.CompilerParams(dimension_semantics=sem, vmem_limit_bytes=vmem, flags=flags)


def _layer_norm(y, g, b):
    mu = jnp.mean(y, axis=-1, keepdims=True)
    d = y - mu
    var = jnp.mean(d * d, axis=-1, keepdims=True)
    return d * lax.rsqrt(var + LN_EPS) * g + b


def _rope_lanes(a, cs, sn, half):
    lane = lax.broadcasted_iota(I32, a.shape, 1)
    first = (lane % 64) < half
    rot = jnp.where(first, pltpu.roll(a, LANES - half, 1), pltpu.roll(a, half, 1))
    return a * cs + rot * sn


def _rope_table_kernel(pos_ref, invf_ref, sgn_ref, cos_ref, sin_ref):
    ang = pos_ref[...].astype(F32) * invf_ref[...]
    cos_ref[...] = jnp.cos(ang)
    sin_ref[...] = jnp.sin(ang) * sgn_ref[...]


def _rope_tables(positions):
    t = positions.size
    pos = positions.reshape(t, 1).astype(I32)
    lane = jnp.arange(LANES)
    in64 = lane % 64
    f64 = ROPE_THETA ** (-jnp.arange(0, 64, 2, dtype=F32) / 64)
    f32_ = ROPE_THETA ** (-jnp.arange(0, IDX_ROPE, 2, dtype=F32) / IDX_ROPE)
    invf_a = f64[in64 % 32]
    sgn_a = jnp.where(in64 < 32, -1.0, 1.0).astype(F32)
    invf_b = jnp.where(in64 < IDX_ROPE, f32_[in64 % (IDX_ROPE // 2)], 0.0)
    sgn_b = jnp.where(in64 < IDX_ROPE // 2, -1.0, 1.0).astype(F32)
    invf = jnp.concatenate([invf_a, invf_b])[None, :]
    sgn = jnp.concatenate([sgn_a, sgn_b])[None, :]
    tm = min(t, 1024)
    return pl.pallas_call(
        _rope_table_kernel,
        name="rope_tables",
        grid=(t // tm,),
        in_specs=[pl.BlockSpec((tm, 1), lambda i: (i, 0)),
                  pl.BlockSpec((1, 2 * LANES), lambda i: (0, 0)),
                  pl.BlockSpec((1, 2 * LANES), lambda i: (0, 0))],
        out_specs=[pl.BlockSpec((tm, 2 * LANES), lambda i: (i, 0)),
                   pl.BlockSpec((tm, 2 * LANES), lambda i: (i, 0))],
        out_shape=[jax.ShapeDtypeStruct((t, 2 * LANES), F32)] * 2,
        compiler_params=_params(("parallel",)),
    )(pos, invf, sgn)


def _swa_qkv_kernel(x_ref, w_ref, wvt_ref, cos_ref, sin_ref, q_ref, k_ref, vt_ref, *, nq, nk):
    xb = x_ref[...].astype(BF16)
    acc = jnp.dot(xb, w_ref[...], preferred_element_type=F32)
    cs = cos_ref[:, :LANES]
    sn = sin_ref[:, :LANES]
    scale = SWA_HEAD_DIM ** -0.5 * LOG2_E
    for c in range(nq // LANES):
        a = acc[:, c * LANES:(c + 1) * LANES]
        q_ref[:, c * LANES:(c + 1) * LANES] = (_rope_lanes(a, cs, sn, 32) * scale).astype(BF16)
    low = lax.broadcasted_iota(I32, (acc.shape[0], LANES), 1) < SWA_HEAD_DIM
    for c in range(nk // LANES):
        kr = _rope_lanes(acc[:, nq + c * LANES:nq + (c + 1) * LANES], cs, sn, 32)
        swapped = pltpu.roll(kr, SWA_HEAD_DIM, 1)
        placements = (jnp.where(low, kr, 0.0), jnp.where(low, 0.0, swapped),
                      jnp.where(low, swapped, 0.0), jnp.where(low, 0.0, kr))
        for n, slab in enumerate(placements):
            k_ref[:, (4 * c + n) * LANES:(4 * c + n + 1) * LANES] = slab.astype(BF16)
    vt = lax.dot_general(wvt_ref[...], xb, (((1,), (1,)), ((), ())), preferred_element_type=F32)
    ones = jnp.ones((SWA_V_PAD - SWA_HEAD_DIM, xb.shape[0]), BF16)
    for kh in range(SWA_KV_HEADS):
        vt_ref[kh * SWA_V_PAD:kh * SWA_V_PAD + SWA_HEAD_DIM, :] = (
            vt[kh * SWA_HEAD_DIM:(kh + 1) * SWA_HEAD_DIM, :].astype(BF16))
        vt_ref[kh * SWA_V_PAD + SWA_HEAD_DIM:(kh + 1) * SWA_V_PAD, :] = ones


def _swa_qkv(x2, w_qkv, cos_t, sin_t):
    t, d = x2.shape
    nq = SWA_HEADS * SWA_HEAD_DIM
    nk = SWA_KV_HEADS * SWA_HEAD_DIM
    nk2 = SWA_KV_HEADS * 2 * LANES
    nvt = SWA_KV_HEADS * SWA_V_PAD
    tm = min(t, 512)
    wb = w_qkv.astype(BF16)
    return pl.pallas_call(
        functools.partial(_swa_qkv_kernel, nq=nq, nk=nk),
        name="swa_qkv",
        grid=(t // tm,),
        in_specs=[pl.BlockSpec((tm, d), lambda i: (i, 0)),
                  pl.BlockSpec((d, nq + nk), lambda i: (0, 0)),
                  pl.BlockSpec((nk, d), lambda i: (0, 0)),
                  pl.BlockSpec((tm, 2 * LANES), lambda i: (i, 0)),
                  pl.BlockSpec((tm, 2 * LANES), lambda i: (i, 0))],
        out_specs=[pl.BlockSpec((tm, nq), lambda i: (i, 0)),
                   pl.BlockSpec((tm, nk2), lambda i: (i, 0)),
                   pl.BlockSpec((nvt, tm), lambda i: (0, i))],
        out_shape=[jax.ShapeDtypeStruct((t, nq), BF16),
                   jax.ShapeDtypeStruct((t, nk2), BF16),
                   jax.ShapeDtypeStruct((nvt, t), BF16)],
        compiler_params=_params(("parallel",)),
    )(x2, wb[:, :nq + nk], jnp.transpose(wb[:, nq + nk:]), cos_t, sin_t)


def _swa_attn_kernel(sink_ref, q_ref, kp_ref, kc_ref, vp_ref, vc_ref, o_ref):
    blk = pl.program_id(1)
    bq = q_ref.shape[0]
    dh = SWA_HEAD_DIM
    slabs = SWA_GROUP // 2
    cols = slabs * bq
    kj = lax.broadcasted_iota(I32, (2 * bq, cols), 0)
    qi = lax.broadcasted_iota(I32, (2 * bq, cols), 1) % bq
    rel = qi + bq - kj
    allowed = (rel >= 0) & (rel < SWA_WINDOW) & ((kj >= bq) | (blk > 0))
    for kh in range(SWA_KV_HEADS):
        base = kh * SWA_GROUP * dh
        qs = jnp.concatenate([q_ref[:, base + s * LANES:base + (s + 1) * LANES] for s in range(slabs)], axis=0)
        vcat = jnp.concatenate([vp_ref[kh * SWA_V_PAD:(kh + 1) * SWA_V_PAD, :],
                                vc_ref[kh * SWA_V_PAD:(kh + 1) * SWA_V_PAD, :]], axis=1)
        for half in range(2):
            ks = (2 * kh + half) * LANES
            kcat = jnp.concatenate([kp_ref[:, ks:ks + LANES], kc_ref[:, ks:ks + LANES]], axis=0)
            heads = [kh * SWA_GROUP + 2 * s + half for s in range(slabs)]
            sink = jnp.concatenate([jnp.full((1, bq), sink_ref[h] * LOG2_E, F32) for h in heads], axis=1)
            logits = lax.dot_general(kcat, qs, (((1,), (1,)), ((), ())), preferred_element_type=F32)
            logits = jnp.where(allowed, logits, NEG_BIG)
            m = jnp.maximum(jnp.max(logits, axis=0, keepdims=True), sink)
            p = jnp.exp2(logits - m).astype(BF16)
            pv = jnp.dot(vcat, p, preferred_element_type=F32)
            inv = 1.0 / (pv[dh:dh + 1, :] + jnp.exp2(sink - m))
            o = (pv[:dh, :] * inv).astype(BF16)
            for s, h in enumerate(heads):
                o_ref[h * dh:(h + 1) * dh, :] = o[:, s * bq:(s + 1) * bq]


def _swa_attn(q, k2, v_t, sinks, batch, seq):
    t, nq = q.shape
    nk2 = k2.shape[1]
    nvt = v_t.shape[0]
    bq = SWA_WINDOW
    nb = seq // bq

    def cur(b, i):
        return (b * nb + i, 0)

    def prev(b, i):
        return (b * nb + jnp.maximum(i - 1, 0), 0)

    def cur_t(b, i):
        return (0, b * nb + i)

    def prev_t(b, i):
        return (0, b * nb + jnp.maximum(i - 1, 0))

    return pl.pallas_call(
        _swa_attn_kernel,
        name="swa_attn",
        grid=(batch, nb),
        in_specs=[pl.BlockSpec(memory_space=pltpu.SMEM),
                  pl.BlockSpec((bq, nq), cur),
                  pl.BlockSpec((bq, nk2), prev),
                  pl.BlockSpec((bq, nk2), cur),
                  pl.BlockSpec((nvt, bq), prev_t),
                  pl.BlockSpec((nvt, bq), cur_t)],
        out_specs=pl.BlockSpec((nq, bq), cur_t),
        out_shape=jax.ShapeDtypeStruct((nq, t), BF16),
        compiler_params=_params(("parallel", "parallel")),
    )(sinks.astype(F32), q, k2, k2, v_t, v_t)


def _proj_ln_kernel(at_ref, w_ref, res_ref, g_ref, b_ref, o_ref, *, alpha):
    mix = lax.dot_general(at_ref[...], w_ref[...], (((0,), (0,)), ((), ())), preferred_element_type=F32)
    o_ref[...] = _layer_norm(alpha * res_ref[...] + mix, g_ref[...], b_ref[...])


def _proj_ln(a_t, w, res, g, b, alpha):
    k, t = a_t.shape
    d = w.shape[1]
    tm = min(t, 512)
    return pl.pallas_call(
        functools.partial(_proj_ln_kernel, alpha=alpha),
        name="proj_ln",
        grid=(t // tm,),
        in_specs=[pl.BlockSpec((k, tm), lambda i: (0, i)),
                  pl.BlockSpec((k, d), lambda i: (0, 0)),
                  pl.BlockSpec((tm, d), lambda i: (i, 0)),
                  pl.BlockSpec((1, d), lambda i: (0, 0)),
                  pl.BlockSpec((1, d), lambda i: (0, 0))],
        out_specs=pl.BlockSpec((tm, d), lambda i: (i, 0)),
        out_shape=jax.ShapeDtypeStruct((t, d), F32),
        compiler_params=_params(("parallel",)),
    )(a_t, w.astype(BF16), res, g[None, :], b[None, :])


def _ffn_dense_kernel(x_ref, w1_ref, w3_ref, w2_ref, g_ref, b_ref, o_ref, xb_ref, acc_ref, *, alpha):
    f = pl.program_id(1)

    @pl.when(f == 0)
    def _():
        xb_ref[...] = x_ref[...].astype(BF16)
        acc_ref[...] = jnp.zeros_like(acc_ref)

    xb = xb_ref[...]
    gate = jnp.dot(xb, w1_ref[...], preferred_element_type=F32)
    up = jnp.dot(xb, w3_ref[...], preferred_element_type=F32)
    h = (jax.nn.silu(gate) * up).astype(BF16)
    acc_ref[...] += jnp.dot(h, w2_ref[...], preferred_element_type=F32)

    @pl.when(f == pl.num_programs(1) - 1)
    def _():
        o_ref[...] = _layer_norm(alpha * x_ref[...] + acc_ref[...], g_ref[...], b_ref[...])


def _ffn_tile(f_dim):
    for tf in (512, 256, 128):
        if f_dim % tf == 0:
            return tf
    return f_dim


def _ffn_dense(x2, w1, w3, w2, g, b, alpha):
    t, d = x2.shape
    f_dim = w1.shape[1]
    tm = min(t, 512)
    tf = _ffn_tile(f_dim)
    return pl.pallas_call(
        functools.partial(_ffn_dense_kernel, alpha=alpha),
        name="ffn_dense",
        grid=(t // tm, f_dim // tf),
        in_specs=[pl.BlockSpec((tm, d), lambda i, f: (i, 0)),
                  pl.BlockSpec((d, tf), lambda i, f: (0, f)),
                  pl.BlockSpec((d, tf), lambda i, f: (0, f)),
                  pl.BlockSpec((tf, d), lambda i, f: (f, 0)),
                  pl.BlockSpec((1, d), lambda i, f: (0, 0)),
                  pl.BlockSpec((1, d), lambda i, f: (0, 0))],
        out_specs=pl.BlockSpec((tm, d), lambda i, f: (i, 0)),
        out_shape=jax.ShapeDtypeStruct((t, d), F32),
        scratch_shapes=[pltpu.VMEM((tm, d), BF16), pltpu.VMEM((tm, d), F32)],
        compiler_params=_params(("parallel", "arbitrary")),
    )(x2, w1.astype(BF16), w3.astype(BF16), w2.astype(BF16), g[None, :], b[None, :])


def _ple_kernel(x_ref, p_ref, wg_ref, wp_ref, o_ref):
    x = x_ref[...]
    gate = jax.nn.sigmoid(jnp.dot(x.astype(BF16), wg_ref[...], preferred_element_type=F32))
    emb = jnp.dot(p_ref[...].astype(BF16), wp_ref[...], preferred_element_type=F32)
    o_ref[...] = x + gate * emb


def _ple(x2, p2, w_g, w_p):
    t, d = x2.shape
    pd = p2.shape[1]
    tm = min(t, 512)
    return pl.pallas_call(
        _ple_kernel,
        name="ple",
        grid=(t // tm,),
        in_specs=[pl.BlockSpec((tm, d), lambda i: (i, 0)),
                  pl.BlockSpec((tm, pd), lambda i: (i, 0)),
                  pl.BlockSpec((d, d), lambda i: (0, 0)),
                  pl.BlockSpec((pd, d), lambda i: (0, 0))],
        out_specs=pl.BlockSpec((tm, d), lambda i: (i, 0)),
        out_shape=jax.ShapeDtypeStruct((t, d), F32),
        compiler_params=_params(("parallel",)),
    )(x2, p2, w_g.astype(BF16), w_p.astype(BF16))


def _dsa_q_kernel(x_ref, w_ref, cos_ref, sin_ref, q_ref):
    xb = x_ref[...].astype(BF16)
    acc = jnp.dot(xb, w_ref[...], preferred_element_type=F32)
    cs = cos_ref[:, :LANES]
    sn = sin_ref[:, :LANES]
    scale = MLA_SCALE * LOG2_E
    for h in range(MLA_HEADS):
        base = h * MLA_QK_PAD
        q_ref[:, base:base + LANES] = (acc[:, base:base + LANES] * scale).astype(BF16)
        a = acc[:, base + LANES:base + 2 * LANES]
        q_ref[:, base + LANES:base + 2 * LANES] = (_rope_lanes(a, cs, sn, 32) * scale).astype(BF16)


def _dsa_q(x2, w_q, cos_t, sin_t):
    t, d = x2.shape
    n = w_q.shape[1]
    tm = min(t, 512)
    return pl.pallas_call(
        _dsa_q_kernel,
        name="dsa_q",
        grid=(t // tm,),
        in_specs=[pl.BlockSpec((tm, d), lambda i: (i, 0)),
                  pl.BlockSpec((d, n), lambda i: (0, 0)),
                  pl.BlockSpec((tm, 2 * LANES), lambda i: (i, 0)),
                  pl.BlockSpec((tm, 2 * LANES), lambda i: (i, 0))],
        out_specs=pl.BlockSpec((tm, n), lambda i: (i, 0)),
        out_shape=jax.ShapeDtypeStruct((t, n), BF16),
        compiler_params=_params(("parallel",)),
    )(x2, w_q, cos_t, sin_t)


def _dsa_small_kernel(x_ref, w_ref, cos_ref, sin_ref, kvn_ref, ckv_ref, kr_ref, ki_ref, wi_ref):
    xb = x_ref[...].astype(BF16)
    acc = jnp.dot(xb, w_ref[...], preferred_element_type=F32)
    c = acc[:, :MLA_LATENT]
    ms = jnp.mean(c * c, axis=-1, keepdims=True)
    ckv_ref[...] = (c * lax.rsqrt(ms + RMS_EPS) * kvn_ref[...]).astype(BF16)
    o = MLA_LATENT
    kr_ref[...] = _rope_lanes(acc[:, o:o + LANES], cos_ref[:, :LANES], sin_ref[:, :LANES], 32).astype(BF16)
    ki_ref[...] = _rope_lanes(acc[:, o + LANES:o + 2 * LANES], cos_ref[:, LANES:], sin_ref[:, LANES:],
                              IDX_ROPE // 2).astype(BF16)
    wi_ref[...] = acc[:, o + 2 * LANES:o + 3 * LANES] * (IDX_HEADS ** -0.5)


def _dsa_small(x2, w_small, kv_norm, cos_t, sin_t):
    t, d = x2.shape
    n = w_small.shape[1]
    tm = min(t, 512)
    return pl.pallas_call(
        _dsa_small_kernel,
        name="dsa_small",
        grid=(t // tm,),
        in_specs=[pl.BlockSpec((tm, d), lambda i: (i, 0)),
                  pl.BlockSpec((d, n), lambda i: (0, 0)),
                  pl.BlockSpec((tm, 2 * LANES), lambda i: (i, 0)),
                  pl.BlockSpec((tm, 2 * LANES), lambda i: (i, 0)),
                  pl.BlockSpec((1, MLA_LATENT), lambda i: (0, 0))],
        out_specs=[pl.BlockSpec((tm, MLA_LATENT), lambda i: (i, 0)),
                   pl.BlockSpec((tm, LANES), lambda i: (i, 0)),
                   pl.BlockSpec((tm, LANES), lambda i: (i, 0)),
                   pl.BlockSpec((tm, LANES), lambda i: (i, 0))],
        out_shape=[jax.ShapeDtypeStruct((t, MLA_LATENT), BF16),
                   jax.ShapeDtypeStruct((t, LANES), BF16),
                   jax.ShapeDtypeStruct((t, LANES), BF16),
                   jax.ShapeDtypeStruct((t, LANES), F32)],
        compiler_params=_params(("parallel",)),
    )(x2, w_small, cos_t, sin_t, kv_norm[None, :].astype(F32))


def _dsa_qi_kernel(x_ref, w_ref, cos_ref, sin_ref, q_ref):
    xb = x_ref[...].astype(BF16)
    acc = jnp.dot(xb, w_ref[...], preferred_element_type=F32)
    cs = cos_ref[:, LANES:]
    sn = sin_ref[:, LANES:]
    scale = IDX_DIM ** -0.5
    for c in range(acc.shape[1] // LANES):
        a = acc[:, c * LANES:(c + 1) * LANES]
        q_ref[:, c * LANES:(c + 1) * LANES] = (_rope_lanes(a, cs, sn, IDX_ROPE // 2) * scale).astype(BF16)


def _dsa_qi(x2, w_qi, cos_t, sin_t):
    t, d = x2.shape
    n = w_qi.shape[1]
    tm = min(t, 512)
    return pl.pallas_call(
        _dsa_qi_kernel,
        name="dsa_qi",
        grid=(t // tm,),
        in_specs=[pl.BlockSpec((tm, d), lambda i: (i, 0)),
                  pl.BlockSpec((d, n), lambda i: (0, 0)),
                  pl.BlockSpec((tm, 2 * LANES), lambda i: (i, 0)),
                  pl.BlockSpec((tm, 2 * LANES), lambda i: (i, 0))],
        out_specs=pl.BlockSpec((tm, n), lambda i: (i, 0)),
        out_shape=jax.ShapeDtypeStruct((t, n), BF16),
        compiler_params=_params(("parallel",)),
    )(x2, w_qi, cos_t, sin_t)


def _dsa_kv_kernel(c_ref, kr_ref, wk_ref, wvt_ref, k_ref, vt_ref):
    c = c_ref[...]
    kn = jnp.dot(c, wk_ref[...], preferred_element_type=F32)
    kr = kr_ref[...]
    for h in range(MLA_HEADS):
        base = h * MLA_QK_PAD
        k_ref[:, base:base + LANES] = kn[:, h * LANES:(h + 1) * LANES].astype(BF16)
        k_ref[:, base + LANES:base + 2 * LANES] = kr
    vt = lax.dot_general(wvt_ref[...], c, (((1,), (1,)), ((), ())), preferred_element_type=F32)
    ones = jnp.ones((MLA_V_PAD - MLA_V, c.shape[0]), BF16)
    for h in range(MLA_HEADS):
        vt_ref[h * MLA_V_PAD:h * MLA_V_PAD + MLA_V, :] = vt[h * MLA_V:(h + 1) * MLA_V, :].astype(BF16)
        vt_ref[h * MLA_V_PAD + MLA_V:(h + 1) * MLA_V_PAD, :] = ones


def _dsa_kv(ckv, kr, w_uk_all, w_uv_t):
    t, c_dim = ckv.shape
    nk = MLA_HEADS * MLA_QK_PAD
    nv = w_uv_t.shape[0]
    nv_pad = MLA_HEADS * MLA_V_PAD
    tm = min(t, 512)
    return pl.pallas_call(
        _dsa_kv_kernel,
        name="dsa_kv",
        grid=(t // tm,),
        in_specs=[pl.BlockSpec((tm, c_dim), lambda i: (i, 0)),
                  pl.BlockSpec((tm, LANES), lambda i: (i, 0)),
                  pl.BlockSpec((c_dim, MLA_HEADS * MLA_NOPE), lambda i: (0, 0)),
                  pl.BlockSpec((nv, c_dim), lambda i: (0, 0))],
        out_specs=[pl.BlockSpec((tm, nk), lambda i: (i, 0)),
                   pl.BlockSpec((nv_pad, tm), lambda i: (0, i))],
        out_shape=[jax.ShapeDtypeStruct((t, nk), BF16),
                   jax.ShapeDtypeStruct((nv_pad, t), BF16)],
        compiler_params=_params(("parallel",)),
    )(ckv, kr, w_uk_all, w_uv_t)


def _float_sort_key(s):
    bits = lax.bitcast_convert_type(s, I32)
    return bits ^ ((bits >> 31) & jnp.int32(0x7FFFFFFF))


def _dsa_select_kernel(k_ref, q_ref, w_ref, bias_ref, key_ref, *, sc, cc, n_sel):
    j = pl.program_id(1)
    s_len, tq = key_ref.shape
    n_chunks = s_len // sc
    q_lo = j * tq
    n_need = (q_lo + tq + sc - 1) // sc
    qf = q_ref[...].reshape(IDX_HEADS * tq, IDX_DIM)
    w_row = w_ref[...]
    q_pos = q_lo + lax.broadcasted_iota(I32, (sc, tq), 1)
    k_off = lax.broadcasted_iota(I32, (sc, tq), 0)
    min_key = jnp.int32(-2 ** 31)

    def score_chunk(c, carry):
        row0 = pl.multiple_of(c * sc, sc)
        kc = k_ref[pl.ds(row0, sc), :]
        d = lax.dot_general(kc, qf, (((1,), (1,)), ((), ())), preferred_element_type=F32)
        s = jnp.zeros((sc, tq), F32)
        for h in range(IDX_HEADS):
            s = s + jnp.maximum(d[:, h * tq:(h + 1) * tq], 0.0) * w_row[:, h * tq:(h + 1) * tq]
        causal = (row0 + k_off) <= q_pos
        key_ref[pl.ds(row0, sc), :] = jnp.where(causal, _float_sort_key(s), min_key)
        return carry

    lax.fori_loop(0, n_need, score_chunk, 0)

    n_cnt = (n_need * sc + cc - 1) // cc

    def fill_chunk(c, carry):
        row0 = pl.multiple_of(c * sc, sc)
        key_ref[pl.ds(row0, sc), :] = jnp.full((sc, tq), min_key, I32)
        return carry

    lax.fori_loop(n_need, n_cnt * (cc // sc), fill_chunk, 0)

    def count_ge(cand):
        def body(c, cnt):
            row0 = pl.multiple_of(c * cc, cc)
            hit = jnp.where(key_ref[pl.ds(row0, cc), :] >= cand, 1, 0).astype(I32)
            return cnt + jnp.sum(hit.reshape(cc // 32, 32, tq), axis=0)

        cnt = lax.fori_loop(0, n_cnt, body, jnp.zeros((32, tq), I32))
        return jnp.sum(cnt, axis=0, keepdims=True)

    zero = jnp.zeros((1, tq), I32)
    thr = jnp.where(count_ge(zero) >= n_sel, zero, jnp.full((1, tq), min_key, I32))

    def bit_step(it, thr):
        cand = thr + (jnp.int32(1) << (30 - it))
        return jnp.where(count_ge(cand) >= n_sel, cand, thr)

    thr = lax.fori_loop(0, 31, bit_step, thr)

    def write_chunk(c, carry):
        row0 = pl.multiple_of(c * sc, sc)
        keep = (key_ref[pl.ds(row0, sc), :] >= thr) & ((row0 + k_off) <= q_pos)
        bias_ref[pl.ds(row0, sc), :] = jnp.where(keep, 0.0, NEG_BIG).astype(F32)
        return carry

    def mask_chunk(c, carry):
        row0 = pl.multiple_of(c * sc, sc)
        bias_ref[pl.ds(row0, sc), :] = jnp.full((sc, tq), NEG_BIG, F32)
        return carry

    lax.fori_loop(0, n_need, write_chunk, 0)
    lax.fori_loop(n_need, n_chunks, mask_chunk, 0)


def _dsa_select(k_i, q_i_t, w_rows, batch, seq, n_sel):
    tq = LANES
    nq = seq // tq
    sc = min(seq, 256)
    cc = min(seq, 1024)
    return pl.pallas_call(
        functools.partial(_dsa_select_kernel, sc=sc, cc=cc, n_sel=n_sel),
        name="dsa_select",
        grid=(batch, nq),
        in_specs=[pl.BlockSpec((None, seq, IDX_DIM), lambda b, j: (b, 0, 0)),
                  pl.BlockSpec((None, IDX_HEADS, tq, IDX_DIM), lambda b, j: (b, 0, j, 0)),
                  pl.BlockSpec((None, None, 1, IDX_HEADS * tq), lambda b, j: (b, j, 0, 0))],
        out_specs=pl.BlockSpec((None, seq, tq), lambda b, j: (b, 0, j)),
        out_shape=jax.ShapeDtypeStruct((batch, seq, seq), F32),
        scratch_shapes=[pltpu.VMEM((seq, tq), I32)],
        compiler_params=_params(("parallel", "parallel")),
    )(k_i, q_i_t, w_rows)


def _dsa_attn_kernel(q_ref, k_ref, vt_ref, bias_ref, o_ref, m_ref, a_ref, *head_refs, sc):
    s_refs = head_refs[:MLA_HEADS]
    acc_refs = head_refs[MLA_HEADS:]
    j = pl.program_id(1)
    c = pl.program_id(2)
    tq = q_ref.shape[0]
    last = (j * tq + tq - 1) // sc
    cur = c % 2
    prev = 1 - cur

    @pl.when(c == 0)
    def _():
        for h in range(MLA_HEADS):
            acc_refs[h][...] = jnp.zeros_like(acc_refs[h])
            s_refs[h][1] = jnp.full((sc, tq), NEG_BIG, F32)
        m_ref[...] = jnp.full_like(m_ref, M_INIT)
        a_ref[...] = jnp.ones_like(a_ref)

    def finish_prev():
        m_prev = m_ref[...]
        a_prev = a_ref[...]
        for h in range(MLA_HEADS):
            p = jnp.exp2(s_refs[h][prev] - m_prev[h:h + 1, :]).astype(BF16)
            pv = jnp.dot(vt_ref[h * MLA_V_PAD:(h + 1) * MLA_V_PAD, :], p, preferred_element_type=F32)
            acc_refs[h][...] = a_prev[h:h + 1, :] * acc_refs[h][...] + pv
        return m_prev

    @pl.when(c <= last)
    def _():
        m_prev = finish_prev()
        chunk_max = []
        for h in range(MLA_HEADS):
            qh = q_ref[:, h * MLA_QK_PAD:(h + 1) * MLA_QK_PAD]
            kh = k_ref[:, h * MLA_QK_PAD:(h + 1) * MLA_QK_PAD]
            s = lax.dot_general(kh, qh, (((1,), (1,)), ((), ())), preferred_element_type=F32) + bias_ref[...]
            s_refs[h][cur] = s
            chunk_max.append(jnp.max(s, axis=0, keepdims=True))
        m_new = jnp.maximum(m_prev, jnp.concatenate(chunk_max, axis=0))
        a_ref[...] = jnp.exp2(m_prev - m_new)
        m_ref[...] = m_new

    @pl.when(c == last + 1)
    def _():
        finish_prev()
        for h in range(MLA_HEADS):
            inv = 1.0 / acc_refs[h][MLA_V:MLA_V + 1, :]
            o_ref[h * MLA_V:(h + 1) * MLA_V, :] = (acc_refs[h][:MLA_V, :] * inv).astype(BF16)


def _dsa_attn(q_cat, k_cat, v_t, bias, batch, seq):
    t, nqk = q_cat.shape
    nv_pad = v_t.shape[0]
    nv = MLA_HEADS * MLA_V
    tq = min(seq, 512)
    sc = min(seq, 256)
    nq = seq // tq
    nc = seq // sc

    def last_chunk(j):
        return (j * tq + tq - 1) // sc

    return pl.pallas_call(
        functools.partial(_dsa_attn_kernel, sc=sc),
        name="dsa_attn",
        grid=(batch, nq, nc + 1),
        in_specs=[pl.BlockSpec((tq, nqk), lambda b, j, c: (b * nq + j, 0)),
                  pl.BlockSpec((sc, nqk), lambda b, j, c: (b * nc + jnp.minimum(c, last_chunk(j)), 0)),
                  pl.BlockSpec((nv_pad, sc), lambda b, j, c: (0, b * nc + jnp.clip(c - 1, 0, last_chunk(j)))),
                  pl.BlockSpec((None, sc, tq), lambda b, j, c: (b, jnp.minimum(c, last_chunk(j)), j))],
        out_specs=pl.BlockSpec((nv, tq), lambda b, j, c: (0, b * nq + j)),
        out_shape=jax.ShapeDtypeStruct((nv, t), BF16),
        scratch_shapes=([pltpu.VMEM((MLA_HEADS, tq), F32), pltpu.VMEM((MLA_HEADS, tq), F32)]
                        + [pltpu.VMEM((2, sc, tq), F32)] * MLA_HEADS
                        + [pltpu.VMEM((MLA_V_PAD, tq), F32)] * MLA_HEADS),
        compiler_params=_params(("parallel", "parallel", "arbitrary")),
    )(q_cat, k_cat, v_t, bias)


def _dsa_mixer_out(x2, cos_t, sin_t, w_in, kv_norm, w_uk, w_uv, batch, seq):
    t, d = x2.shape
    nq = MLA_HEADS * (MLA_NOPE + MLA_ROPE)
    o_c = nq
    o_kr = o_c + MLA_LATENT
    o_qi = o_kr + MLA_ROPE
    o_ki = o_qi + IDX_HEADS * IDX_DIM
    o_wi = o_ki + IDX_DIM
    wb = w_in.astype(BF16)
    wq = wb[:, :nq].reshape(d, MLA_HEADS, MLA_NOPE + MLA_ROPE)
    wq = jnp.pad(wq, ((0, 0), (0, 0), (0, MLA_QK_PAD - MLA_NOPE - MLA_ROPE))).reshape(d, MLA_HEADS * MLA_QK_PAD)
    pad_to_lanes = lambda w: jnp.pad(w, ((0, 0), (0, LANES - w.shape[1])))
    w_small = jnp.concatenate([wb[:, o_c:o_kr], pad_to_lanes(wb[:, o_kr:o_qi]), pad_to_lanes(wb[:, o_ki:o_wi]),
                               pad_to_lanes(wb[:, o_wi:])], axis=1)
    w_qi = wb[:, o_qi:o_ki]
    w_uk_all = jnp.transpose(w_uk, (1, 0, 2)).reshape(MLA_LATENT, MLA_HEADS * MLA_NOPE).astype(BF16)
    w_uv_t = jnp.transpose(w_uv, (0, 2, 1)).reshape(MLA_HEADS * MLA_V, MLA_LATENT).astype(BF16)

    q_cat = _dsa_q(x2, wq, cos_t, sin_t)
    ckv, kr, ki, wi = _dsa_small(x2, w_small, kv_norm, cos_t, sin_t)
    qi = _dsa_qi(x2, w_qi, cos_t, sin_t)
    k_cat, v_t = _dsa_kv(ckv, kr, w_uk_all, w_uv_t)

    tq = LANES
    nqb = seq // tq
    k_i = ki[:, :IDX_DIM].reshape(batch, seq, IDX_DIM)
    q_i_t = jnp.transpose(qi.reshape(batch, seq, IDX_HEADS, IDX_DIM), (0, 2, 1, 3))
    w_rows = jnp.transpose(wi[:, :IDX_HEADS].reshape(batch, nqb, tq, IDX_HEADS), (0, 1, 3, 2))
    w_rows = w_rows.reshape(batch, nqb, 1, IDX_HEADS * tq)
    n_sel = min(DSA_TOPK, seq // 4)
    bias = _dsa_select(k_i, q_i_t, w_rows, batch, seq, n_sel)
    o_t = _dsa_attn(q_cat, k_cat, v_t, bias, batch, seq)
    return o_t


def _router_kernel(x_ref, w_ref, idx_ref, gate_ref):
    logits = jnp.dot(x_ref[...], w_ref[...], precision=lax.Precision.HIGHEST, preferred_element_type=F32)
    lane = lax.broadcasted_iota(I32, logits.shape, 1)
    lg = jnp.where(lane < MOE_EXPERTS, logits, -jnp.inf)
    v1 = jnp.max(lg, axis=-1, keepdims=True)
    i1 = jnp.min(jnp.where(lg == v1, lane, LANES), axis=-1, keepdims=True)
    lg2 = jnp.where(lane == i1, -jnp.inf, lg)
    v2 = jnp.max(lg2, axis=-1, keepdims=True)
    i2 = jnp.min(jnp.where(lg2 == v2, lane, LANES), axis=-1, keepdims=True)
    e2 = jnp.exp(v2 - v1)
    den = 1.0 + e2
    idx_ref[...] = jnp.where(lane == 0, i1, jnp.where(lane == 1, i2, 0))
    gate_ref[...] = jnp.where(lane == 0, 1.0 / den, jnp.where(lane == 1, e2 / den, 0.0))


def _router(x2, w_router):
    t, d = x2.shape
    wr = jnp.pad(w_router.astype(F32), ((0, 0), (0, LANES - w_router.shape[1])))
    tm = min(t, 512)
    return pl.pallas_call(
        _router_kernel,
        name="moe_router",
        grid=(t // tm,),
        in_specs=[pl.BlockSpec((tm, d), lambda i: (i, 0)),
                  pl.BlockSpec((d, LANES), lambda i: (0, 0))],
        out_specs=[pl.BlockSpec((tm, LANES), lambda i: (i, 0)),
                   pl.BlockSpec((tm, LANES), lambda i: (i, 0))],
        out_shape=[jax.ShapeDtypeStruct((t, LANES), I32),
                   jax.ShapeDtypeStruct((t, LANES), F32)],
        compiler_params=_params(("parallel",)),
    )(x2, wr)


def _row_copy(src_hbm, src_row, dst, dst_row, sem):
    return pltpu.make_async_copy(src_hbm.at[pl.ds(src_row, 1)], dst.at[pl.ds(dst_row, 1)], sem)


def _moe_ffn_kernel(be_ref, nu_ref, tok_ref, tok_next_ref, x_hbm, w1_ref, w3_ref, w2_ref, o_ref,
                    xg_ref, xb_ref, acc_ref, sem):
    i = pl.program_id(0)
    f = pl.program_id(1)
    nf = pl.num_programs(1)
    n_used = nu_ref[0]
    used = i < n_used
    tm = xb_ref.shape[0]
    slot = i % 2

    def start_gather(ids_ref, buf):
        def body(r, carry):
            _row_copy(x_hbm, ids_ref[r], xg_ref.at[buf], r, sem.at[buf]).start()
            return carry

        lax.fori_loop(0, tm, body, 0, unroll=8)

    @pl.when(used & (f == 0))
    def _():
        @pl.when(i == 0)
        def _():
            start_gather(tok_ref, 0)

        pltpu.make_async_copy(x_hbm.at[pl.ds(0, tm)], xg_ref.at[slot], sem.at[slot]).wait()

        @pl.when(i + 1 < n_used)
        def _():
            start_gather(tok_next_ref, 1 - slot)

        xb_ref[...] = xg_ref[slot].astype(BF16)
        acc_ref[...] = jnp.zeros_like(acc_ref)

    @pl.when(used)
    def _():
        xb = xb_ref[...]
        gate = jnp.dot(xb, w1_ref[...], preferred_element_type=F32)
        up = jnp.dot(xb, w3_ref[...], preferred_element_type=F32)
        h = (jax.nn.silu(gate) * up).astype(BF16)
        acc_ref[...] += jnp.dot(h, w2_ref[...], preferred_element_type=F32)

    @pl.when(used & (f == nf - 1))
    def _():
        o_ref[...] = acc_ref[...]

    @pl.when(jnp.logical_not(used) & (f == nf - 1))
    def _():
        o_ref[...] = jnp.zeros_like(o_ref)


def _moe_ffn(x2, slot_tok, w1, w3, w2, blk_expert, n_used, tm):
    n_slots = slot_tok.shape[0]
    d = x2.shape[1]
    f_dim = w1.shape[2]
    tf = _ffn_tile(f_dim)
    nf = f_dim // tf
    nt = n_slots // tm

    def f_sel(i, f, nu):
        return jnp.where(i < nu[0], f, nf - 1)

    grid_spec = pltpu.PrefetchScalarGridSpec(
        num_scalar_prefetch=2,
        grid=(nt, nf),
        in_specs=[pl.BlockSpec((tm,), lambda i, f, be, nu: (i,), memory_space=pltpu.SMEM),
                  pl.BlockSpec((tm,), lambda i, f, be, nu: (jnp.minimum(i + 1, nt - 1),), memory_space=pltpu.SMEM),
                  pl.BlockSpec(memory_space=pl.ANY),
                  pl.BlockSpec((None, d, tf), lambda i, f, be, nu: (be[i], 0, f_sel(i, f, nu))),
                  pl.BlockSpec((None, d, tf), lambda i, f, be, nu: (be[i], 0, f_sel(i, f, nu))),
                  pl.BlockSpec((None, tf, d), lambda i, f, be, nu: (be[i], f_sel(i, f, nu), 0))],
        out_specs=pl.BlockSpec((tm, d), lambda i, f, be, nu: (i, 0)),
        scratch_shapes=[pltpu.VMEM((2, tm, d), F32), pltpu.VMEM((tm, d), BF16), pltpu.VMEM((tm, d), F32),
                        pltpu.SemaphoreType.DMA((2,))],
    )
    return pl.pallas_call(
        _moe_ffn_kernel,
        name="moe_ffn",
        grid_spec=grid_spec,
        out_shape=jax.ShapeDtypeStruct((n_slots, d), F32),
        compiler_params=_params(("arbitrary", "arbitrary")),
    )(blk_expert, n_used, slot_tok, slot_tok, x2, w1, w3, w2)


def _moe_combine_kernel(dest_ref, dest_next_ref, y_hbm, x_ref, gate_ref, g_ref, b_ref, o_ref, y_ref, sem, *,
                        alpha, tb):
    i = pl.program_id(0)
    slot = i % 2

    def start_gather(ids_ref, buf):
        def body(tok, carry):
            for k in range(MOE_TOP_K):
                _row_copy(y_hbm, ids_ref[MOE_TOP_K * tok + k], y_ref.at[buf, k], tok, sem.at[buf]).start()
            return carry

        lax.fori_loop(0, tb, body, 0, unroll=4)

    @pl.when(i == 0)
    def _():
        start_gather(dest_ref, 0)

    for k in range(MOE_TOP_K):
        pltpu.make_async_copy(y_hbm.at[pl.ds(0, tb)], y_ref.at[slot, k], sem.at[slot]).wait()

    @pl.when(i + 1 < pl.num_programs(0))
    def _():
        start_gather(dest_next_ref, 1 - slot)

    gates = gate_ref[...]
    ff = gates[:, 0:1] * y_ref[slot, 0] + gates[:, 1:2] * y_ref[slot, 1]
    o_ref[...] = _layer_norm(alpha * x_ref[...] + ff, g_ref[...], b_ref[...])


def _moe_combine(x2, y, dest, gates, g, b, alpha):
    t, d = x2.shape
    tb = min(t, 256)
    nt = t // tb
    return pl.pallas_call(
        functools.partial(_moe_combine_kernel, alpha=alpha, tb=tb),
        name="moe_combine",
        grid=(nt,),
        in_specs=[pl.BlockSpec((tb * MOE_TOP_K,), lambda i: (i,), memory_space=pltpu.SMEM),
                  pl.BlockSpec((tb * MOE_TOP_K,), lambda i: (jnp.minimum(i + 1, nt - 1),), memory_space=pltpu.SMEM),
                  pl.BlockSpec(memory_space=pl.ANY),
                  pl.BlockSpec((tb, d), lambda i: (i, 0)),
                  pl.BlockSpec((tb, LANES), lambda i: (i, 0)),
                  pl.BlockSpec((1, d), lambda i: (0, 0)),
                  pl.BlockSpec((1, d), lambda i: (0, 0))],
        out_specs=pl.BlockSpec((tb, d), lambda i: (i, 0)),
        out_shape=jax.ShapeDtypeStruct((t, d), F32),
        scratch_shapes=[pltpu.VMEM((2, MOE_TOP_K, tb, d), F32), pltpu.SemaphoreType.DMA((2,))],
        compiler_params=_params(("arbitrary",)),
    )(dest, dest, y, x2, gates, g[None, :], b[None, :])


def _moe_layer(x2, w_router, w1, w3, w2, g, b, alpha):
    t, d = x2.shape
    n = t * MOE_TOP_K
    tm = min(512, t)
    idx_pad, gates = _router(x2, w_router)
    e_flat = idx_pad[:, :MOE_TOP_K].reshape(n)
    flat_ids = jnp.arange(n, dtype=I32)
    _, order = lax.sort((e_flat, flat_ids), num_keys=1, is_stable=True)
    _, inv_order = lax.sort((order, flat_ids), num_keys=1)
    experts = jnp.arange(MOE_EXPERTS, dtype=I32)
    counts = jnp.sum((e_flat[:, None] == experts[None, :]).astype(I32), axis=0)
    starts = jnp.cumsum(counts) - counts
    padded = (counts + tm - 1) // tm * tm
    pends = jnp.cumsum(padded)
    pstarts = pends - padded
    n_tiles = (n + MOE_EXPERTS * (tm - 1) + tm - 1) // tm
    n_used = (pends[-1] // tm).astype(I32)
    blk_start = jnp.arange(n_tiles, dtype=I32) * tm
    blk_expert = jnp.minimum(jnp.sum((pends[None, :] <= blk_start[:, None]).astype(I32), axis=1), MOE_EXPERTS - 1)
    blk_expert = jnp.where(jnp.arange(n_tiles) < n_used, blk_expert, blk_expert[jnp.maximum(n_used - 1, 0)])
    n_slots = n_tiles * tm
    slot = jnp.arange(n_slots, dtype=I32)
    slot_e = jnp.repeat(blk_expert, tm)
    shift = pstarts - starts
    order_tok = jnp.pad(order // MOE_TOP_K, (0, n_slots - n))
    slot_tok = jnp.zeros((n_slots,), I32)
    for e in range(MOE_EXPERTS):
        in_group = (slot_e == e) & (slot - pstarts[e] < counts[e]) & (slot < n_used * tm)
        slot_tok = jnp.where(in_group, jnp.roll(order_tok, shift[e]), slot_tok)
    dest = (inv_order + jnp.sum(jnp.where(e_flat[:, None] == experts[None, :], shift[None, :], 0), axis=1)).astype(I32)

    y = _moe_ffn(x2, slot_tok, w1.astype(BF16), w3.astype(BF16), w2.astype(BF16), blk_expert.astype(I32),
                 n_used.reshape(1), tm)
    return _moe_combine(x2, y, dest, gates, g, b, alpha)


def kernel(x, p, positions, swa_w_qkv, swa_sinks, swa_w_o, dense_w1, dense_w3, dense_w2, dsa_w_in, dsa_kv_norm,
           dsa_w_uk, dsa_w_uv, dsa_w_o, moe_router, moe_w1, moe_w3, moe_w2, ln_g, ln_b, ple_w_p, ple_w_g):
    batch, seq, d = x.shape
    depth = ln_g.shape[0]
    alpha = (2.0 * depth) ** 0.25
    t = batch * seq
    x2 = x.reshape(t, d)
    cos_t, sin_t = _rope_tables(positions)
    for i in range(depth):
        j = i // 2
        if i % 2 == 0:
            q, k2, v_t = _swa_qkv(x2, swa_w_qkv[j], cos_t, sin_t)
            o = _swa_attn(q, k2, v_t, swa_sinks[j], batch, seq)
            x2 = _proj_ln(o, swa_w_o[j], x2, ln_g[i, 0], ln_b[i, 0], alpha)
            x2 = _ffn_dense(x2, dense_w1[j], dense_w3[j], dense_w2[j], ln_g[i, 1], ln_b[i, 1], alpha)
        else:
            o = _dsa_mixer_out(x2, cos_t, sin_t, dsa_w_in[j], dsa_kv_norm[j], dsa_w_uk[j], dsa_w_uv[j], batch, seq)
            x2 = _proj_ln(o, dsa_w_o[j], x2, ln_g[i, 0], ln_b[i, 0], alpha)
            x2 = _moe_layer(x2, moe_router[j], moe_w1[j], moe_w3[j], moe_w2[j], ln_g[i, 1], ln_b[i, 1], alpha)
        x2 = _ple(x2, p[i].reshape(t, -1), ple_w_g[i], ple_w_p[i])
    return x2.reshape(batch, seq, d)
```

```python
import functools

import jax
import jax.numpy as jnp
from jax import lax
from jax.experimental import pallas as pl
from jax.experimental.pallas import tpu as pltpu

F32 = jnp.float32
BF16 = jnp.bfloat16
I32 = jnp.int32

LANES = 128
ROPE_THETA = 10000.0
LN_EPS = 1e-5
RMS_EPS = 1e-6
SWA_HEADS = 32
SWA_KV_HEADS = 4
SWA_GROUP = SWA_HEADS // SWA_KV_HEADS
SWA_HEAD_DIM = 64
SWA_WINDOW = 128
SWA_V_PAD = SWA_HEAD_DIM + 16
MLA_HEADS = 16
MLA_NOPE = 128
MLA_ROPE = 64
MLA_V = 128
MLA_LATENT = 512
MLA_SCALE = (MLA_NOPE + MLA_ROPE) ** -0.5
MLA_QK_PAD = 256
MLA_V_PAD = MLA_V + 16
LOG2_E = 1.4426950408889634
IDX_HEADS = 16
IDX_DIM = 64
IDX_ROPE = 32
DSA_TOPK = 256
MOE_EXPERTS = 8
MOE_TOP_K = 2
NEG_BIG = -1e30
M_INIT = -1e20
VMEM_LIMIT = 56 * 1024 * 1024


def _params(sem, vmem=VMEM_LIMIT, flags=None):
    return pltpu.CompilerParams(dimension_semantics=sem, vmem_limit_bytes=vmem, flags=flags)


def _layer_norm(y, g, b):
    mu = jnp.mean(y, axis=-1, keepdims=True)
    d = y - mu
    var = jnp.mean(d * d, axis=-1, keepdims=True)
    return d * lax.rsqrt(var + LN_EPS) * g + b


def _rope_lanes(a, cs, sn, half):
    lane = lax.broadcasted_iota(I32, a.shape, 1)
    first = (lane % 64) < half
    rot = jnp.where(first, pltpu.roll(a, LANES - half, 1), pltpu.roll(a, half, 1))
    return a * cs + rot * sn


def _rope_table_kernel(pos_ref, invf_ref, sgn_ref, cos_ref, sin_ref):
    ang = pos_ref[...].astype(F32) * invf_ref[...]
    cos_ref[...] = jnp.cos(ang)
    sin_ref[...] = jnp.sin(ang) * sgn_ref[...]


def _rope_tables(positions):
    t = positions.size
    pos = positions.reshape(t, 1).astype(I32)
    lane = jnp.arange(LANES)
    in64 = lane % 64
    f64 = ROPE_THETA ** (-jnp.arange(0, 64, 2, dtype=F32) / 64)
    f32_ = ROPE_THETA ** (-jnp.arange(0, IDX_ROPE, 2, dtype=F32) / IDX_ROPE)
    invf_a = f64[in64 % 32]
    sgn_a = jnp.where(in64 < 32, -1.0, 1.0).astype(F32)
    invf_b = jnp.where(in64 < IDX_ROPE, f32_[in64 % (IDX_ROPE // 2)], 0.0)
    sgn_b = jnp.where(in64 < IDX_ROPE // 2, -1.0, 1.0).astype(F32)
    invf = jnp.concatenate([invf_a, invf_b])[None, :]
    sgn = jnp.concatenate([sgn_a, sgn_b])[None, :]
    tm = min(t, 1024)
    return pl.pallas_call(
        _rope_table_kernel,
        name="rope_tables",
        grid=(t // tm,),
        in_specs=[pl.BlockSpec((tm, 1), lambda i: (i, 0)),
                  pl.BlockSpec((1, 2 * LANES), lambda i: (0, 0)),
                  pl.BlockSpec((1, 2 * LANES), lambda i: (0, 0))],
        out_specs=[pl.BlockSpec((tm, 2 * LANES), lambda i: (i, 0)),
                   pl.BlockSpec((tm, 2 * LANES), lambda i: (i, 0))],
        out_shape=[jax.ShapeDtypeStruct((t, 2 * LANES), F32)] * 2,
        compiler_params=_params(("parallel",)),
    )(pos, invf, sgn)


def _swa_qkv_kernel(x_ref, w_ref, wvt_ref, cos_ref, sin_ref, q_ref, k_ref, vt_ref, *, nq, nk):
    xb = x_ref[...].astype(BF16)
    acc = jnp.dot(xb, w_ref[...], preferred_element_type=F32)
    cs = cos_ref[:, :LANES]
    sn = sin_ref[:, :LANES]
    scale = SWA_HEAD_DIM ** -0.5 * LOG2_E
    for c in range(nq // LANES):
        a = acc[:, c * LANES:(c + 1) * LANES]
        q_ref[:, c * LANES:(c + 1) * LANES] = (_rope_lanes(a, cs, sn, 32) * scale).astype(BF16)
    low = lax.broadcasted_iota(I32, (acc.shape[0], LANES), 1) < SWA_HEAD_DIM
    for c in range(nk // LANES):
        kr = _rope_lanes(acc[:, nq + c * LANES:nq + (c + 1) * LANES], cs, sn, 32)
        swapped = pltpu.roll(kr, SWA_HEAD_DIM, 1)
        placements = (jnp.where(low, kr, 0.0), jnp.where(low, 0.0, swapped),
                      jnp.where(low, swapped, 0.0), jnp.where(low, 0.0, kr))
        for n, slab in enumerate(placements):
            k_ref[:, (4 * c + n) * LANES:(4 * c + n + 1) * LANES] = slab.astype(BF16)
    vt = lax.dot_general(wvt_ref[...], xb, (((1,), (1,)), ((), ())), preferred_element_type=F32)
    ones = jnp.ones((SWA_V_PAD - SWA_HEAD_DIM, xb.shape[0]), BF16)
    for kh in range(SWA_KV_HEADS):
        vt_ref[kh * SWA_V_PAD:kh * SWA_V_PAD + SWA_HEAD_DIM, :] = (
            vt[kh * SWA_HEAD_DIM:(kh + 1) * SWA_HEAD_DIM, :].astype(BF16))
        vt_ref[kh * SWA_V_PAD + SWA_HEAD_DIM:(kh + 1) * SWA_V_PAD, :] = ones


def _swa_qkv(x2, w_qkv, cos_t, sin_t):
    t, d = x2.shape
    nq = SWA_HEADS * SWA_HEAD_DIM
    nk = SWA_KV_HEADS * SWA_HEAD_DIM
    nk2 = SWA_KV_HEADS * 2 * LANES
    nvt = SWA_KV_HEADS * SWA_V_PAD
    tm = min(t, 512)
    wb = w_qkv.astype(BF16)
    return pl.pallas_call(
        functools.partial(_swa_qkv_kernel, nq=nq, nk=nk),
        name="swa_qkv",
        grid=(t // tm,),
        in_specs=[pl.BlockSpec((tm, d), lambda i: (i, 0)),
                  pl.BlockSpec((d, nq + nk), lambda i: (0, 0)),
                  pl.BlockSpec((nk, d), lambda i: (0, 0)),
                  pl.BlockSpec((tm, 2 * LANES), lambda i: (i, 0)),
                  pl.BlockSpec((tm, 2 * LANES), lambda i: (i, 0))],
        out_specs=[pl.BlockSpec((tm, nq), lambda i: (i, 0)),
                   pl.BlockSpec((tm, nk2), lambda i: (i, 0)),
                   pl.BlockSpec((nvt, tm), lambda i: (0, i))],
        out_shape=[jax.ShapeDtypeStruct((t, nq), BF16),
                   jax.ShapeDtypeStruct((t, nk2), BF16),
                   jax.ShapeDtypeStruct((nvt, t), BF16)],
        compiler_params=_params(("parallel",)),
    )(x2, wb[:, :nq + nk], jnp.transpose(wb[:, nq + nk:]), cos_t, sin_t)


def _swa_attn_kernel(sink_ref, q_ref, kp_ref, kc_ref, vp_ref, vc_ref, o_ref):
    blk = pl.program_id(1)
    bq = q_ref.shape[0]
    dh = SWA_HEAD_DIM
    slabs = SWA_GROUP // 2
    cols = slabs * bq
    kj = lax.broadcasted_iota(I32, (2 * bq, cols), 0)
    qi = lax.broadcasted_iota(I32, (2 * bq, cols), 1) % bq
    rel = qi + bq - kj
    allowed = (rel >= 0) & (rel < SWA_WINDOW) & ((kj >= bq) | (blk > 0))
    for kh in range(SWA_KV_HEADS):
        base = kh * SWA_GROUP * dh
        qs = jnp.concatenate([q_ref[:, base + s * LANES:base + (s + 1) * LANES] for s in range(slabs)], axis=0)
        vcat = jnp.concatenate([vp_ref[kh * SWA_V_PAD:(kh + 1) * SWA_V_PAD, :],
                                vc_ref[kh * SWA_V_PAD:(kh + 1) * SWA_V_PAD, :]], axis=1)
        for half in range(2):
            ks = (2 * kh + half) * LANES
            kcat = jnp.concatenate([kp_ref[:, ks:ks + LANES], kc_ref[:, ks:ks + LANES]], axis=0)
            heads = [kh * SWA_GROUP + 2 * s + half for s in range(slabs)]
            sink = jnp.concatenate([jnp.full((1, bq), sink_ref[h] * LOG2_E, F32) for h in heads], axis=1)
            logits = lax.dot_general(kcat, qs, (((1,), (1,)), ((), ())), preferred_element_type=F32)
            logits = jnp.where(allowed, logits, NEG_BIG)
            m = jnp.maximum(jnp.max(logits, axis=0, keepdims=True), sink)
            p = jnp.exp2(logits - m).astype(BF16)
            pv = jnp.dot(vcat, p, preferred_element_type=F32)
            inv = 1.0 / (pv[dh:dh + 1, :] + jnp.exp2(sink - m))
            o = (pv[:dh, :] * inv).astype(BF16)
            for s, h in enumerate(heads):
                o_ref[h * dh:(h + 1) * dh, :] = o[:, s * bq:(s + 1) * bq]


def _swa_attn(q, k2, v_t, sinks, batch, seq):
    t, nq = q.shape
    nk2 = k2.shape[1]
    nvt = v_t.shape[0]
    bq = SWA_WINDOW
    nb = seq // bq

    def cur(b, i):
        return (b * nb + i, 0)

    def prev(b, i):
        return (b * nb + jnp.maximum(i - 1, 0), 0)

    def cur_t(b, i):
        return (0, b * nb + i)

    def prev_t(b, i):
        return (0, b * nb + jnp.maximum(i - 1, 0))

    return pl.pallas_call(
        _swa_attn_kernel,
        name="swa_attn",
        grid=(batch, nb),
        in_specs=[pl.BlockSpec(memory_space=pltpu.SMEM),
                  pl.BlockSpec((bq, nq), cur),
                  pl.BlockSpec((bq, nk2), prev),
                  pl.BlockSpec((bq, nk2), cur),
                  pl.BlockSpec((nvt, bq), prev_t),
                  pl.BlockSpec((nvt, bq), cur_t)],
        out_specs=pl.BlockSpec((nq, bq), cur_t),
        out_shape=jax.ShapeDtypeStruct((nq, t), BF16),
        compiler_params=_params(("parallel", "parallel")),
    )(sinks.astype(F32), q, k2, k2, v_t, v_t)


def _proj_ln_kernel(at_ref, w_ref, res_ref, g_ref, b_ref, o_ref, *, alpha):
    mix = lax.dot_general(at_ref[...], w_ref[...], (((0,), (0,)), ((), ())), preferred_element_type=F32)
    o_ref[...] = _layer_norm(alpha * res_ref[...] + mix, g_ref[...], b_ref[...])


def _proj_ln(a_t, w, res, g, b, alpha):
    k, t = a_t.shape
    d = w.shape[1]
    tm = min(t, 512)
    return pl.pallas_call(
        functools.partial(_proj_ln_kernel, alpha=alpha),
        name="proj_ln",
        grid=(t // tm,),
        in_specs=[pl.BlockSpec((k, tm), lambda i: (0, i)),
                  pl.BlockSpec((k, d), lambda i: (0, 0)),
                  pl.BlockSpec((tm, d), lambda i: (i, 0)),
                  pl.BlockSpec((1, d), lambda i: (0, 0)),
                  pl.BlockSpec((1, d), lambda i: (0, 0))],
        out_specs=pl.BlockSpec((tm, d), lambda i: (i, 0)),
        out_shape=jax.ShapeDtypeStruct((t, d), F32),
        compiler_params=_params(("parallel",)),
    )(a_t, w.astype(BF16), res, g[None, :], b[None, :])


def _ffn_dense_kernel(x_ref, w1_ref, w3_ref, w2_ref, g_ref, b_ref, o_ref, xb_ref, acc_ref, *, alpha):
    f = pl.program_id(1)

    @pl.when(f == 0)
    def _():
        xb_ref[...] = x_ref[...].astype(BF16)
        acc_ref[...] = jnp.zeros_like(acc_ref)

    xb = xb_ref[...]
    gate = jnp.dot(xb, w1_ref[...], preferred_element_type=F32)
    up = jnp.dot(xb, w3_ref[...], preferred_element_type=F32)
    h = (jax.nn.silu(gate) * up).astype(BF16)
    acc_ref[...] += jnp.dot(h, w2_ref[...], preferred_element_type=F32)

    @pl.when(f == pl.num_programs(1) - 1)
    def _():
        o_ref[...] = _layer_norm(alpha * x_ref[...] + acc_ref[...], g_ref[...], b_ref[...])


def _ffn_tile(f_dim, largest=512):
    for tf in (largest, 512, 256, 128):
        if f_dim % tf == 0:
            return tf
    return f_dim


def _ffn_dense(x2, w1, w3, w2, g, b, alpha):
    t, d = x2.shape
    f_dim = w1.shape[1]
    tm = min(t, 512)
    tf = _ffn_tile(f_dim)
    return pl.pallas_call(
        functools.partial(_ffn_dense_kernel, alpha=alpha),
        name="ffn_dense",
        grid=(t // tm, f_dim // tf),
        in_specs=[pl.BlockSpec((tm, d), lambda i, f: (i, 0)),
                  pl.BlockSpec((d, tf), lambda i, f: (0, f)),
                  pl.BlockSpec((d, tf), lambda i, f: (0, f)),
                  pl.BlockSpec((tf, d), lambda i, f: (f, 0)),
                  pl.BlockSpec((1, d), lambda i, f: (0, 0)),
                  pl.BlockSpec((1, d), lambda i, f: (0, 0))],
        out_specs=pl.BlockSpec((tm, d), lambda i, f: (i, 0)),
        out_shape=jax.ShapeDtypeStruct((t, d), F32),
        scratch_shapes=[pltpu.VMEM((tm, d), BF16), pltpu.VMEM((tm, d), F32)],
        compiler_params=_params(("parallel", "arbitrary")),
    )(x2, w1.astype(BF16), w3.astype(BF16), w2.astype(BF16), g[None, :], b[None, :])


def _ple_kernel(x_ref, p_ref, wg_ref, wp_ref, o_ref):
    x = x_ref[...]
    gate = jax.nn.sigmoid(jnp.dot(x.astype(BF16), wg_ref[...], preferred_element_type=F32))
    emb = jnp.dot(p_ref[...].astype(BF16), wp_ref[...], preferred_element_type=F32)
    o_ref[...] = x + gate * emb


def _ple(x2, p2, w_g, w_p):
    t, d = x2.shape
    pd = p2.shape[1]
    tm = min(t, 512)
    return pl.pallas_call(
        _ple_kernel,
        name="ple",
        grid=(t // tm,),
        in_specs=[pl.BlockSpec((tm, d), lambda i: (i, 0)),
                  pl.BlockSpec((tm, pd), lambda i: (i, 0)),
                  pl.BlockSpec((d, d), lambda i: (0, 0)),
                  pl.BlockSpec((pd, d), lambda i: (0, 0))],
        out_specs=pl.BlockSpec((tm, d), lambda i: (i, 0)),
        out_shape=jax.ShapeDtypeStruct((t, d), F32),
        compiler_params=_params(("parallel",)),
    )(x2, p2, w_g.astype(BF16), w_p.astype(BF16))


def _dsa_q_kernel(x_ref, w_ref, cos_ref, sin_ref, q_ref):
    xb = x_ref[...].astype(BF16)
    acc = jnp.dot(xb, w_ref[...], preferred_element_type=F32)
    cs = cos_ref[:, :LANES]
    sn = sin_ref[:, :LANES]
    scale = MLA_SCALE * LOG2_E
    for h in range(MLA_HEADS):
        base = h * MLA_QK_PAD
        q_ref[:, base:base + LANES] = (acc[:, base:base + LANES] * scale).astype(BF16)
        a = acc[:, base + LANES:base + 2 * LANES]
        q_ref[:, base + LANES:base + 2 * LANES] = (_rope_lanes(a, cs, sn, 32) * scale).astype(BF16)


def _dsa_q(x2, w_q, cos_t, sin_t):
    t, d = x2.shape
    n = w_q.shape[1]
    tm = min(t, 512)
    return pl.pallas_call(
        _dsa_q_kernel,
        name="dsa_q",
        grid=(t // tm,),
        in_specs=[pl.BlockSpec((tm, d), lambda i: (i, 0)),
                  pl.BlockSpec((d, n), lambda i: (0, 0)),
                  pl.BlockSpec((tm, 2 * LANES), lambda i: (i, 0)),
                  pl.BlockSpec((tm, 2 * LANES), lambda i: (i, 0))],
        out_specs=pl.BlockSpec((tm, n), lambda i: (i, 0)),
        out_shape=jax.ShapeDtypeStruct((t, n), BF16),
        compiler_params=_params(("parallel",)),
    )(x2, w_q, cos_t, sin_t)


def _dsa_small_kernel(x_ref, w_ref, cos_ref, sin_ref, kvn_ref, ckv_ref, kr_ref, ki_ref, wi_ref):
    xb = x_ref[...].astype(BF16)
    acc = jnp.dot(xb, w_ref[...], preferred_element_type=F32)
    c = acc[:, :MLA_LATENT]
    ms = jnp.mean(c * c, axis=-1, keepdims=True)
    ckv_ref[...] = (c * lax.rsqrt(ms + RMS_EPS) * kvn_ref[...]).astype(BF16)
    o = MLA_LATENT
    kr_ref[...] = _rope_lanes(acc[:, o:o + LANES], cos_ref[:, :LANES], sin_ref[:, :LANES], 32).astype(BF16)
    ki_ref[...] = _rope_lanes(acc[:, o + LANES:o + 2 * LANES], cos_ref[:, LANES:], sin_ref[:, LANES:],
                              IDX_ROPE // 2).astype(BF16)
    wi_ref[...] = acc[:, o + 2 * LANES:o + 3 * LANES] * (IDX_HEADS ** -0.5)


def _dsa_small(x2, w_small, kv_norm, cos_t, sin_t):
    t, d = x2.shape
    n = w_small.shape[1]
    tm = min(t, 512)
    return pl.pallas_call(
        _dsa_small_kernel,
        name="dsa_small",
        grid=(t // tm,),
        in_specs=[pl.BlockSpec((tm, d), lambda i: (i, 0)),
                  pl.BlockSpec((d, n), lambda i: (0, 0)),
                  pl.BlockSpec((tm, 2 * LANES), lambda i: (i, 0)),
                  pl.BlockSpec((tm, 2 * LANES), lambda i: (i, 0)),
                  pl.BlockSpec((1, MLA_LATENT), lambda i: (0, 0))],
        out_specs=[pl.BlockSpec((tm, MLA_LATENT), lambda i: (i, 0)),
                   pl.BlockSpec((tm, LANES), lambda i: (i, 0)),
                   pl.BlockSpec((tm, LANES), lambda i: (i, 0)),
                   pl.BlockSpec((tm, LANES), lambda i: (i, 0))],
        out_shape=[jax.ShapeDtypeStruct((t, MLA_LATENT), BF16),
                   jax.ShapeDtypeStruct((t, LANES), BF16),
                   jax.ShapeDtypeStruct((t, LANES), BF16),
                   jax.ShapeDtypeStruct((t, LANES), F32)],
        compiler_params=_params(("parallel",)),
    )(x2, w_small, cos_t, sin_t, kv_norm[None, :].astype(F32))


def _dsa_qi_kernel(x_ref, w_ref, cos_ref, sin_ref, q_ref):
    xb = x_ref[...].astype(BF16)
    acc = jnp.dot(xb, w_ref[...], preferred_element_type=F32)
    cs = cos_ref[:, LANES:]
    sn = sin_ref[:, LANES:]
    scale = IDX_DIM ** -0.5
    for c in range(acc.shape[1] // LANES):
        a = acc[:, c * LANES:(c + 1) * LANES]
        q_ref[:, c * LANES:(c + 1) * LANES] = (_rope_lanes(a, cs, sn, IDX_ROPE // 2) * scale).astype(BF16)


def _dsa_qi(x2, w_qi, cos_t, sin_t):
    t, d = x2.shape
    n = w_qi.shape[1]
    tm = min(t, 512)
    return pl.pallas_call(
        _dsa_qi_kernel,
        name="dsa_qi",
        grid=(t // tm,),
        in_specs=[pl.BlockSpec((tm, d), lambda i: (i, 0)),
                  pl.BlockSpec((d, n), lambda i: (0, 0)),
                  pl.BlockSpec((tm, 2 * LANES), lambda i: (i, 0)),
                  pl.BlockSpec((tm, 2 * LANES), lambda i: (i, 0))],
        out_specs=pl.BlockSpec((tm, n), lambda i: (i, 0)),
        out_shape=jax.ShapeDtypeStruct((t, n), BF16),
        compiler_params=_params(("parallel",)),
    )(x2, w_qi, cos_t, sin_t)


def _dsa_kv_kernel(c_ref, kr_ref, wk_ref, wvt_ref, k_ref, vt_ref):
    c = c_ref[...]
    kn = jnp.dot(c, wk_ref[...], preferred_element_type=F32)
    kr = kr_ref[...]
    for h in range(MLA_HEADS):
        base = h * MLA_QK_PAD
        k_ref[:, base:base + LANES] = kn[:, h * LANES:(h + 1) * LANES].astype(BF16)
        k_ref[:, base + LANES:base + 2 * LANES] = kr
    vt = lax.dot_general(wvt_ref[...], c, (((1,), (1,)), ((), ())), preferred_element_type=F32)
    ones = jnp.ones((MLA_V_PAD - MLA_V, c.shape[0]), BF16)
    for h in range(MLA_HEADS):
        vt_ref[h * MLA_V_PAD:h * MLA_V_PAD + MLA_V, :] = vt[h * MLA_V:(h + 1) * MLA_V, :].astype(BF16)
        vt_ref[h * MLA_V_PAD + MLA_V:(h + 1) * MLA_V_PAD, :] = ones


def _dsa_kv(ckv, kr, w_uk_all, w_uv_t):
    t, c_dim = ckv.shape
    nk = MLA_HEADS * MLA_QK_PAD
    nv = w_uv_t.shape[0]
    nv_pad = MLA_HEADS * MLA_V_PAD
    tm = min(t, 512)
    return pl.pallas_call(
        _dsa_kv_kernel,
        name="dsa_kv",
        grid=(t // tm,),
        in_specs=[pl.BlockSpec((tm, c_dim), lambda i: (i, 0)),
                  pl.BlockSpec((tm, LANES), lambda i: (i, 0)),
                  pl.BlockSpec((c_dim, MLA_HEADS * MLA_NOPE), lambda i: (0, 0)),
                  pl.BlockSpec((nv, c_dim), lambda i: (0, 0))],
        out_specs=[pl.BlockSpec((tm, nk), lambda i: (i, 0)),
                   pl.BlockSpec((nv_pad, tm), lambda i: (0, i))],
        out_shape=[jax.ShapeDtypeStruct((t, nk), BF16),
                   jax.ShapeDtypeStruct((nv_pad, t), BF16)],
        compiler_params=_params(("parallel",)),
    )(ckv, kr, w_uk_all, w_uv_t)


def _float_sort_key(s):
    bits = lax.bitcast_convert_type(s, I32)
    return bits ^ ((bits >> 31) & jnp.int32(0x7FFFFFFF))


def _dsa_select_kernel(k_ref, q_ref, w_ref, bias_ref, key_ref, *, sc, cc, unroll, n_sel):
    j = pl.program_id(1)
    s_len, tq = key_ref.shape
    n_chunks = s_len // sc
    q_lo = j * tq
    n_need = (q_lo + tq + sc * unroll - 1) // (sc * unroll) * unroll
    qf = q_ref[...].reshape(IDX_HEADS * tq, IDX_DIM)
    w_row = w_ref[...]
    q_pos = q_lo + lax.broadcasted_iota(I32, (sc, tq), 1)
    k_off = lax.broadcasted_iota(I32, (sc, tq), 0)
    min_key = jnp.int32(-2 ** 31)

    def score_chunks(i, carry):
        for u in range(unroll):
            row0 = pl.multiple_of((i * unroll + u) * sc, sc)
            kc = k_ref[pl.ds(row0, sc), :]
            d = lax.dot_general(kc, qf, (((1,), (1,)), ((), ())), preferred_element_type=F32)
            s = jnp.zeros((sc, tq), F32)
            for h in range(IDX_HEADS):
                s = s + jnp.maximum(d[:, h * tq:(h + 1) * tq], 0.0) * w_row[:, h * tq:(h + 1) * tq]
            causal = (row0 + k_off) <= q_pos
            key_ref[pl.ds(row0, sc), :] = jnp.where(causal, _float_sort_key(s), min_key)
        return carry

    lax.fori_loop(0, n_need // unroll, score_chunks, 0)

    n_cnt = (n_need * sc + cc - 1) // cc

    def fill_chunk(c, carry):
        row0 = pl.multiple_of(c * sc, sc)
        key_ref[pl.ds(row0, sc), :] = jnp.full((sc, tq), min_key, I32)
        return carry

    lax.fori_loop(n_need, n_cnt * (cc // sc), fill_chunk, 0)

    def count_ge(cand):
        def body(c, cnt):
            row0 = pl.multiple_of(c * cc, cc)
            hit = jnp.where(key_ref[pl.ds(row0, cc), :] >= cand, 1, 0).astype(I32)
            return cnt + jnp.sum(hit.reshape(cc // 32, 32, tq), axis=0)

        cnt = lax.fori_loop(0, n_cnt, body, jnp.zeros((32, tq), I32))
        return jnp.sum(cnt, axis=0, keepdims=True)

    zero = jnp.zeros((1, tq), I32)
    thr = jnp.where(count_ge(zero) >= n_sel, zero, jnp.full((1, tq), min_key, I32))

    def bit_step(it, thr):
        cand = thr + (jnp.int32(1) << (30 - it))
        return jnp.where(count_ge(cand) >= n_sel, cand, thr)

    thr = lax.fori_loop(0, 31, bit_step, thr)

    def write_chunk(c, carry):
        row0 = pl.multiple_of(c * sc, sc)
        keep = (key_ref[pl.ds(row0, sc), :] >= thr) & ((row0 + k_off) <= q_pos)
        bias_ref[pl.ds(row0, sc), :] = jnp.where(keep, 0.0, NEG_BIG).astype(F32)
        return carry

    def mask_chunk(c, carry):
        row0 = pl.multiple_of(c * sc, sc)
        bias_ref[pl.ds(row0, sc), :] = jnp.full((sc, tq), NEG_BIG, F32)
        return carry

    lax.fori_loop(0, n_need, write_chunk, 0)
    lax.fori_loop(n_need, n_chunks, mask_chunk, 0)


def _dsa_select(k_i, q_i_t, w_rows, batch, seq, n_sel):
    tq = LANES
    nq = seq // tq
    sc = min(seq, 256)
    cc = min(seq, 1024)
    return pl.pallas_call(
        functools.partial(_dsa_select_kernel, sc=sc, cc=cc, unroll=2 if (seq // sc) % 2 == 0 else 1, n_sel=n_sel),
        name="dsa_select",
        grid=(batch, nq),
        in_specs=[pl.BlockSpec((None, seq, IDX_DIM), lambda b, j: (b, 0, 0)),
                  pl.BlockSpec((None, IDX_HEADS, tq, IDX_DIM), lambda b, j: (b, 0, j, 0)),
                  pl.BlockSpec((None, None, 1, IDX_HEADS * tq), lambda b, j: (b, j, 0, 0))],
        out_specs=pl.BlockSpec((None, seq, tq), lambda b, j: (b, 0, j)),
        out_shape=jax.ShapeDtypeStruct((batch, seq, seq), F32),
        scratch_shapes=[pltpu.VMEM((seq, tq), I32)],
        compiler_params=_params(("parallel", "parallel")),
    )(k_i, q_i_t, w_rows)


def _dsa_attn_kernel(q_ref, k_ref, vt_ref, bias_ref, o_ref, m_ref, a_ref, *head_refs, sc):
    s_refs = head_refs[:MLA_HEADS]
    acc_refs = head_refs[MLA_HEADS:]
    j = pl.program_id(1)
    c = pl.program_id(2)
    tq = q_ref.shape[0]
    last = (j * tq + tq - 1) // sc
    cur = c % 2
    prev = 1 - cur

    @pl.when(c == 0)
    def _():
        for h in range(MLA_HEADS):
            acc_refs[h][...] = jnp.zeros_like(acc_refs[h])
            s_refs[h][1] = jnp.full((sc, tq), NEG_BIG, F32)
        m_ref[...] = jnp.full_like(m_ref, M_INIT)
        a_ref[...] = jnp.ones_like(a_ref)

    def finish_prev():
        m_prev = m_ref[...]
        a_prev = a_ref[...]
        for h in range(MLA_HEADS):
            p = jnp.exp2(s_refs[h][prev] - m_prev[h:h + 1, :]).astype(BF16)
            pv = jnp.dot(vt_ref[h * MLA_V_PAD:(h + 1) * MLA_V_PAD, :], p, preferred_element_type=F32)
            acc_refs[h][...] = a_prev[h:h + 1, :] * acc_refs[h][...] + pv
        return m_prev

    @pl.when(c <= last)
    def _():
        m_prev = finish_prev()
        chunk_max = []
        for h in range(MLA_HEADS):
            qh = q_ref[:, h * MLA_QK_PAD:(h + 1) * MLA_QK_PAD]
            kh = k_ref[:, h * MLA_QK_PAD:(h + 1) * MLA_QK_PAD]
            s = lax.dot_general(kh, qh, (((1,), (1,)), ((), ())), preferred_element_type=F32) + bias_ref[...]
            s_refs[h][cur] = s
            chunk_max.append(jnp.max(s, axis=0, keepdims=True))
        m_new = jnp.maximum(m_prev, jnp.concatenate(chunk_max, axis=0))
        a_ref[...] = jnp.exp2(m_prev - m_new)
        m_ref[...] = m_new

    @pl.when(c == last + 1)
    def _():
        finish_prev()
        for h in range(MLA_HEADS):
            inv = 1.0 / acc_refs[h][MLA_V:MLA_V + 1, :]
            o_ref[h * MLA_V:(h + 1) * MLA_V, :] = (acc_refs[h][:MLA_V, :] * inv).astype(BF16)


def _dsa_attn(q_cat, k_cat, v_t, bias, batch, seq):
    t, nqk = q_cat.shape
    nv_pad = v_t.shape[0]
    nv = MLA_HEADS * MLA_V
    tq = min(seq, 512)
    sc = min(seq, 256)
    nq = seq // tq
    nc = seq // sc

    def last_chunk(j):
        return (j * tq + tq - 1) // sc

    return pl.pallas_call(
        functools.partial(_dsa_attn_kernel, sc=sc),
        name="dsa_attn",
        grid=(batch, nq, nc + 1),
        in_specs=[pl.BlockSpec((tq, nqk), lambda b, j, c: (b * nq + j, 0)),
                  pl.BlockSpec((sc, nqk), lambda b, j, c: (b * nc + jnp.minimum(c, last_chunk(j)), 0)),
                  pl.BlockSpec((nv_pad, sc), lambda b, j, c: (0, b * nc + jnp.clip(c - 1, 0, last_chunk(j)))),
                  pl.BlockSpec((None, sc, tq), lambda b, j, c: (b, jnp.minimum(c, last_chunk(j)), j))],
        out_specs=pl.BlockSpec((nv, tq), lambda b, j, c: (0, b * nq + j)),
        out_shape=jax.ShapeDtypeStruct((nv, t), BF16),
        scratch_shapes=([pltpu.VMEM((MLA_HEADS, tq), F32), pltpu.VMEM((MLA_HEADS, tq), F32)]
                        + [pltpu.VMEM((2, sc, tq), F32)] * MLA_HEADS
                        + [pltpu.VMEM((MLA_V_PAD, tq), F32)] * MLA_HEADS),
        compiler_params=_params(("parallel", "parallel", "arbitrary")),
    )(q_cat, k_cat, v_t, bias)


def _dsa_mixer_out(x2, cos_t, sin_t, w_in, kv_norm, w_uk, w_uv, batch, seq):
    t, d = x2.shape
    nq = MLA_HEADS * (MLA_NOPE + MLA_ROPE)
    o_c = nq
    o_kr = o_c + MLA_LATENT
    o_qi = o_kr + MLA_ROPE
    o_ki = o_qi + IDX_HEADS * IDX_DIM
    o_wi = o_ki + IDX_DIM
    wb = w_in.astype(BF16)
    wq = wb[:, :nq].reshape(d, MLA_HEADS, MLA_NOPE + MLA_ROPE)
    wq = jnp.pad(wq, ((0, 0), (0, 0), (0, MLA_QK_PAD - MLA_NOPE - MLA_ROPE))).reshape(d, MLA_HEADS * MLA_QK_PAD)
    pad_to_lanes = lambda w: jnp.pad(w, ((0, 0), (0, LANES - w.shape[1])))
    w_small = jnp.concatenate([wb[:, o_c:o_kr], pad_to_lanes(wb[:, o_kr:o_qi]), pad_to_lanes(wb[:, o_ki:o_wi]),
                               pad_to_lanes(wb[:, o_wi:])], axis=1)
    w_qi = wb[:, o_qi:o_ki]
    w_uk_all = jnp.transpose(w_uk, (1, 0, 2)).reshape(MLA_LATENT, MLA_HEADS * MLA_NOPE).astype(BF16)
    w_uv_t = jnp.transpose(w_uv, (0, 2, 1)).reshape(MLA_HEADS * MLA_V, MLA_LATENT).astype(BF16)

    q_cat = _dsa_q(x2, wq, cos_t, sin_t)
    ckv, kr, ki, wi = _dsa_small(x2, w_small, kv_norm, cos_t, sin_t)
    qi = _dsa_qi(x2, w_qi, cos_t, sin_t)
    k_cat, v_t = _dsa_kv(ckv, kr, w_uk_all, w_uv_t)

    tq = LANES
    nqb = seq // tq
    k_i = ki[:, :IDX_DIM].reshape(batch, seq, IDX_DIM)
    q_i_t = jnp.transpose(qi.reshape(batch, seq, IDX_HEADS, IDX_DIM), (0, 2, 1, 3))
    w_rows = jnp.transpose(wi[:, :IDX_HEADS].reshape(batch, nqb, tq, IDX_HEADS), (0, 1, 3, 2))
    w_rows = w_rows.reshape(batch, nqb, 1, IDX_HEADS * tq)
    n_sel = min(DSA_TOPK, seq // 4)
    bias = _dsa_select(k_i, q_i_t, w_rows, batch, seq, n_sel)
    o_t = _dsa_attn(q_cat, k_cat, v_t, bias, batch, seq)
    return o_t


def _router_kernel(x_ref, w_ref, idx_ref, gate_ref):
    logits = jnp.dot(x_ref[...], w_ref[...], precision=lax.Precision.HIGHEST, preferred_element_type=F32)
    lane = lax.broadcasted_iota(I32, logits.shape, 1)
    lg = jnp.where(lane < MOE_EXPERTS, logits, -jnp.inf)
    v1 = jnp.max(lg, axis=-1, keepdims=True)
    i1 = jnp.min(jnp.where(lg == v1, lane, LANES), axis=-1, keepdims=True)
    lg2 = jnp.where(lane == i1, -jnp.inf, lg)
    v2 = jnp.max(lg2, axis=-1, keepdims=True)
    i2 = jnp.min(jnp.where(lg2 == v2, lane, LANES), axis=-1, keepdims=True)
    e2 = jnp.exp(v2 - v1)
    den = 1.0 + e2
    idx_ref[...] = jnp.where(lane == 0, i1, jnp.where(lane == 1, i2, 0))
    gate_ref[...] = jnp.where(lane == 0, 1.0 / den, jnp.where(lane == 1, e2 / den, 0.0))


def _router(x2, w_router):
    t, d = x2.shape
    wr = jnp.pad(w_router.astype(F32), ((0, 0), (0, LANES - w_router.shape[1])))
    tm = min(t, 512)
    return pl.pallas_call(
        _router_kernel,
        name="moe_router",
        grid=(t // tm,),
        in_specs=[pl.BlockSpec((tm, d), lambda i: (i, 0)),
                  pl.BlockSpec((d, LANES), lambda i: (0, 0))],
        out_specs=[pl.BlockSpec((tm, LANES), lambda i: (i, 0)),
                   pl.BlockSpec((tm, LANES), lambda i: (i, 0))],
        out_shape=[jax.ShapeDtypeStruct((t, LANES), I32),
                   jax.ShapeDtypeStruct((t, LANES), F32)],
        compiler_params=_params(("parallel",)),
    )(x2, wr)


def _row_copy(src_hbm, src_row, dst, dst_row, sem):
    return pltpu.make_async_copy(src_hbm.at[pl.ds(src_row, 1)], dst.at[pl.ds(dst_row, 1)], sem)


def _moe_ffn_kernel(be_ref, nu_ref, tok_ref, tok_next_ref, x_hbm, w1_ref, w3_ref, w2_ref, o_ref,
                    xg_ref, xb_ref, acc_ref, sem):
    i = pl.program_id(0)
    f = pl.program_id(1)
    nf = pl.num_programs(1)
    n_used = nu_ref[0]
    used = i < n_used
    tm = xb_ref.shape[0]
    slot = i % 2

    def start_gather(ids_ref, buf):
        def body(r, carry):
            _row_copy(x_hbm, ids_ref[r], xg_ref.at[buf], r, sem.at[buf]).start()
            return carry

        lax.fori_loop(0, tm, body, 0, unroll=8)

    @pl.when(used & (f == 0))
    def _():
        @pl.when(i == 0)
        def _():
            start_gather(tok_ref, 0)

        pltpu.make_async_copy(x_hbm.at[pl.ds(0, tm)], xg_ref.at[slot], sem.at[slot]).wait()

        @pl.when(i + 1 < n_used)
        def _():
            start_gather(tok_next_ref, 1 - slot)

        xb_ref[...] = xg_ref[slot].astype(BF16)
        acc_ref[...] = jnp.zeros_like(acc_ref)

    @pl.when(used)
    def _():
        xb = xb_ref[...]
        gate = jnp.dot(xb, w1_ref[...], preferred_element_type=F32)
        up = jnp.dot(xb, w3_ref[...], preferred_element_type=F32)
        h = (jax.nn.silu(gate) * up).astype(BF16)
        acc_ref[...] += jnp.dot(h, w2_ref[...], preferred_element_type=F32)

    @pl.when(used & (f == nf - 1))
    def _():
        o_ref[...] = acc_ref[...]

    @pl.when(jnp.logical_not(used) & (f == nf - 1))
    def _():
        o_ref[...] = jnp.zeros_like(o_ref)


def _moe_ffn(x2, slot_tok, w1, w3, w2, blk_expert, n_used, tm):
    n_slots = slot_tok.shape[0]
    d = x2.shape[1]
    f_dim = w1.shape[2]
    tf = _ffn_tile(f_dim, largest=1024)
    nf = f_dim // tf
    nt = n_slots // tm

    def f_sel(i, f, nu):
        return jnp.where(i < nu[0], f, nf - 1)

    grid_spec = pltpu.PrefetchScalarGridSpec(
        num_scalar_prefetch=2,
        grid=(nt, nf),
        in_specs=[pl.BlockSpec((tm,), lambda i, f, be, nu: (i,), memory_space=pltpu.SMEM),
                  pl.BlockSpec((tm,), lambda i, f, be, nu: (jnp.minimum(i + 1, nt - 1),), memory_space=pltpu.SMEM),
                  pl.BlockSpec(memory_space=pl.ANY),
                  pl.BlockSpec((None, d, tf), lambda i, f, be, nu: (be[i], 0, f_sel(i, f, nu))),
                  pl.BlockSpec((None, d, tf), lambda i, f, be, nu: (be[i], 0, f_sel(i, f, nu))),
                  pl.BlockSpec((None, tf, d), lambda i, f, be, nu: (be[i], f_sel(i, f, nu), 0))],
        out_specs=pl.BlockSpec((tm, d), lambda i, f, be, nu: (i, 0)),
        scratch_shapes=[pltpu.VMEM((2, tm, d), F32), pltpu.VMEM((tm, d), BF16), pltpu.VMEM((tm, d), F32),
                        pltpu.SemaphoreType.DMA((2,))],
    )
    return pl.pallas_call(
        _moe_ffn_kernel,
        name="moe_ffn",
        grid_spec=grid_spec,
        out_shape=jax.ShapeDtypeStruct((n_slots, d), F32),
        compiler_params=_params(("arbitrary", "arbitrary")),
    )(blk_expert, n_used, slot_tok, slot_tok, x2, w1, w3, w2)


def _moe_combine_kernel(dest_ref, dest_next_ref, y_hbm, x_ref, gate_ref, g_ref, b_ref, o_ref, y_ref, sem, *,
                        alpha, tb):
    i = pl.program_id(0)
    slot = i % 2

    def start_gather(ids_ref, buf):
        def body(tok, carry):
            for k in range(MOE_TOP_K):
                _row_copy(y_hbm, ids_ref[MOE_TOP_K * tok + k], y_ref.at[buf, k], tok, sem.at[buf]).start()
            return carry

        lax.fori_loop(0, tb, body, 0, unroll=4)

    @pl.when(i == 0)
    def _():
        start_gather(dest_ref, 0)

    for k in range(MOE_TOP_K):
        pltpu.make_async_copy(y_hbm.at[pl.ds(0, tb)], y_ref.at[slot, k], sem.at[slot]).wait()

    @pl.when(i + 1 < pl.num_programs(0))
    def _():
        start_gather(dest_next_ref, 1 - slot)

    gates = gate_ref[...]
    ff = gates[:, 0:1] * y_ref[slot, 0] + gates[:, 1:2] * y_ref[slot, 1]
    o_ref[...] = _layer_norm(alpha * x_ref[...] + ff, g_ref[...], b_ref[...])


def _moe_combine(x2, y, dest, gates, g, b, alpha):
    t, d = x2.shape
    tb = min(t, 256)
    nt = t // tb
    return pl.pallas_call(
        functools.partial(_moe_combine_kernel, alpha=alpha, tb=tb),
        name="moe_combine",
        grid=(nt,),
        in_specs=[pl.BlockSpec((tb * MOE_TOP_K,), lambda i: (i,), memory_space=pltpu.SMEM),
                  pl.BlockSpec((tb * MOE_TOP_K,), lambda i: (jnp.minimum(i + 1, nt - 1),), memory_space=pltpu.SMEM),
                  pl.BlockSpec(memory_space=pl.ANY),
                  pl.BlockSpec((tb, d), lambda i: (i, 0)),
                  pl.BlockSpec((tb, LANES), lambda i: (i, 0)),
                  pl.BlockSpec((1, d), lambda i: (0, 0)),
                  pl.BlockSpec((1, d), lambda i: (0, 0))],
        out_specs=pl.BlockSpec((tb, d), lambda i: (i, 0)),
        out_shape=jax.ShapeDtypeStruct((t, d), F32),
        scratch_shapes=[pltpu.VMEM((2, MOE_TOP_K, tb, d), F32), pltpu.SemaphoreType.DMA((2,))],
        compiler_params=_params(("arbitrary",)),
    )(dest, dest, y, x2, gates, g[None, :], b[None, :])


def _moe_layer(x2, w_router, w1, w3, w2, g, b, alpha):
    t, d = x2.shape
    n = t * MOE_TOP_K
    tm = min(512, t)
    idx_pad, gates = _router(x2, w_router)
    e_flat = idx_pad[:, :MOE_TOP_K].reshape(n)
    flat_ids = jnp.arange(n, dtype=I32)
    _, order = lax.sort((e_flat, flat_ids), num_keys=1, is_stable=True)
    _, inv_order = lax.sort((order, flat_ids), num_keys=1)
    experts = jnp.arange(MOE_EXPERTS, dtype=I32)
    counts = jnp.sum((e_flat[:, None] == experts[None, :]).astype(I32), axis=0)
    starts = jnp.cumsum(counts) - counts
    padded = (counts + tm - 1) // tm * tm
    pends = jnp.cumsum(padded)
    pstarts = pends - padded
    n_tiles = (n + MOE_EXPERTS * (tm - 1) + tm - 1) // tm
    n_used = (pends[-1] // tm).astype(I32)
    blk_start = jnp.arange(n_tiles, dtype=I32) * tm
    blk_expert = jnp.minimum(jnp.sum((pends[None, :] <= blk_start[:, None]).astype(I32), axis=1), MOE_EXPERTS - 1)
    blk_expert = jnp.where(jnp.arange(n_tiles) < n_used, blk_expert, blk_expert[jnp.maximum(n_used - 1, 0)])
    n_slots = n_tiles * tm
    slot = jnp.arange(n_slots, dtype=I32)
    slot_e = jnp.repeat(blk_expert, tm)
    shift = pstarts - starts
    order_tok = jnp.pad(order // MOE_TOP_K, (0, n_slots - n))
    slot_tok = jnp.zeros((n_slots,), I32)
    for e in range(MOE_EXPERTS):
        in_group = (slot_e == e) & (slot - pstarts[e] < counts[e]) & (slot < n_used * tm)
        slot_tok = jnp.where(in_group, jnp.roll(order_tok, shift[e]), slot_tok)
    dest = (inv_order + jnp.sum(jnp.where(e_flat[:, None] == experts[None, :], shift[None, :], 0), axis=1)).astype(I32)

    y = _moe_ffn(x2, slot_tok, w1.astype(BF16), w3.astype(BF16), w2.astype(BF16), blk_expert.astype(I32),
                 n_used.reshape(1), tm)
    return _moe_combine(x2, y, dest, gates, g, b, alpha)


def kernel(x, p, positions, swa_w_qkv, swa_sinks, swa_w_o, dense_w1, dense_w3, dense_w2, dsa_w_in, dsa_kv_norm,
           dsa_w_uk, dsa_w_uv, dsa_w_o, moe_router, moe_w1, moe_w3, moe_w2, ln_g, ln_b, ple_w_p, ple_w_g):
    batch, seq, d = x.shape
    depth = ln_g.shape[0]
    alpha = (2.0 * depth) ** 0.25
    t = batch * seq
    x2 = x.reshape(t, d)
    cos_t, sin_t = _rope_tables(positions)
    for i in range(depth):
        j = i // 2
        if i % 2 == 0:
            q, k2, v_t = _swa_qkv(x2, swa_w_qkv[j], cos_t, sin_t)
            o = _swa_attn(q, k2, v_t, swa_sinks[j], batch, seq)
            x2 = _proj_ln(o, swa_w_o[j], x2, ln_g[i, 0], ln_b[i, 0], alpha)
            x2 = _ffn_dense(x2, dense_w1[j], dense_w3[j], dense_w2[j], ln_g[i, 1], ln_b[i, 1], alpha)
        else:
            o = _dsa_mixer_out(x2, cos_t, sin_t, dsa_w_in[j], dsa_kv_norm[j], dsa_w_uk[j], dsa_w_uv[j], batch, seq)
            x2 = _proj_ln(o, dsa_w_o[j], x2, ln_g[i, 0], ln_b[i, 0], alpha)
            x2 = _moe_layer(x2, moe_router[j], moe_w1[j], moe_w3[j], moe_w2[j], ln_g[i, 1], ln_b[i, 1], alpha)
        x2 = _ple(x2, p[i].reshape(t, -1), ple_w_g[i], ple_w_p[i])
    return x2.reshape(batch, seq, d)
```

```python
import functools

import jax
import jax.numpy as jnp
from jax import lax
from jax.experimental import pallas as pl
from jax.experimental.pallas import tpu as pltpu

F32 = jnp.float32
BF16 = jnp.bfloat16
I32 = jnp.int32

LANES = 128
ROPE_THETA = 10000.0
LN_EPS = 1e-5
RMS_EPS = 1e-6
SWA_HEADS = 32
SWA_KV_HEADS = 4
SWA_GROUP = SWA_HEADS // SWA_KV_HEADS
SWA_HEAD_DIM = 64
SWA_WINDOW = 128
SWA_V_PAD = SWA_HEAD_DIM + 16
MLA_HEADS = 16
MLA_NOPE = 128
MLA_ROPE = 64
MLA_V = 128
MLA_LATENT = 512
MLA_SCALE = (MLA_NOPE + MLA_ROPE) ** -0.5
MLA_QK_PAD = 256
MLA_V_PAD = MLA_V + 16
LOG2_E = 1.4426950408889634
IDX_HEADS = 16
IDX_DIM = 64
IDX_ROPE = 32
DSA_TOPK = 256
MOE_EXPERTS = 8
MOE_TOP_K = 2
NEG_BIG = -1e30
M_INIT = -1e20
VMEM_LIMIT = 56 * 1024 * 1024


def _params(sem, vmem=VMEM_LIMIT, flags=None):
    return pltpu.CompilerParams(dimension_semantics=sem, vmem_limit_bytes=vmem, flags=flags)


def _layer_norm(y, g, b):
    mu = jnp.mean(y, axis=-1, keepdims=True)
    d = y - mu
    var = jnp.mean(d * d, axis=-1, keepdims=True)
    return d * lax.rsqrt(var + LN_EPS) * g + b


def _rope_lanes(a, cs, sn, half):
    lane = lax.broadcasted_iota(I32, a.shape, 1)
    first = (lane % 64) < half
    rot = jnp.where(first, pltpu.roll(a, LANES - half, 1), pltpu.roll(a, half, 1))
    return a * cs + rot * sn


def _rope_table_kernel(pos_ref, invf_ref, sgn_ref, cos_ref, sin_ref):
    ang = pos_ref[...].astype(F32) * invf_ref[...]
    cos_ref[...] = jnp.cos(ang)
    sin_ref[...] = jnp.sin(ang) * sgn_ref[...]


def _rope_tables(positions):
    t = positions.size
    pos = positions.reshape(t, 1).astype(I32)
    lane = jnp.arange(LANES)
    in64 = lane % 64
    f64 = ROPE_THETA ** (-jnp.arange(0, 64, 2, dtype=F32) / 64)
    f32_ = ROPE_THETA ** (-jnp.arange(0, IDX_ROPE, 2, dtype=F32) / IDX_ROPE)
    invf_a = f64[in64 % 32]
    sgn_a = jnp.where(in64 < 32, -1.0, 1.0).astype(F32)
    invf_b = jnp.where(in64 < IDX_ROPE, f32_[in64 % (IDX_ROPE // 2)], 0.0)
    sgn_b = jnp.where(in64 < IDX_ROPE // 2, -1.0, 1.0).astype(F32)
    invf = jnp.concatenate([invf_a, invf_b])[None, :]
    sgn = jnp.concatenate([sgn_a, sgn_b])[None, :]
    tm = min(t, 1024)
    return pl.pallas_call(
        _rope_table_kernel,
        name="rope_tables",
        grid=(t // tm,),
        in_specs=[pl.BlockSpec((tm, 1), lambda i: (i, 0)),
                  pl.BlockSpec((1, 2 * LANES), lambda i: (0, 0)),
                  pl.BlockSpec((1, 2 * LANES), lambda i: (0, 0))],
        out_specs=[pl.BlockSpec((tm, 2 * LANES), lambda i: (i, 0)),
                   pl.BlockSpec((tm, 2 * LANES), lambda i: (i, 0))],
        out_shape=[jax.ShapeDtypeStruct((t, 2 * LANES), F32)] * 2,
        compiler_params=_params(("parallel",)),
    )(pos, invf, sgn)


def _swa_qkv_kernel(x_ref, w_ref, wvt_ref, cos_ref, sin_ref, q_ref, k_ref, vt_ref, *, nq, nk):
    xb = x_ref[...].astype(BF16)
    acc = jnp.dot(xb, w_ref[...], preferred_element_type=F32)
    cs = cos_ref[:, :LANES]
    sn = sin_ref[:, :LANES]
    scale = SWA_HEAD_DIM ** -0.5 * LOG2_E
    for c in range(nq // LANES):
        a = acc[:, c * LANES:(c + 1) * LANES]
        q_ref[:, c * LANES:(c + 1) * LANES] = (_rope_lanes(a, cs, sn, 32) * scale).astype(BF16)
    low = lax.broadcasted_iota(I32, (acc.shape[0], LANES), 1) < SWA_HEAD_DIM
    for c in range(nk // LANES):
        kr = _rope_lanes(acc[:, nq + c * LANES:nq + (c + 1) * LANES], cs, sn, 32)
        swapped = pltpu.roll(kr, SWA_HEAD_DIM, 1)
        placements = (jnp.where(low, kr, 0.0), jnp.where(low, 0.0, swapped),
                      jnp.where(low, swapped, 0.0), jnp.where(low, 0.0, kr))
        for n, slab in enumerate(placements):
            k_ref[:, (4 * c + n) * LANES:(4 * c + n + 1) * LANES] = slab.astype(BF16)
    vt = lax.dot_general(wvt_ref[...], xb, (((1,), (1,)), ((), ())), preferred_element_type=F32)
    ones = jnp.ones((SWA_V_PAD - SWA_HEAD_DIM, xb.shape[0]), BF16)
    for kh in range(SWA_KV_HEADS):
        vt_ref[kh * SWA_V_PAD:kh * SWA_V_PAD + SWA_HEAD_DIM, :] = (
            vt[kh * SWA_HEAD_DIM:(kh + 1) * SWA_HEAD_DIM, :].astype(BF16))
        vt_ref[kh * SWA_V_PAD + SWA_HEAD_DIM:(kh + 1) * SWA_V_PAD, :] = ones


def _swa_qkv(x2, w_qkv, cos_t, sin_t):
    t, d = x2.shape
    nq = SWA_HEADS * SWA_HEAD_DIM
    nk = SWA_KV_HEADS * SWA_HEAD_DIM
    nk2 = SWA_KV_HEADS * 2 * LANES
    nvt = SWA_KV_HEADS * SWA_V_PAD
    tm = min(t, 512)
    wb = w_qkv.astype(BF16)
    return pl.pallas_call(
        functools.partial(_swa_qkv_kernel, nq=nq, nk=nk),
        name="swa_qkv",
        grid=(t // tm,),
        in_specs=[pl.BlockSpec((tm, d), lambda i: (i, 0)),
                  pl.BlockSpec((d, nq + nk), lambda i: (0, 0)),
                  pl.BlockSpec((nk, d), lambda i: (0, 0)),
                  pl.BlockSpec((tm, 2 * LANES), lambda i: (i, 0)),
                  pl.BlockSpec((tm, 2 * LANES), lambda i: (i, 0))],
        out_specs=[pl.BlockSpec((tm, nq), lambda i: (i, 0)),
                   pl.BlockSpec((tm, nk2), lambda i: (i, 0)),
                   pl.BlockSpec((nvt, tm), lambda i: (0, i))],
        out_shape=[jax.ShapeDtypeStruct((t, nq), BF16),
                   jax.ShapeDtypeStruct((t, nk2), BF16),
                   jax.ShapeDtypeStruct((nvt, t), BF16)],
        compiler_params=_params(("parallel",)),
    )(x2, wb[:, :nq + nk], jnp.transpose(wb[:, nq + nk:]), cos_t, sin_t)


def _swa_attn_kernel(sink_ref, q_ref, kp_ref, kc_ref, vp_ref, vc_ref, o_ref):
    blk = pl.program_id(1)
    bq = q_ref.shape[0]
    dh = SWA_HEAD_DIM
    slabs = SWA_GROUP // 2
    cols = slabs * bq
    kj = lax.broadcasted_iota(I32, (2 * bq, cols), 0)
    qi = lax.broadcasted_iota(I32, (2 * bq, cols), 1) % bq
    rel = qi + bq - kj
    allowed = (rel >= 0) & (rel < SWA_WINDOW) & ((kj >= bq) | (blk > 0))
    for kh in range(SWA_KV_HEADS):
        base = kh * SWA_GROUP * dh
        qs = jnp.concatenate([q_ref[:, base + s * LANES:base + (s + 1) * LANES] for s in range(slabs)], axis=0)
        vcat = jnp.concatenate([vp_ref[kh * SWA_V_PAD:(kh + 1) * SWA_V_PAD, :],
                                vc_ref[kh * SWA_V_PAD:(kh + 1) * SWA_V_PAD, :]], axis=1)
        for half in range(2):
            ks = (2 * kh + half) * LANES
            kcat = jnp.concatenate([kp_ref[:, ks:ks + LANES], kc_ref[:, ks:ks + LANES]], axis=0)
            heads = [kh * SWA_GROUP + 2 * s + half for s in range(slabs)]
            sink = jnp.concatenate([jnp.full((1, bq), sink_ref[h] * LOG2_E, F32) for h in heads], axis=1)
            logits = lax.dot_general(kcat, qs, (((1,), (1,)), ((), ())), preferred_element_type=F32)
            logits = jnp.where(allowed, logits, NEG_BIG)
            m = jnp.maximum(jnp.max(logits, axis=0, keepdims=True), sink)
            p = jnp.exp2(logits - m).astype(BF16)
            pv = jnp.dot(vcat, p, preferred_element_type=F32)
            inv = 1.0 / (pv[dh:dh + 1, :] + jnp.exp2(sink - m))
            o = (pv[:dh, :] * inv).astype(BF16)
            for s, h in enumerate(heads):
                o_ref[h * dh:(h + 1) * dh, :] = o[:, s * bq:(s + 1) * bq]


def _swa_attn(q, k2, v_t, sinks, batch, seq):
    t, nq = q.shape
    nk2 = k2.shape[1]
    nvt = v_t.shape[0]
    bq = SWA_WINDOW
    nb = seq // bq

    def cur(b, i):
        return (b * nb + i, 0)

    def prev(b, i):
        return (b * nb + jnp.maximum(i - 1, 0), 0)

    def cur_t(b, i):
        return (0, b * nb + i)

    def prev_t(b, i):
        return (0, b * nb + jnp.maximum(i - 1, 0))

    return pl.pallas_call(
        _swa_attn_kernel,
        name="swa_attn",
        grid=(batch, nb),
        in_specs=[pl.BlockSpec(memory_space=pltpu.SMEM),
                  pl.BlockSpec((bq, nq), cur),
                  pl.BlockSpec((bq, nk2), prev),
                  pl.BlockSpec((bq, nk2), cur),
                  pl.BlockSpec((nvt, bq), prev_t),
                  pl.BlockSpec((nvt, bq), cur_t)],
        out_specs=pl.BlockSpec((nq, bq), cur_t),
        out_shape=jax.ShapeDtypeStruct((nq, t), BF16),
        compiler_params=_params(("parallel", "parallel")),
    )(sinks.astype(F32), q, k2, k2, v_t, v_t)


def _proj_ln_kernel(at_ref, w_ref, res_ref, g_ref, b_ref, o_ref, *, alpha):
    mix = lax.dot_general(at_ref[...], w_ref[...], (((0,), (0,)), ((), ())), preferred_element_type=F32)
    o_ref[...] = _layer_norm(alpha * res_ref[...] + mix, g_ref[...], b_ref[...])


def _proj_ln(a_t, w, res, g, b, alpha):
    k, t = a_t.shape
    d = w.shape[1]
    tm = min(t, 512)
    return pl.pallas_call(
        functools.partial(_proj_ln_kernel, alpha=alpha),
        name="proj_ln",
        grid=(t // tm,),
        in_specs=[pl.BlockSpec((k, tm), lambda i: (0, i)),
                  pl.BlockSpec((k, d), lambda i: (0, 0)),
                  pl.BlockSpec((tm, d), lambda i: (i, 0)),
                  pl.BlockSpec((1, d), lambda i: (0, 0)),
                  pl.BlockSpec((1, d), lambda i: (0, 0))],
        out_specs=pl.BlockSpec((tm, d), lambda i: (i, 0)),
        out_shape=jax.ShapeDtypeStruct((t, d), F32),
        compiler_params=_params(("parallel",)),
    )(a_t, w.astype(BF16), res, g[None, :], b[None, :])


def _ffn_dense_kernel(x_ref, w1_ref, w3_ref, w2_ref, g_ref, b_ref, o_ref, xb_ref, acc_ref, *, alpha):
    f = pl.program_id(1)

    @pl.when(f == 0)
    def _():
        xb_ref[...] = x_ref[...].astype(BF16)
        acc_ref[...] = jnp.zeros_like(acc_ref)

    xb = xb_ref[...]
    gate = jnp.dot(xb, w1_ref[...], preferred_element_type=F32)
    up = jnp.dot(xb, w3_ref[...], preferred_element_type=F32)
    h = (jax.nn.silu(gate) * up).astype(BF16)
    acc_ref[...] += jnp.dot(h, w2_ref[...], preferred_element_type=F32)

    @pl.when(f == pl.num_programs(1) - 1)
    def _():
        o_ref[...] = _layer_norm(alpha * x_ref[...] + acc_ref[...], g_ref[...], b_ref[...])


def _ffn_tile(f_dim, largest=512):
    for tf in (largest, 512, 256, 128):
        if f_dim % tf == 0:
            return tf
    return f_dim


def _ffn_dense(x2, w1, w3, w2, g, b, alpha):
    t, d = x2.shape
    f_dim = w1.shape[1]
    tm = min(t, 512)
    tf = _ffn_tile(f_dim)
    return pl.pallas_call(
        functools.partial(_ffn_dense_kernel, alpha=alpha),
        name="ffn_dense",
        grid=(t // tm, f_dim // tf),
        in_specs=[pl.BlockSpec((tm, d), lambda i, f: (i, 0)),
                  pl.BlockSpec((d, tf), lambda i, f: (0, f)),
                  pl.BlockSpec((d, tf), lambda i, f: (0, f)),
                  pl.BlockSpec((tf, d), lambda i, f: (f, 0)),
                  pl.BlockSpec((1, d), lambda i, f: (0, 0)),
                  pl.BlockSpec((1, d), lambda i, f: (0, 0))],
        out_specs=pl.BlockSpec((tm, d), lambda i, f: (i, 0)),
        out_shape=jax.ShapeDtypeStruct((t, d), F32),
        scratch_shapes=[pltpu.VMEM((tm, d), BF16), pltpu.VMEM((tm, d), F32)],
        compiler_params=_params(("parallel", "arbitrary")),
    )(x2, w1.astype(BF16), w3.astype(BF16), w2.astype(BF16), g[None, :], b[None, :])


def _ple_kernel(x_ref, p_ref, wg_ref, wp_ref, o_ref):
    x = x_ref[...]
    gate = jax.nn.sigmoid(jnp.dot(x.astype(BF16), wg_ref[...], preferred_element_type=F32))
    emb = jnp.dot(p_ref[...].astype(BF16), wp_ref[...], preferred_element_type=F32)
    o_ref[...] = x + gate * emb


def _ple(x2, p2, w_g, w_p):
    t, d = x2.shape
    pd = p2.shape[1]
    tm = min(t, 512)
    return pl.pallas_call(
        _ple_kernel,
        name="ple",
        grid=(t // tm,),
        in_specs=[pl.BlockSpec((tm, d), lambda i: (i, 0)),
                  pl.BlockSpec((tm, pd), lambda i: (i, 0)),
                  pl.BlockSpec((d, d), lambda i: (0, 0)),
                  pl.BlockSpec((pd, d), lambda i: (0, 0))],
        out_specs=pl.BlockSpec((tm, d), lambda i: (i, 0)),
        out_shape=jax.ShapeDtypeStruct((t, d), F32),
        compiler_params=_params(("parallel",)),
    )(x2, p2, w_g.astype(BF16), w_p.astype(BF16))


def _dsa_q_kernel(x_ref, w_ref, cos_ref, sin_ref, q_ref):
    xb = x_ref[...].astype(BF16)
    acc = jnp.dot(xb, w_ref[...], preferred_element_type=F32)
    cs = cos_ref[:, :LANES]
    sn = sin_ref[:, :LANES]
    scale = MLA_SCALE * LOG2_E
    for h in range(MLA_HEADS):
        base = h * MLA_QK_PAD
        q_ref[:, base:base + LANES] = (acc[:, base:base + LANES] * scale).astype(BF16)
        a = acc[:, base + LANES:base + 2 * LANES]
        q_ref[:, base + LANES:base + 2 * LANES] = (_rope_lanes(a, cs, sn, 32) * scale).astype(BF16)


def _dsa_q(x2, w_q, cos_t, sin_t):
    t, d = x2.shape
    n = w_q.shape[1]
    tm = min(t, 512)
    return pl.pallas_call(
        _dsa_q_kernel,
        name="dsa_q",
        grid=(t // tm,),
        in_specs=[pl.BlockSpec((tm, d), lambda i: (i, 0)),
                  pl.BlockSpec((d, n), lambda i: (0, 0)),
                  pl.BlockSpec((tm, 2 * LANES), lambda i: (i, 0)),
                  pl.BlockSpec((tm, 2 * LANES), lambda i: (i, 0))],
        out_specs=pl.BlockSpec((tm, n), lambda i: (i, 0)),
        out_shape=jax.ShapeDtypeStruct((t, n), BF16),
        compiler_params=_params(("parallel",)),
    )(x2, w_q, cos_t, sin_t)


def _dsa_small_kernel(x_ref, w_ref, cos_ref, sin_ref, kvn_ref, ckv_ref, kr_ref, ki_ref, wi_ref):
    xb = x_ref[...].astype(BF16)
    acc = jnp.dot(xb, w_ref[...], preferred_element_type=F32)
    c = acc[:, :MLA_LATENT]
    ms = jnp.mean(c * c, axis=-1, keepdims=True)
    ckv_ref[...] = (c * lax.rsqrt(ms + RMS_EPS) * kvn_ref[...]).astype(BF16)
    o = MLA_LATENT
    kr_ref[...] = _rope_lanes(acc[:, o:o + LANES], cos_ref[:, :LANES], sin_ref[:, :LANES], 32).astype(BF16)
    ki_ref[...] = _rope_lanes(acc[:, o + LANES:o + 2 * LANES], cos_ref[:, LANES:], sin_ref[:, LANES:],
                              IDX_ROPE // 2).astype(BF16)
    wi_ref[...] = acc[:, o + 2 * LANES:o + 3 * LANES] * (IDX_HEADS ** -0.5)


def _dsa_small(x2, w_small, kv_norm, cos_t, sin_t):
    t, d = x2.shape
    n = w_small.shape[1]
    tm = min(t, 512)
    return pl.pallas_call(
        _dsa_small_kernel,
        name="dsa_small",
        grid=(t // tm,),
        in_specs=[pl.BlockSpec((tm, d), lambda i: (i, 0)),
                  pl.BlockSpec((d, n), lambda i: (0, 0)),
                  pl.BlockSpec((tm, 2 * LANES), lambda i: (i, 0)),
                  pl.BlockSpec((tm, 2 * LANES), lambda i: (i, 0)),
                  pl.BlockSpec((1, MLA_LATENT), lambda i: (0, 0))],
        out_specs=[pl.BlockSpec((tm, MLA_LATENT), lambda i: (i, 0)),
                   pl.BlockSpec((tm, LANES), lambda i: (i, 0)),
                   pl.BlockSpec((tm, LANES), lambda i: (i, 0)),
                   pl.BlockSpec((tm, LANES), lambda i: (i, 0))],
        out_shape=[jax.ShapeDtypeStruct((t, MLA_LATENT), BF16),
                   jax.ShapeDtypeStruct((t, LANES), BF16),
                   jax.ShapeDtypeStruct((t, LANES), BF16),
                   jax.ShapeDtypeStruct((t, LANES), F32)],
        compiler_params=_params(("parallel",)),
    )(x2, w_small, cos_t, sin_t, kv_norm[None, :].astype(F32))


def _dsa_qi_kernel(x_ref, w_ref, cos_ref, sin_ref, q_ref):
    xb = x_ref[...].astype(BF16)
    acc = jnp.dot(xb, w_ref[...], preferred_element_type=F32)
    cs = cos_ref[:, LANES:]
    sn = sin_ref[:, LANES:]
    scale = IDX_DIM ** -0.5
    for c in range(acc.shape[1] // LANES):
        a = acc[:, c * LANES:(c + 1) * LANES]
        q_ref[:, c * LANES:(c + 1) * LANES] = (_rope_lanes(a, cs, sn, IDX_ROPE // 2) * scale).astype(BF16)


def _dsa_qi(x2, w_qi, cos_t, sin_t):
    t, d = x2.shape
    n = w_qi.shape[1]
    tm = min(t, 512)
    return pl.pallas_call(
        _dsa_qi_kernel,
        name="dsa_qi",
        grid=(t // tm,),
        in_specs=[pl.BlockSpec((tm, d), lambda i: (i, 0)),
                  pl.BlockSpec((d, n), lambda i: (0, 0)),
                  pl.BlockSpec((tm, 2 * LANES), lambda i: (i, 0)),
                  pl.BlockSpec((tm, 2 * LANES), lambda i: (i, 0))],
        out_specs=pl.BlockSpec((tm, n), lambda i: (i, 0)),
        out_shape=jax.ShapeDtypeStruct((t, n), BF16),
        compiler_params=_params(("parallel",)),
    )(x2, w_qi, cos_t, sin_t)


def _dsa_kv_kernel(c_ref, kr_ref, wk_ref, wvt_ref, k_ref, vt_ref):
    c = c_ref[...]
    kn = jnp.dot(c, wk_ref[...], preferred_element_type=F32)
    kr = kr_ref[...]
    for h in range(MLA_HEADS):
        base = h * MLA_QK_PAD
        k_ref[:, base:base + LANES] = kn[:, h * LANES:(h + 1) * LANES].astype(BF16)
        k_ref[:, base + LANES:base + 2 * LANES] = kr
    vt = lax.dot_general(wvt_ref[...], c, (((1,), (1,)), ((), ())), preferred_element_type=F32)
    ones = jnp.ones((MLA_V_PAD - MLA_V, c.shape[0]), BF16)
    for h in range(MLA_HEADS):
        vt_ref[h * MLA_V_PAD:h * MLA_V_PAD + MLA_V, :] = vt[h * MLA_V:(h + 1) * MLA_V, :].astype(BF16)
        vt_ref[h * MLA_V_PAD + MLA_V:(h + 1) * MLA_V_PAD, :] = ones


def _dsa_kv(ckv, kr, w_uk_all, w_uv_t):
    t, c_dim = ckv.shape
    nk = MLA_HEADS * MLA_QK_PAD
    nv = w_uv_t.shape[0]
    nv_pad = MLA_HEADS * MLA_V_PAD
    tm = min(t, 512)
    return pl.pallas_call(
        _dsa_kv_kernel,
        name="dsa_kv",
        grid=(t // tm,),
        in_specs=[pl.BlockSpec((tm, c_dim), lambda i: (i, 0)),
                  pl.BlockSpec((tm, LANES), lambda i: (i, 0)),
                  pl.BlockSpec((c_dim, MLA_HEADS * MLA_NOPE), lambda i: (0, 0)),
                  pl.BlockSpec((nv, c_dim), lambda i: (0, 0))],
        out_specs=[pl.BlockSpec((tm, nk), lambda i: (i, 0)),
                   pl.BlockSpec((nv_pad, tm), lambda i: (0, i))],
        out_shape=[jax.ShapeDtypeStruct((t, nk), BF16),
                   jax.ShapeDtypeStruct((nv_pad, t), BF16)],
        compiler_params=_params(("parallel",)),
    )(ckv, kr, w_uk_all, w_uv_t)


def _float_sort_key(s):
    bits = lax.bitcast_convert_type(s, I32)
    return bits ^ ((bits >> 31) & jnp.int32(0x7FFFFFFF))


def _dsa_select_kernel(k_ref, q_ref, w_ref, bias_ref, key_ref, *, sc, cc, unroll, n_sel):
    j = pl.program_id(1)
    s_len, tq = key_ref.shape
    n_chunks = s_len // sc
    q_lo = j * tq
    n_need = (q_lo + tq + sc * unroll - 1) // (sc * unroll) * unroll
    qt = q_ref[...]
    w_row = w_ref[...]
    q_pos = q_lo + lax.broadcasted_iota(I32, (sc, tq), 1)
    k_off = lax.broadcasted_iota(I32, (sc, tq), 0)
    min_key = jnp.int32(-2 ** 31)

    def score_chunks(i, carry):
        for u in range(unroll):
            row0 = pl.multiple_of((i * unroll + u) * sc, sc)
            kc = k_ref[pl.ds(row0, sc), :]
            d = jnp.dot(kc, qt, preferred_element_type=F32)
            s = jnp.zeros((sc, tq), F32)
            for h in range(IDX_HEADS):
                s = s + jnp.maximum(d[:, h * tq:(h + 1) * tq], 0.0) * w_row[:, h * tq:(h + 1) * tq]
            causal = (row0 + k_off) <= q_pos
            key_ref[pl.ds(row0, sc), :] = jnp.where(causal, _float_sort_key(s), min_key)
        return carry

    lax.fori_loop(0, n_need // unroll, score_chunks, 0)

    n_cnt = (n_need * sc + cc - 1) // cc

    def fill_chunk(c, carry):
        row0 = pl.multiple_of(c * sc, sc)
        key_ref[pl.ds(row0, sc), :] = jnp.full((sc, tq), min_key, I32)
        return carry

    lax.fori_loop(n_need, n_cnt * (cc // sc), fill_chunk, 0)

    def count_ge(cand):
        def body(c, cnt):
            row0 = pl.multiple_of(c * cc, cc)
            hit = jnp.where(key_ref[pl.ds(row0, cc), :] >= cand, 1, 0).astype(I32)
            return cnt + jnp.sum(hit.reshape(cc // 32, 32, tq), axis=0)

        cnt = lax.fori_loop(0, n_cnt, body, jnp.zeros((32, tq), I32))
        return jnp.sum(cnt, axis=0, keepdims=True)

    zero = jnp.zeros((1, tq), I32)
    thr = jnp.where(count_ge(zero) >= n_sel, zero, jnp.full((1, tq), min_key, I32))

    def bit_step(it, thr):
        cand = thr + (jnp.int32(1) << (30 - it))
        return jnp.where(count_ge(cand) >= n_sel, cand, thr)

    thr = lax.fori_loop(0, 31, bit_step, thr)

    def write_chunk(c, carry):
        row0 = pl.multiple_of(c * sc, sc)
        keep = (key_ref[pl.ds(row0, sc), :] >= thr) & ((row0 + k_off) <= q_pos)
        bias_ref[pl.ds(row0, sc), :] = jnp.where(keep, 0.0, NEG_BIG).astype(F32)
        return carry

    def mask_chunk(c, carry):
        row0 = pl.multiple_of(c * sc, sc)
        bias_ref[pl.ds(row0, sc), :] = jnp.full((sc, tq), NEG_BIG, F32)
        return carry

    lax.fori_loop(0, n_need, write_chunk, 0)
    lax.fori_loop(n_need, n_chunks, mask_chunk, 0)


def _dsa_select(k_i, q_i_t, w_rows, batch, seq, n_sel):
    tq = LANES
    nq = seq // tq
    sc = min(seq, 256)
    cc = min(seq, 1024)
    return pl.pallas_call(
        functools.partial(_dsa_select_kernel, sc=sc, cc=cc, unroll=2 if (seq // sc) % 2 == 0 else 1, n_sel=n_sel),
        name="dsa_select",
        grid=(batch, nq),
        in_specs=[pl.BlockSpec((None, seq, IDX_DIM), lambda b, j: (b, 0, 0)),
                  pl.BlockSpec((None, None, IDX_DIM, IDX_HEADS * tq), lambda b, j: (b, j, 0, 0)),
                  pl.BlockSpec((None, None, 1, IDX_HEADS * tq), lambda b, j: (b, j, 0, 0))],
        out_specs=pl.BlockSpec((None, seq, tq), lambda b, j: (b, 0, j)),
        out_shape=jax.ShapeDtypeStruct((batch, seq, seq), F32),
        scratch_shapes=[pltpu.VMEM((seq, tq), I32)],
        compiler_params=_params(("parallel", "parallel")),
    )(k_i, q_i_t, w_rows)


def _dsa_attn_kernel(qt_ref, k_ref, vt_ref, bias_ref, o_ref, m_ref, a_ref, *head_refs, sc):
    s_refs = head_refs[:MLA_HEADS]
    acc_refs = head_refs[MLA_HEADS:]
    j = pl.program_id(1)
    c = pl.program_id(2)
    tq = qt_ref.shape[1]
    last = (j * tq + tq - 1) // sc
    cur = c % 2
    prev = 1 - cur

    @pl.when(c == 0)
    def _():
        for h in range(MLA_HEADS):
            acc_refs[h][...] = jnp.zeros_like(acc_refs[h])
            s_refs[h][1] = jnp.full((sc, tq), NEG_BIG, F32)
        m_ref[...] = jnp.full_like(m_ref, M_INIT)
        a_ref[...] = jnp.ones_like(a_ref)

    def finish_prev():
        m_prev = m_ref[...]
        a_prev = a_ref[...]
        for h in range(MLA_HEADS):
            p = jnp.exp2(s_refs[h][prev] - m_prev[h:h + 1, :]).astype(BF16)
            pv = jnp.dot(vt_ref[h * MLA_V_PAD:(h + 1) * MLA_V_PAD, :], p, preferred_element_type=F32)
            acc_refs[h][...] = a_prev[h:h + 1, :] * acc_refs[h][...] + pv
        return m_prev

    @pl.when(c <= last)
    def _():
        m_prev = finish_prev()
        chunk_max = []
        for h in range(MLA_HEADS):
            qh = qt_ref[h * MLA_QK_PAD:(h + 1) * MLA_QK_PAD, :]
            kh = k_ref[:, h * MLA_QK_PAD:(h + 1) * MLA_QK_PAD]
            s = jnp.dot(kh, qh, preferred_element_type=F32) + bias_ref[...]
            s_refs[h][cur] = s
            chunk_max.append(jnp.max(s, axis=0, keepdims=True))
        m_new = jnp.maximum(m_prev, jnp.concatenate(chunk_max, axis=0))
        a_ref[...] = jnp.exp2(m_prev - m_new)
        m_ref[...] = m_new

    @pl.when(c == last + 1)
    def _():
        finish_prev()
        for h in range(MLA_HEADS):
            inv = 1.0 / acc_refs[h][MLA_V:MLA_V + 1, :]
            o_ref[h * MLA_V:(h + 1) * MLA_V, :] = (acc_refs[h][:MLA_V, :] * inv).astype(BF16)


def _dsa_attn(q_cat_t, k_cat, v_t, bias, batch, seq):
    nqk, t = q_cat_t.shape
    nv_pad = v_t.shape[0]
    nv = MLA_HEADS * MLA_V
    tq = min(seq, 512)
    sc = min(seq, 256)
    nq = seq // tq
    nc = seq // sc

    def last_chunk(j):
        return (j * tq + tq - 1) // sc

    return pl.pallas_call(
        functools.partial(_dsa_attn_kernel, sc=sc),
        name="dsa_attn",
        grid=(batch, nq, nc + 1),
        in_specs=[pl.BlockSpec((nqk, tq), lambda b, j, c: (0, b * nq + j)),
                  pl.BlockSpec((sc, nqk), lambda b, j, c: (b * nc + jnp.minimum(c, last_chunk(j)), 0)),
                  pl.BlockSpec((nv_pad, sc), lambda b, j, c: (0, b * nc + jnp.clip(c - 1, 0, last_chunk(j)))),
                  pl.BlockSpec((None, sc, tq), lambda b, j, c: (b, jnp.minimum(c, last_chunk(j)), j))],
        out_specs=pl.BlockSpec((nv, tq), lambda b, j, c: (0, b * nq + j)),
        out_shape=jax.ShapeDtypeStruct((nv, t), BF16),
        scratch_shapes=([pltpu.VMEM((MLA_HEADS, tq), F32), pltpu.VMEM((MLA_HEADS, tq), F32)]
                        + [pltpu.VMEM((2, sc, tq), F32)] * MLA_HEADS
                        + [pltpu.VMEM((MLA_V_PAD, tq), F32)] * MLA_HEADS),
        compiler_params=_params(("parallel", "parallel", "arbitrary")),
    )(q_cat_t, k_cat, v_t, bias)


def _dsa_mixer_out(x2, cos_t, sin_t, w_in, kv_norm, w_uk, w_uv, batch, seq):
    t, d = x2.shape
    nq = MLA_HEADS * (MLA_NOPE + MLA_ROPE)
    o_c = nq
    o_kr = o_c + MLA_LATENT
    o_qi = o_kr + MLA_ROPE
    o_ki = o_qi + IDX_HEADS * IDX_DIM
    o_wi = o_ki + IDX_DIM
    wb = w_in.astype(BF16)
    wq = wb[:, :nq].reshape(d, MLA_HEADS, MLA_NOPE + MLA_ROPE)
    wq = jnp.pad(wq, ((0, 0), (0, 0), (0, MLA_QK_PAD - MLA_NOPE - MLA_ROPE))).reshape(d, MLA_HEADS * MLA_QK_PAD)
    pad_to_lanes = lambda w: jnp.pad(w, ((0, 0), (0, LANES - w.shape[1])))
    w_small = jnp.concatenate([wb[:, o_c:o_kr], pad_to_lanes(wb[:, o_kr:o_qi]), pad_to_lanes(wb[:, o_ki:o_wi]),
                               pad_to_lanes(wb[:, o_wi:])], axis=1)
    w_qi = wb[:, o_qi:o_ki]
    w_uk_all = jnp.transpose(w_uk, (1, 0, 2)).reshape(MLA_LATENT, MLA_HEADS * MLA_NOPE).astype(BF16)
    w_uv_t = jnp.transpose(w_uv, (0, 2, 1)).reshape(MLA_HEADS * MLA_V, MLA_LATENT).astype(BF16)

    q_cat = _dsa_q(x2, wq, cos_t, sin_t)
    ckv, kr, ki, wi = _dsa_small(x2, w_small, kv_norm, cos_t, sin_t)
    qi = _dsa_qi(x2, w_qi, cos_t, sin_t)
    k_cat, v_t = _dsa_kv(ckv, kr, w_uk_all, w_uv_t)

    tq = LANES
    nqb = seq // tq
    k_i = ki[:, :IDX_DIM].reshape(batch, seq, IDX_DIM)
    q_i_t = jnp.transpose(qi.reshape(batch, nqb, tq, IDX_HEADS, IDX_DIM), (0, 1, 4, 3, 2))
    q_i_t = q_i_t.reshape(batch, nqb, IDX_DIM, IDX_HEADS * tq)
    w_rows = jnp.transpose(wi[:, :IDX_HEADS].reshape(batch, nqb, tq, IDX_HEADS), (0, 1, 3, 2))
    w_rows = w_rows.reshape(batch, nqb, 1, IDX_HEADS * tq)
    n_sel = min(DSA_TOPK, seq // 4)
    bias = _dsa_select(k_i, q_i_t, w_rows, batch, seq, n_sel)
    o_t = _dsa_attn(jnp.transpose(q_cat), k_cat, v_t, bias, batch, seq)
    return o_t


def _router_kernel(x_ref, w_ref, idx_ref, gate_ref):
    logits = jnp.dot(x_ref[...], w_ref[...], precision=lax.Precision.HIGHEST, preferred_element_type=F32)
    lane = lax.broadcasted_iota(I32, logits.shape, 1)
    lg = jnp.where(lane < MOE_EXPERTS, logits, -jnp.inf)
    v1 = jnp.max(lg, axis=-1, keepdims=True)
    i1 = jnp.min(jnp.where(lg == v1, lane, LANES), axis=-1, keepdims=True)
    lg2 = jnp.where(lane == i1, -jnp.inf, lg)
    v2 = jnp.max(lg2, axis=-1, keepdims=True)
    i2 = jnp.min(jnp.where(lg2 == v2, lane, LANES), axis=-1, keepdims=True)
    e2 = jnp.exp(v2 - v1)
    den = 1.0 + e2
    idx_ref[...] = jnp.where(lane == 0, i1, jnp.where(lane == 1, i2, 0))
    gate_ref[...] = jnp.where(lane == 0, 1.0 / den, jnp.where(lane == 1, e2 / den, 0.0))


def _router(x2, w_router):
    t, d = x2.shape
    wr = jnp.pad(w_router.astype(F32), ((0, 0), (0, LANES - w_router.shape[1])))
    tm = min(t, 512)
    return pl.pallas_call(
        _router_kernel,
        name="moe_router",
        grid=(t // tm,),
        in_specs=[pl.BlockSpec((tm, d), lambda i: (i, 0)),
                  pl.BlockSpec((d, LANES), lambda i: (0, 0))],
        out_specs=[pl.BlockSpec((tm, LANES), lambda i: (i, 0)),
                   pl.BlockSpec((tm, LANES), lambda i: (i, 0))],
        out_shape=[jax.ShapeDtypeStruct((t, LANES), I32),
                   jax.ShapeDtypeStruct((t, LANES), F32)],
        compiler_params=_params(("parallel",)),
    )(x2, wr)


def _row_copy(src_hbm, src_row, dst, dst_row, sem):
    return pltpu.make_async_copy(src_hbm.at[pl.ds(src_row, 1)], dst.at[pl.ds(dst_row, 1)], sem)


def _moe_ffn_kernel(be_ref, nu_ref, tok_ref, tok_next_ref, x_hbm, w1_ref, w3_ref, w2_ref, o_ref,
                    xg_ref, xb_ref, acc_ref, sem):
    i = pl.program_id(0)
    f = pl.program_id(1)
    nf = pl.num_programs(1)
    n_used = nu_ref[0]
    used = i < n_used
    tm = xb_ref.shape[0]
    slot = i % 2

    def start_gather(ids_ref, buf):
        def body(r, carry):
            _row_copy(x_hbm, ids_ref[r], xg_ref.at[buf], r, sem.at[buf]).start()
            return carry

        lax.fori_loop(0, tm, body, 0, unroll=8)

    @pl.when(used & (f == 0))
    def _():
        @pl.when(i == 0)
        def _():
            start_gather(tok_ref, 0)

        pltpu.make_async_copy(x_hbm.at[pl.ds(0, tm)], xg_ref.at[slot], sem.at[slot]).wait()

        @pl.when(i + 1 < n_used)
        def _():
            start_gather(tok_next_ref, 1 - slot)

        xb_ref[...] = xg_ref[slot].astype(BF16)
        acc_ref[...] = jnp.zeros_like(acc_ref)

    @pl.when(used)
    def _():
        xb = xb_ref[...]
        gate = jnp.dot(xb, w1_ref[...], preferred_element_type=F32)
        up = jnp.dot(xb, w3_ref[...], preferred_element_type=F32)
        h = (jax.nn.silu(gate) * up).astype(BF16)
        acc_ref[...] += jnp.dot(h, w2_ref[...], preferred_element_type=F32)

    @pl.when(used & (f == nf - 1))
    def _():
        o_ref[...] = acc_ref[...]

    @pl.when(jnp.logical_not(used) & (f == nf - 1))
    def _():
        o_ref[...] = jnp.zeros_like(o_ref)


def _moe_ffn(x2, slot_tok, w1, w3, w2, blk_expert, n_used, tm):
    n_slots = slot_tok.shape[0]
    d = x2.shape[1]
    f_dim = w1.shape[2]
    tf = _ffn_tile(f_dim, largest=1024)
    nf = f_dim // tf
    nt = n_slots // tm

    def f_sel(i, f, nu):
        return jnp.where(i < nu[0], f, nf - 1)

    grid_spec = pltpu.PrefetchScalarGridSpec(
        num_scalar_prefetch=2,
        grid=(nt, nf),
        in_specs=[pl.BlockSpec((tm,), lambda i, f, be, nu: (i,), memory_space=pltpu.SMEM),
                  pl.BlockSpec((tm,), lambda i, f, be, nu: (jnp.minimum(i + 1, nt - 1),), memory_space=pltpu.SMEM),
                  pl.BlockSpec(memory_space=pl.ANY),
                  pl.BlockSpec((None, d, tf), lambda i, f, be, nu: (be[i], 0, f_sel(i, f, nu))),
                  pl.BlockSpec((None, d, tf), lambda i, f, be, nu: (be[i], 0, f_sel(i, f, nu))),
                  pl.BlockSpec((None, tf, d), lambda i, f, be, nu: (be[i], f_sel(i, f, nu), 0))],
        out_specs=pl.BlockSpec((tm, d), lambda i, f, be, nu: (i, 0)),
        scratch_shapes=[pltpu.VMEM((2, tm, d), F32), pltpu.VMEM((tm, d), BF16), pltpu.VMEM((tm, d), F32),
                        pltpu.SemaphoreType.DMA((2,))],
    )
    return pl.pallas_call(
        _moe_ffn_kernel,
        name="moe_ffn",
        grid_spec=grid_spec,
        out_shape=jax.ShapeDtypeStruct((n_slots, d), F32),
        compiler_params=_params(("arbitrary", "arbitrary")),
    )(blk_expert, n_used, slot_tok, slot_tok, x2, w1, w3, w2)


def _moe_combine_kernel(dest_ref, dest_next_ref, y_hbm, x_ref, gate_ref, g_ref, b_ref, o_ref, y_ref, sem, *,
                        alpha, tb):
    i = pl.program_id(0)
    slot = i % 2

    def start_gather(ids_ref, buf):
        def body(tok, carry):
            for k in range(MOE_TOP_K):
                _row_copy(y_hbm, ids_ref[MOE_TOP_K * tok + k], y_ref.at[buf, k], tok, sem.at[buf]).start()
            return carry

        lax.fori_loop(0, tb, body, 0, unroll=4)

    @pl.when(i == 0)
    def _():
        start_gather(dest_ref, 0)

    for k in range(MOE_TOP_K):
        pltpu.make_async_copy(y_hbm.at[pl.ds(0, tb)], y_ref.at[slot, k], sem.at[slot]).wait()

    @pl.when(i + 1 < pl.num_programs(0))
    def _():
        start_gather(dest_next_ref, 1 - slot)

    gates = gate_ref[...]
    ff = gates[:, 0:1] * y_ref[slot, 0] + gates[:, 1:2] * y_ref[slot, 1]
    o_ref[...] = _layer_norm(alpha * x_ref[...] + ff, g_ref[...], b_ref[...])


def _moe_combine(x2, y, dest, gates, g, b, alpha):
    t, d = x2.shape
    tb = min(t, 256)
    nt = t // tb
    return pl.pallas_call(
        functools.partial(_moe_combine_kernel, alpha=alpha, tb=tb),
        name="moe_combine",
        grid=(nt,),
        in_specs=[pl.BlockSpec((tb * MOE_TOP_K,), lambda i: (i,), memory_space=pltpu.SMEM),
                  pl.BlockSpec((tb * MOE_TOP_K,), lambda i: (jnp.minimum(i + 1, nt - 1),), memory_space=pltpu.SMEM),
                  pl.BlockSpec(memory_space=pl.ANY),
                  pl.BlockSpec((tb, d), lambda i: (i, 0)),
                  pl.BlockSpec((tb, LANES), lambda i: (i, 0)),
                  pl.BlockSpec((1, d), lambda i: (0, 0)),
                  pl.BlockSpec((1, d), lambda i: (0, 0))],
        out_specs=pl.BlockSpec((tb, d), lambda i: (i, 0)),
        out_shape=jax.ShapeDtypeStruct((t, d), F32),
        scratch_shapes=[pltpu.VMEM((2, MOE_TOP_K, tb, d), F32), pltpu.SemaphoreType.DMA((2,))],
        compiler_params=_params(("arbitrary",)),
    )(dest, dest, y, x2, gates, g[None, :], b[None, :])


def _moe_layer(x2, w_router, w1, w3, w2, g, b, alpha):
    t, d = x2.shape
    n = t * MOE_TOP_K
    tm = min(512, t)
    idx_pad, gates = _router(x2, w_router)
    e_flat = idx_pad[:, :MOE_TOP_K].reshape(n)
    flat_ids = jnp.arange(n, dtype=I32)
    _, order = lax.sort((e_flat, flat_ids), num_keys=1, is_stable=True)
    _, inv_order = lax.sort((order, flat_ids), num_keys=1)
    experts = jnp.arange(MOE_EXPERTS, dtype=I32)
    counts = jnp.sum((e_flat[:, None] == experts[None, :]).astype(I32), axis=0)
    starts = jnp.cumsum(counts) - counts
    padded = (counts + tm - 1) // tm * tm
    pends = jnp.cumsum(padded)
    pstarts = pends - padded
    n_tiles = (n + MOE_EXPERTS * (tm - 1) + tm - 1) // tm
    n_used = (pends[-1] // tm).astype(I32)
    blk_start = jnp.arange(n_tiles, dtype=I32) * tm
    blk_expert = jnp.minimum(jnp.sum((pends[None, :] <= blk_start[:, None]).astype(I32), axis=1), MOE_EXPERTS - 1)
    blk_expert = jnp.where(jnp.arange(n_tiles) < n_used, blk_expert, blk_expert[jnp.maximum(n_used - 1, 0)])
    n_slots = n_tiles * tm
    slot = jnp.arange(n_slots, dtype=I32)
    slot_e = jnp.repeat(blk_expert, tm)
    shift = pstarts - starts
    order_tok = jnp.pad(order // MOE_TOP_K, (0, n_slots - n))
    slot_tok = jnp.zeros((n_slots,), I32)
    for e in range(MOE_EXPERTS):
        in_group = (slot_e == e) & (slot - pstarts[e] < counts[e]) & (slot < n_used * tm)
        slot_tok = jnp.where(in_group, jnp.roll(order_tok, shift[e]), slot_tok)
    dest = (inv_order + jnp.sum(jnp.where(e_flat[:, None] == experts[None, :], shift[None, :], 0), axis=1)).astype(I32)

    y = _moe_ffn(x2, slot_tok, w1.astype(BF16), w3.astype(BF16), w2.astype(BF16), blk_expert.astype(I32),
                 n_used.reshape(1), tm)
    return _moe_combine(x2, y, dest, gates, g, b, alpha)


def kernel(x, p, positions, swa_w_qkv, swa_sinks, swa_w_o, dense_w1, dense_w3, dense_w2, dsa_w_in, dsa_kv_norm,
           dsa_w_uk, dsa_w_uv, dsa_w_o, moe_router, moe_w1, moe_w3, moe_w2, ln_g, ln_b, ple_w_p, ple_w_g):
    batch, seq, d = x.shape
    depth = ln_g.shape[0]
    alpha = (2.0 * depth) ** 0.25
    t = batch * seq
    x2 = x.reshape(t, d)
    cos_t, sin_t = _rope_tables(positions)
    for i in range(depth):
        j = i // 2
        if i % 2 == 0:
            q, k2, v_t = _swa_qkv(x2, swa_w_qkv[j], cos_t, sin_t)
            o = _swa_attn(q, k2, v_t, swa_sinks[j], batch, seq)
            x2 = _proj_ln(o, swa_w_o[j], x2, ln_g[i, 0], ln_b[i, 0], alpha)
            x2 = _ffn_dense(x2, dense_w1[j], dense_w3[j], dense_w2[j], ln_g[i, 1], ln_b[i, 1], alpha)
        else:
            o = _dsa_mixer_out(x2, cos_t, sin_t, dsa_w_in[j], dsa_kv_norm[j], dsa_w_uk[j], dsa_w_uv[j], batch, seq)
            x2 = _proj_ln(o, dsa_w_o[j], x2, ln_g[i, 0], ln_b[i, 0], alpha)
            x2 = _moe_layer(x2, moe_router[j], moe_w1[j], moe_w3[j], moe_w2[j], ln_g[i, 1], ln_b[i, 1], alpha)
        x2 = _ple(x2, p[i].reshape(t, -1), ple_w_g[i], ple_w_p[i])
    return x2.reshape(batch, seq, d)
```

```python
import functools

import jax
import jax.numpy as jnp
from jax import lax
from jax.experimental import pallas as pl
from jax.experimental.pallas import tpu as pltpu

F32 = jnp.float32
BF16 = jnp.bfloat16
I32 = jnp.int32

LANES = 128
ROPE_THETA = 10000.0
LN_EPS = 1e-5
RMS_EPS = 1e-6
SWA_HEADS = 32
SWA_KV_HEADS = 4
SWA_GROUP = SWA_HEADS // SWA_KV_HEADS
SWA_HEAD_DIM = 64
SWA_WINDOW = 128
SWA_V_PAD = SWA_HEAD_DIM + 16
MLA_HEADS = 16
MLA_NOPE = 128
MLA_ROPE = 64
MLA_V = 128
MLA_LATENT = 512
MLA_SCALE = (MLA_NOPE + MLA_ROPE) ** -0.5
MLA_QK_PAD = 256
MLA_V_PAD = MLA_V + 16
LOG2_E = 1.4426950408889634
IDX_HEADS = 16
IDX_DIM = 64
IDX_ROPE = 32
DSA_TOPK = 256
MOE_EXPERTS = 8
MOE_TOP_K = 2
NEG_BIG = -1e30
M_INIT = -1e20
VMEM_LIMIT = 56 * 1024 * 1024


def _params(sem, vmem=VMEM_LIMIT, flags=None):
    return pltpu.CompilerParams(dimension_semantics=sem, vmem_limit_bytes=vmem, flags=flags)


def _layer_norm(y, g, b):
    mu = jnp.mean(y, axis=-1, keepdims=True)
    d = y - mu
    var = jnp.mean(d * d, axis=-1, keepdims=True)
    return d * lax.rsqrt(var + LN_EPS) * g + b


def _rope_lanes(a, cs, sn, half):
    lane = lax.broadcasted_iota(I32, a.shape, 1)
    first = (lane % 64) < half
    rot = jnp.where(first, pltpu.roll(a, LANES - half, 1), pltpu.roll(a, half, 1))
    return a * cs + rot * sn


def _rope_table_kernel(pos_ref, invf_ref, sgn_ref, cos_ref, sin_ref):
    ang = pos_ref[...].astype(F32) * invf_ref[...]
    cos_ref[...] = jnp.cos(ang)
    sin_ref[...] = jnp.sin(ang) * sgn_ref[...]


def _rope_tables(positions):
    t = positions.size
    pos = positions.reshape(t, 1).astype(I32)
    lane = jnp.arange(LANES)
    in64 = lane % 64
    f64 = ROPE_THETA ** (-jnp.arange(0, 64, 2, dtype=F32) / 64)
    f32_ = ROPE_THETA ** (-jnp.arange(0, IDX_ROPE, 2, dtype=F32) / IDX_ROPE)
    invf_a = f64[in64 % 32]
    sgn_a = jnp.where(in64 < 32, -1.0, 1.0).astype(F32)
    invf_b = jnp.where(in64 < IDX_ROPE, f32_[in64 % (IDX_ROPE // 2)], 0.0)
    sgn_b = jnp.where(in64 < IDX_ROPE // 2, -1.0, 1.0).astype(F32)
    invf = jnp.concatenate([invf_a, invf_b])[None, :]
    sgn = jnp.concatenate([sgn_a, sgn_b])[None, :]
    tm = min(t, 1024)
    return pl.pallas_call(
        _rope_table_kernel,
        name="rope_tables",
        grid=(t // tm,),
        in_specs=[pl.BlockSpec((tm, 1), lambda i: (i, 0)),
                  pl.BlockSpec((1, 2 * LANES), lambda i: (0, 0)),
                  pl.BlockSpec((1, 2 * LANES), lambda i: (0, 0))],
        out_specs=[pl.BlockSpec((tm, 2 * LANES), lambda i: (i, 0)),
                   pl.BlockSpec((tm, 2 * LANES), lambda i: (i, 0))],
        out_shape=[jax.ShapeDtypeStruct((t, 2 * LANES), F32)] * 2,
        compiler_params=_params(("parallel",)),
    )(pos, invf, sgn)


def _swa_qkv_kernel(x_ref, w_ref, wvt_ref, cos_ref, sin_ref, q_ref, k_ref, vt_ref, *, nq, nk):
    xb = x_ref[...].astype(BF16)
    acc = jnp.dot(xb, w_ref[...], preferred_element_type=F32)
    cs = cos_ref[:, :LANES]
    sn = sin_ref[:, :LANES]
    scale = SWA_HEAD_DIM ** -0.5 * LOG2_E
    for c in range(nq // LANES):
        a = acc[:, c * LANES:(c + 1) * LANES]
        q_ref[:, c * LANES:(c + 1) * LANES] = (_rope_lanes(a, cs, sn, 32) * scale).astype(BF16)
    low = lax.broadcasted_iota(I32, (acc.shape[0], LANES), 1) < SWA_HEAD_DIM
    for c in range(nk // LANES):
        kr = _rope_lanes(acc[:, nq + c * LANES:nq + (c + 1) * LANES], cs, sn, 32)
        swapped = pltpu.roll(kr, SWA_HEAD_DIM, 1)
        placements = (jnp.where(low, kr, 0.0), jnp.where(low, 0.0, swapped),
                      jnp.where(low, swapped, 0.0), jnp.where(low, 0.0, kr))
        for n, slab in enumerate(placements):
            k_ref[:, (4 * c + n) * LANES:(4 * c + n + 1) * LANES] = slab.astype(BF16)
    vt = lax.dot_general(wvt_ref[...], xb, (((1,), (1,)), ((), ())), preferred_element_type=F32)
    ones = jnp.ones((SWA_V_PAD - SWA_HEAD_DIM, xb.shape[0]), BF16)
    for kh in range(SWA_KV_HEADS):
        vt_ref[kh * SWA_V_PAD:kh * SWA_V_PAD + SWA_HEAD_DIM, :] = (
            vt[kh * SWA_HEAD_DIM:(kh + 1) * SWA_HEAD_DIM, :].astype(BF16))
        vt_ref[kh * SWA_V_PAD + SWA_HEAD_DIM:(kh + 1) * SWA_V_PAD, :] = ones


def _swa_qkv(x2, w_qkv, cos_t, sin_t):
    t, d = x2.shape
    nq = SWA_HEADS * SWA_HEAD_DIM
    nk = SWA_KV_HEADS * SWA_HEAD_DIM
    nk2 = SWA_KV_HEADS * 2 * LANES
    nvt = SWA_KV_HEADS * SWA_V_PAD
    tm = min(t, 512)
    wb = w_qkv.astype(BF16)
    return pl.pallas_call(
        functools.partial(_swa_qkv_kernel, nq=nq, nk=nk),
        name="swa_qkv",
        grid=(t // tm,),
        in_specs=[pl.BlockSpec((tm, d), lambda i: (i, 0)),
                  pl.BlockSpec((d, nq + nk), lambda i: (0, 0)),
                  pl.BlockSpec((nk, d), lambda i: (0, 0)),
                  pl.BlockSpec((tm, 2 * LANES), lambda i: (i, 0)),
                  pl.BlockSpec((tm, 2 * LANES), lambda i: (i, 0))],
        out_specs=[pl.BlockSpec((tm, nq), lambda i: (i, 0)),
                   pl.BlockSpec((tm, nk2), lambda i: (i, 0)),
                   pl.BlockSpec((nvt, tm), lambda i: (0, i))],
        out_shape=[jax.ShapeDtypeStruct((t, nq), BF16),
                   jax.ShapeDtypeStruct((t, nk2), BF16),
                   jax.ShapeDtypeStruct((nvt, t), BF16)],
        compiler_params=_params(("parallel",)),
    )(x2, wb[:, :nq + nk], jnp.transpose(wb[:, nq + nk:]), cos_t, sin_t)


def _swa_attn_kernel(sink_ref, q_ref, kp_ref, kc_ref, vp_ref, vc_ref, o_ref):
    blk = pl.program_id(1)
    bq = q_ref.shape[0]
    dh = SWA_HEAD_DIM
    slabs = SWA_GROUP // 2
    cols = slabs * bq
    kj = lax.broadcasted_iota(I32, (2 * bq, cols), 0)
    qi = lax.broadcasted_iota(I32, (2 * bq, cols), 1) % bq
    rel = qi + bq - kj
    allowed = (rel >= 0) & (rel < SWA_WINDOW) & ((kj >= bq) | (blk > 0))
    for kh in range(SWA_KV_HEADS):
        base = kh * SWA_GROUP * dh
        qs = jnp.concatenate([q_ref[:, base + s * LANES:base + (s + 1) * LANES] for s in range(slabs)], axis=0)
        vcat = jnp.concatenate([vp_ref[kh * SWA_V_PAD:(kh + 1) * SWA_V_PAD, :],
                                vc_ref[kh * SWA_V_PAD:(kh + 1) * SWA_V_PAD, :]], axis=1)
        for half in range(2):
            ks = (2 * kh + half) * LANES
            kcat = jnp.concatenate([kp_ref[:, ks:ks + LANES], kc_ref[:, ks:ks + LANES]], axis=0)
            heads = [kh * SWA_GROUP + 2 * s + half for s in range(slabs)]
            sink = jnp.concatenate([jnp.full((1, bq), sink_ref[h] * LOG2_E, F32) for h in heads], axis=1)
            logits = lax.dot_general(kcat, qs, (((1,), (1,)), ((), ())), preferred_element_type=F32)
            logits = jnp.where(allowed, logits, NEG_BIG)
            m = jnp.maximum(jnp.max(logits, axis=0, keepdims=True), sink)
            p = jnp.exp2(logits - m).astype(BF16)
            pv = jnp.dot(vcat, p, preferred_element_type=F32)
            inv = 1.0 / (pv[dh:dh + 1, :] + jnp.exp2(sink - m))
            o = (pv[:dh, :] * inv).astype(BF16)
            for s, h in enumerate(heads):
                o_ref[h * dh:(h + 1) * dh, :] = o[:, s * bq:(s + 1) * bq]


def _swa_attn(q, k2, v_t, sinks, batch, seq):
    t, nq = q.shape
    nk2 = k2.shape[1]
    nvt = v_t.shape[0]
    bq = SWA_WINDOW
    nb = seq // bq

    def cur(b, i):
        return (b * nb + i, 0)

    def prev(b, i):
        return (b * nb + jnp.maximum(i - 1, 0), 0)

    def cur_t(b, i):
        return (0, b * nb + i)

    def prev_t(b, i):
        return (0, b * nb + jnp.maximum(i - 1, 0))

    return pl.pallas_call(
        _swa_attn_kernel,
        name="swa_attn",
        grid=(batch, nb),
        in_specs=[pl.BlockSpec(memory_space=pltpu.SMEM),
                  pl.BlockSpec((bq, nq), cur),
                  pl.BlockSpec((bq, nk2), prev),
                  pl.BlockSpec((bq, nk2), cur),
                  pl.BlockSpec((nvt, bq), prev_t),
                  pl.BlockSpec((nvt, bq), cur_t)],
        out_specs=pl.BlockSpec((nq, bq), cur_t),
        out_shape=jax.ShapeDtypeStruct((nq, t), BF16),
        compiler_params=_params(("parallel", "parallel")),
    )(sinks.astype(F32), q, k2, k2, v_t, v_t)


def _proj_ln_kernel(at_ref, w_ref, res_ref, g_ref, b_ref, o_ref, *, alpha):
    mix = lax.dot_general(at_ref[...], w_ref[...], (((0,), (0,)), ((), ())), preferred_element_type=F32)
    o_ref[...] = _layer_norm(alpha * res_ref[...] + mix, g_ref[...], b_ref[...])


def _proj_ln(a_t, w, res, g, b, alpha):
    k, t = a_t.shape
    d = w.shape[1]
    tm = min(t, 512)
    return pl.pallas_call(
        functools.partial(_proj_ln_kernel, alpha=alpha),
        name="proj_ln",
        grid=(t // tm,),
        in_specs=[pl.BlockSpec((k, tm), lambda i: (0, i)),
                  pl.BlockSpec((k, d), lambda i: (0, 0)),
                  pl.BlockSpec((tm, d), lambda i: (i, 0)),
                  pl.BlockSpec((1, d), lambda i: (0, 0)),
                  pl.BlockSpec((1, d), lambda i: (0, 0))],
        out_specs=pl.BlockSpec((tm, d), lambda i: (i, 0)),
        out_shape=jax.ShapeDtypeStruct((t, d), F32),
        compiler_params=_params(("parallel",)),
    )(a_t, w.astype(BF16), res, g[None, :], b[None, :])


def _ffn_dense_kernel(x_ref, w1_ref, w3_ref, w2_ref, g_ref, b_ref, o_ref, xb_ref, acc_ref, *, alpha):
    f = pl.program_id(1)

    @pl.when(f == 0)
    def _():
        xb_ref[...] = x_ref[...].astype(BF16)
        acc_ref[...] = jnp.zeros_like(acc_ref)

    xb = xb_ref[...]
    gate = jnp.dot(xb, w1_ref[...], preferred_element_type=F32)
    up = jnp.dot(xb, w3_ref[...], preferred_element_type=F32)
    h = (jax.nn.silu(gate) * up).astype(BF16)
    acc_ref[...] += jnp.dot(h, w2_ref[...], preferred_element_type=F32)

    @pl.when(f == pl.num_programs(1) - 1)
    def _():
        o_ref[...] = _layer_norm(alpha * x_ref[...] + acc_ref[...], g_ref[...], b_ref[...])


def _ffn_tile(f_dim, largest=512):
    for tf in (largest, 512, 256, 128):
        if f_dim % tf == 0:
            return tf
    return f_dim


def _ffn_dense(x2, w1, w3, w2, g, b, alpha):
    t, d = x2.shape
    f_dim = w1.shape[1]
    tm = min(t, 512)
    tf = _ffn_tile(f_dim)
    return pl.pallas_call(
        functools.partial(_ffn_dense_kernel, alpha=alpha),
        name="ffn_dense",
        grid=(t // tm, f_dim // tf),
        in_specs=[pl.BlockSpec((tm, d), lambda i, f: (i, 0)),
                  pl.BlockSpec((d, tf), lambda i, f: (0, f)),
                  pl.BlockSpec((d, tf), lambda i, f: (0, f)),
                  pl.BlockSpec((tf, d), lambda i, f: (f, 0)),
                  pl.BlockSpec((1, d), lambda i, f: (0, 0)),
                  pl.BlockSpec((1, d), lambda i, f: (0, 0))],
        out_specs=pl.BlockSpec((tm, d), lambda i, f: (i, 0)),
        out_shape=jax.ShapeDtypeStruct((t, d), F32),
        scratch_shapes=[pltpu.VMEM((tm, d), BF16), pltpu.VMEM((tm, d), F32)],
        compiler_params=_params(("parallel", "arbitrary")),
    )(x2, w1.astype(BF16), w3.astype(BF16), w2.astype(BF16), g[None, :], b[None, :])


def _ple_kernel(x_ref, p_ref, wg_ref, wp_ref, o_ref):
    x = x_ref[...]
    gate = jax.nn.sigmoid(jnp.dot(x.astype(BF16), wg_ref[...], preferred_element_type=F32))
    emb = jnp.dot(p_ref[...].astype(BF16), wp_ref[...], preferred_element_type=F32)
    o_ref[...] = x + gate * emb


def _ple(x2, p2, w_g, w_p):
    t, d = x2.shape
    pd = p2.shape[1]
    tm = min(t, 512)
    return pl.pallas_call(
        _ple_kernel,
        name="ple",
        grid=(t // tm,),
        in_specs=[pl.BlockSpec((tm, d), lambda i: (i, 0)),
                  pl.BlockSpec((tm, pd), lambda i: (i, 0)),
                  pl.BlockSpec((d, d), lambda i: (0, 0)),
                  pl.BlockSpec((pd, d), lambda i: (0, 0))],
        out_specs=pl.BlockSpec((tm, d), lambda i: (i, 0)),
        out_shape=jax.ShapeDtypeStruct((t, d), F32),
        compiler_params=_params(("parallel",)),
    )(x2, p2, w_g.astype(BF16), w_p.astype(BF16))


def _dsa_q_kernel(x_ref, w_ref, cos_ref, sin_ref, q_ref):
    xb = x_ref[...].astype(BF16)
    acc = jnp.dot(xb, w_ref[...], preferred_element_type=F32)
    cs = cos_ref[:, :LANES]
    sn = sin_ref[:, :LANES]
    scale = MLA_SCALE * LOG2_E
    for h in range(MLA_HEADS):
        base = h * MLA_QK_PAD
        q_ref[:, base:base + LANES] = (acc[:, base:base + LANES] * scale).astype(BF16)
        a = acc[:, base + LANES:base + 2 * LANES]
        q_ref[:, base + LANES:base + 2 * LANES] = (_rope_lanes(a, cs, sn, 32) * scale).astype(BF16)


def _dsa_q(x2, w_q, cos_t, sin_t):
    t, d = x2.shape
    n = w_q.shape[1]
    tm = min(t, 512)
    return pl.pallas_call(
        _dsa_q_kernel,
        name="dsa_q",
        grid=(t // tm,),
        in_specs=[pl.BlockSpec((tm, d), lambda i: (i, 0)),
                  pl.BlockSpec((d, n), lambda i: (0, 0)),
                  pl.BlockSpec((tm, 2 * LANES), lambda i: (i, 0)),
                  pl.BlockSpec((tm, 2 * LANES), lambda i: (i, 0))],
        out_specs=pl.BlockSpec((tm, n), lambda i: (i, 0)),
        out_shape=jax.ShapeDtypeStruct((t, n), BF16),
        compiler_params=_params(("parallel",)),
    )(x2, w_q, cos_t, sin_t)


def _dsa_small_kernel(x_ref, w_ref, cos_ref, sin_ref, kvn_ref, ckv_ref, kr_ref, ki_ref, wi_ref):
    xb = x_ref[...].astype(BF16)
    acc = jnp.dot(xb, w_ref[...], preferred_element_type=F32)
    c = acc[:, :MLA_LATENT]
    ms = jnp.mean(c * c, axis=-1, keepdims=True)
    ckv_ref[...] = (c * lax.rsqrt(ms + RMS_EPS) * kvn_ref[...]).astype(BF16)
    o = MLA_LATENT
    kr_ref[...] = _rope_lanes(acc[:, o:o + LANES], cos_ref[:, :LANES], sin_ref[:, :LANES], 32).astype(BF16)
    ki_ref[...] = _rope_lanes(acc[:, o + LANES:o + 2 * LANES], cos_ref[:, LANES:], sin_ref[:, LANES:],
                              IDX_ROPE // 2).astype(BF16)
    wi_ref[...] = acc[:, o + 2 * LANES:o + 3 * LANES] * (IDX_HEADS ** -0.5)


def _dsa_small(x2, w_small, kv_norm, cos_t, sin_t):
    t, d = x2.shape
    n = w_small.shape[1]
    tm = min(t, 512)
    return pl.pallas_call(
        _dsa_small_kernel,
        name="dsa_small",
        grid=(t // tm,),
        in_specs=[pl.BlockSpec((tm, d), lambda i: (i, 0)),
                  pl.BlockSpec((d, n), lambda i: (0, 0)),
                  pl.BlockSpec((tm, 2 * LANES), lambda i: (i, 0)),
                  pl.BlockSpec((tm, 2 * LANES), lambda i: (i, 0)),
                  pl.BlockSpec((1, MLA_LATENT), lambda i: (0, 0))],
        out_specs=[pl.BlockSpec((tm, MLA_LATENT), lambda i: (i, 0)),
                   pl.BlockSpec((tm, LANES), lambda i: (i, 0)),
                   pl.BlockSpec((tm, LANES), lambda i: (i, 0)),
                   pl.BlockSpec((tm, LANES), lambda i: (i, 0))],
        out_shape=[jax.ShapeDtypeStruct((t, MLA_LATENT), BF16),
                   jax.ShapeDtypeStruct((t, LANES), BF16),
                   jax.ShapeDtypeStruct((t, LANES), BF16),
                   jax.ShapeDtypeStruct((t, LANES), F32)],
        compiler_params=_params(("parallel",)),
    )(x2, w_small, cos_t, sin_t, kv_norm[None, :].astype(F32))


def _dsa_qi_kernel(x_ref, w_ref, cos_ref, sin_ref, q_ref):
    xb = x_ref[...].astype(BF16)
    acc = jnp.dot(xb, w_ref[...], preferred_element_type=F32)
    cs = cos_ref[:, LANES:]
    sn = sin_ref[:, LANES:]
    scale = IDX_DIM ** -0.5
    for c in range(acc.shape[1] // LANES):
        a = acc[:, c * LANES:(c + 1) * LANES]
        q_ref[:, c * LANES:(c + 1) * LANES] = (_rope_lanes(a, cs, sn, IDX_ROPE // 2) * scale).astype(BF16)


def _dsa_qi(x2, w_qi, cos_t, sin_t):
    t, d = x2.shape
    n = w_qi.shape[1]
    tm = min(t, 512)
    return pl.pallas_call(
        _dsa_qi_kernel,
        name="dsa_qi",
        grid=(t // tm,),
        in_specs=[pl.BlockSpec((tm, d), lambda i: (i, 0)),
                  pl.BlockSpec((d, n), lambda i: (0, 0)),
                  pl.BlockSpec((tm, 2 * LANES), lambda i: (i, 0)),
                  pl.BlockSpec((tm, 2 * LANES), lambda i: (i, 0))],
        out_specs=pl.BlockSpec((tm, n), lambda i: (i, 0)),
        out_shape=jax.ShapeDtypeStruct((t, n), BF16),
        compiler_params=_params(("parallel",)),
    )(x2, w_qi, cos_t, sin_t)


def _dsa_kv_kernel(c_ref, kr_ref, wk_ref, wvt_ref, k_ref, vt_ref):
    c = c_ref[...]
    kn = jnp.dot(c, wk_ref[...], preferred_element_type=F32)
    kr = kr_ref[...]
    for h in range(MLA_HEADS):
        base = h * MLA_QK_PAD
        k_ref[:, base:base + LANES] = kn[:, h * LANES:(h + 1) * LANES].astype(BF16)
        k_ref[:, base + LANES:base + 2 * LANES] = kr
    vt = lax.dot_general(wvt_ref[...], c, (((1,), (1,)), ((), ())), preferred_element_type=F32)
    ones = jnp.ones((MLA_V_PAD - MLA_V, c.shape[0]), BF16)
    for h in range(MLA_HEADS):
        vt_ref[h * MLA_V_PAD:h * MLA_V_PAD + MLA_V, :] = vt[h * MLA_V:(h + 1) * MLA_V, :].astype(BF16)
        vt_ref[h * MLA_V_PAD + MLA_V:(h + 1) * MLA_V_PAD, :] = ones


def _dsa_kv(ckv, kr, w_uk_all, w_uv_t):
    t, c_dim = ckv.shape
    nk = MLA_HEADS * MLA_QK_PAD
    nv = w_uv_t.shape[0]
    nv_pad = MLA_HEADS * MLA_V_PAD
    tm = min(t, 512)
    return pl.pallas_call(
        _dsa_kv_kernel,
        name="dsa_kv",
        grid=(t // tm,),
        in_specs=[pl.BlockSpec((tm, c_dim), lambda i: (i, 0)),
                  pl.BlockSpec((tm, LANES), lambda i: (i, 0)),
                  pl.BlockSpec((c_dim, MLA_HEADS * MLA_NOPE), lambda i: (0, 0)),
                  pl.BlockSpec((nv, c_dim), lambda i: (0, 0))],
        out_specs=[pl.BlockSpec((tm, nk), lambda i: (i, 0)),
                   pl.BlockSpec((nv_pad, tm), lambda i: (0, i))],
        out_shape=[jax.ShapeDtypeStruct((t, nk), BF16),
                   jax.ShapeDtypeStruct((nv_pad, t), BF16)],
        compiler_params=_params(("parallel",)),
    )(ckv, kr, w_uk_all, w_uv_t)


def _float_sort_key(s):
    bits = lax.bitcast_convert_type(s, I32)
    return bits ^ ((bits >> 31) & jnp.int32(0x7FFFFFFF))


def _dsa_select_kernel(k_ref, q_ref, w_ref, bias_ref, key_ref, tie_ref, *, sc, cc, unroll, n_sel):
    j = pl.program_id(1)
    s_len, tq = key_ref.shape
    n_chunks = s_len // sc
    q_lo = j * tq
    n_need = (q_lo + tq + sc * unroll - 1) // (sc * unroll) * unroll
    qt = q_ref[...]
    w_row = w_ref[...]
    q_pos = q_lo + lax.broadcasted_iota(I32, (sc, tq), 1)
    k_off = lax.broadcasted_iota(I32, (sc, tq), 0)
    min_key = jnp.int32(-2 ** 31)

    def score_chunks(i, carry):
        for u in range(unroll):
            row0 = pl.multiple_of((i * unroll + u) * sc, sc)
            kc = k_ref[pl.ds(row0, sc), :]
            d = jnp.dot(kc, qt, preferred_element_type=F32)
            s = jnp.zeros((sc, tq), F32)
            for h in range(IDX_HEADS):
                s = s + jnp.maximum(d[:, h * tq:(h + 1) * tq], 0.0) * w_row[:, h * tq:(h + 1) * tq]
            causal = (row0 + k_off) <= q_pos
            key_ref[pl.ds(row0, sc), :] = jnp.where(causal, _float_sort_key(s), min_key)
        return carry

    lax.fori_loop(0, n_need // unroll, score_chunks, 0)

    n_cnt = (n_need * sc + cc - 1) // cc

    def fill_chunk(c, carry):
        row0 = pl.multiple_of(c * sc, sc)
        key_ref[pl.ds(row0, sc), :] = jnp.full((sc, tq), min_key, I32)
        return carry

    lax.fori_loop(n_need, n_cnt * (cc // sc), fill_chunk, 0)

    def count_ge(cand):
        def body(c, cnt):
            row0 = pl.multiple_of(c * cc, cc)
            hit = jnp.where(key_ref[pl.ds(row0, cc), :] >= cand, 1, 0).astype(I32)
            return cnt + jnp.sum(hit.reshape(cc // 32, 32, tq), axis=0)

        cnt = lax.fori_loop(0, n_cnt, body, jnp.zeros((32, tq), I32))
        return jnp.sum(cnt, axis=0, keepdims=True)

    zero = jnp.zeros((1, tq), I32)
    c0 = count_ge(zero)
    thr = jnp.where(c0 >= n_sel, zero, jnp.full((1, tq), min_key, I32))
    cnt_thr = jnp.where(c0 >= n_sel, c0, 0)

    def bit_step(it, carry):
        thr, cnt_thr = carry
        cand = thr + (jnp.int32(1) << (30 - it))
        cnt = count_ge(cand)
        take = cnt >= n_sel
        return jnp.where(take, cand, thr), jnp.where(take, cnt, cnt_thr)

    thr, cnt_thr = lax.fori_loop(0, 31, bit_step, (thr, cnt_thr))

    excess = cnt_thr - n_sel
    tie_ref[0:1, :] = jnp.full((1, tq), s_len, I32)

    @pl.when(jnp.max(excess) > 0)
    def _():
        def count_tie_below(x):
            def body(c, cnt):
                row0 = pl.multiple_of(c * cc, cc)
                idx = row0 + lax.broadcasted_iota(I32, (cc, tq), 0)
                hit = jnp.where((key_ref[pl.ds(row0, cc), :] == thr) & (idx < x), 1, 0).astype(I32)
                return cnt + jnp.sum(hit.reshape(cc // 32, 32, tq), axis=0)

            cnt = lax.fori_loop(0, n_cnt, body, jnp.zeros((32, tq), I32))
            return jnp.sum(cnt, axis=0, keepdims=True)

        n_tie = count_tie_below(jnp.full((1, tq), s_len, I32))
        need = n_sel - (cnt_thr - n_tie)
        n_bits = s_len.bit_length()

        def idx_step(it, x):
            cand = x + (jnp.int32(1) << (n_bits - 1 - it))
            return jnp.where(count_tie_below(cand) <= need, cand, x)

        x = lax.fori_loop(0, n_bits, idx_step, zero)
        tie_ref[0:1, :] = jnp.where(excess > 0, x, s_len)

    tie_below = tie_ref[0:1, :]
    thr_low = jnp.maximum(thr, min_key + 1)

    def write_chunk(c, carry):
        row0 = pl.multiple_of(c * sc, sc)
        thr_here = thr_low + jnp.where(k_off >= tie_below - row0, 1, 0)
        keep = key_ref[pl.ds(row0, sc), :] >= thr_here
        bias_ref[pl.ds(row0, sc), :] = jnp.where(keep, 0.0, NEG_BIG).astype(F32)
        return carry

    def mask_chunk(c, carry):
        row0 = pl.multiple_of(c * sc, sc)
        bias_ref[pl.ds(row0, sc), :] = jnp.full((sc, tq), NEG_BIG, F32)
        return carry

    lax.fori_loop(0, n_need, write_chunk, 0)
    lax.fori_loop(n_need, n_chunks, mask_chunk, 0)


def _dsa_select(k_i, q_i_t, w_rows, batch, seq, n_sel):
    tq = LANES
    nq = seq // tq
    sc = min(seq, 256)
    cc = min(seq, 1024)
    return pl.pallas_call(
        functools.partial(_dsa_select_kernel, sc=sc, cc=cc, unroll=2 if (seq // sc) % 2 == 0 else 1, n_sel=n_sel),
        name="dsa_select",
        grid=(batch, nq),
        in_specs=[pl.BlockSpec((None, seq, IDX_DIM), lambda b, j: (b, 0, 0)),
                  pl.BlockSpec((None, None, IDX_DIM, IDX_HEADS * tq), lambda b, j: (b, j, 0, 0)),
                  pl.BlockSpec((None, None, 1, IDX_HEADS * tq), lambda b, j: (b, j, 0, 0))],
        out_specs=pl.BlockSpec((None, seq, tq), lambda b, j: (b, 0, j)),
        out_shape=jax.ShapeDtypeStruct((batch, seq, seq), F32),
        scratch_shapes=[pltpu.VMEM((seq, tq), I32), pltpu.VMEM((8, tq), I32)],
        compiler_params=_params(("parallel", "parallel")),
    )(k_i, q_i_t, w_rows)


def _dsa_attn_kernel(qt_ref, k_ref, vt_ref, bias_ref, o_ref, m_ref, a_ref, *head_refs, sc):
    s_refs = head_refs[:MLA_HEADS]
    acc_refs = head_refs[MLA_HEADS:]
    j = pl.program_id(1)
    c = pl.program_id(2)
    tq = qt_ref.shape[1]
    last = (j * tq + tq - 1) // sc
    cur = c % 2
    prev = 1 - cur

    @pl.when(c == 0)
    def _():
        for h in range(MLA_HEADS):
            acc_refs[h][...] = jnp.zeros_like(acc_refs[h])
            s_refs[h][1] = jnp.full((sc, tq), NEG_BIG, F32)
        m_ref[...] = jnp.full_like(m_ref, M_INIT)
        a_ref[...] = jnp.ones_like(a_ref)

    def finish_prev():
        m_prev = m_ref[...]
        a_prev = a_ref[...]
        for h in range(MLA_HEADS):
            p = jnp.exp2(s_refs[h][prev] - m_prev[h:h + 1, :]).astype(BF16)
            pv = jnp.dot(vt_ref[h * MLA_V_PAD:(h + 1) * MLA_V_PAD, :], p, preferred_element_type=F32)
            acc_refs[h][...] = a_prev[h:h + 1, :] * acc_refs[h][...] + pv
        return m_prev

    @pl.when(c <= last)
    def _():
        m_prev = finish_prev()
        chunk_max = []
        for h in range(MLA_HEADS):
            qh = qt_ref[h * MLA_QK_PAD:(h + 1) * MLA_QK_PAD, :]
            kh = k_ref[:, h * MLA_QK_PAD:(h + 1) * MLA_QK_PAD]
            s = jnp.dot(kh, qh, preferred_element_type=F32) + bias_ref[...]
            s_refs[h][cur] = s
            chunk_max.append(jnp.max(s, axis=0, keepdims=True))
        m_new = jnp.maximum(m_prev, jnp.concatenate(chunk_max, axis=0))
        a_ref[...] = jnp.exp2(m_prev - m_new)
        m_ref[...] = m_new

    @pl.when(c == last + 1)
    def _():
        finish_prev()
        for h in range(MLA_HEADS):
            inv = 1.0 / acc_refs[h][MLA_V:MLA_V + 1, :]
            o_ref[h * MLA_V:(h + 1) * MLA_V, :] = (acc_refs[h][:MLA_V, :] * inv).astype(BF16)


def _dsa_attn(q_cat_t, k_cat, v_t, bias, batch, seq):
    nqk, t = q_cat_t.shape
    nv_pad = v_t.shape[0]
    nv = MLA_HEADS * MLA_V
    tq = min(seq, 512)
    sc = min(seq, 256)
    nq = seq // tq
    nc = seq // sc

    def last_chunk(j):
        return (j * tq + tq - 1) // sc

    return pl.pallas_call(
        functools.partial(_dsa_attn_kernel, sc=sc),
        name="dsa_attn",
        grid=(batch, nq, nc + 1),
        in_specs=[pl.BlockSpec((nqk, tq), lambda b, j, c: (0, b * nq + j)),
                  pl.BlockSpec((sc, nqk), lambda b, j, c: (b * nc + jnp.minimum(c, last_chunk(j)), 0)),
                  pl.BlockSpec((nv_pad, sc), lambda b, j, c: (0, b * nc + jnp.clip(c - 1, 0, last_chunk(j)))),
                  pl.BlockSpec((None, sc, tq), lambda b, j, c: (b, jnp.minimum(c, last_chunk(j)), j))],
        out_specs=pl.BlockSpec((nv, tq), lambda b, j, c: (0, b * nq + j)),
        out_shape=jax.ShapeDtypeStruct((nv, t), BF16),
        scratch_shapes=([pltpu.VMEM((MLA_HEADS, tq), F32), pltpu.VMEM((MLA_HEADS, tq), F32)]
                        + [pltpu.VMEM((2, sc, tq), F32)] * MLA_HEADS
                        + [pltpu.VMEM((MLA_V_PAD, tq), F32)] * MLA_HEADS),
        compiler_params=_params(("parallel", "parallel", "arbitrary")),
    )(q_cat_t, k_cat, v_t, bias)


def _dsa_mixer_out(x2, cos_t, sin_t, w_in, kv_norm, w_uk, w_uv, batch, seq):
    t, d = x2.shape
    nq = MLA_HEADS * (MLA_NOPE + MLA_ROPE)
    o_c = nq
    o_kr = o_c + MLA_LATENT
    o_qi = o_kr + MLA_ROPE
    o_ki = o_qi + IDX_HEADS * IDX_DIM
    o_wi = o_ki + IDX_DIM
    wb = w_in.astype(BF16)
    wq = wb[:, :nq].reshape(d, MLA_HEADS, MLA_NOPE + MLA_ROPE)
    wq = jnp.pad(wq, ((0, 0), (0, 0), (0, MLA_QK_PAD - MLA_NOPE - MLA_ROPE))).reshape(d, MLA_HEADS * MLA_QK_PAD)
    pad_to_lanes = lambda w: jnp.pad(w, ((0, 0), (0, LANES - w.shape[1])))
    w_small = jnp.concatenate([wb[:, o_c:o_kr], pad_to_lanes(wb[:, o_kr:o_qi]), pad_to_lanes(wb[:, o_ki:o_wi]),
                               pad_to_lanes(wb[:, o_wi:])], axis=1)
    w_qi = wb[:, o_qi:o_ki]
    w_uk_all = jnp.transpose(w_uk, (1, 0, 2)).reshape(MLA_LATENT, MLA_HEADS * MLA_NOPE).astype(BF16)
    w_uv_t = jnp.transpose(w_uv, (0, 2, 1)).reshape(MLA_HEADS * MLA_V, MLA_LATENT).astype(BF16)

    q_cat = _dsa_q(x2, wq, cos_t, sin_t)
    ckv, kr, ki, wi = _dsa_small(x2, w_small, kv_norm, cos_t, sin_t)
    qi = _dsa_qi(x2, w_qi, cos_t, sin_t)
    k_cat, v_t = _dsa_kv(ckv, kr, w_uk_all, w_uv_t)

    tq = LANES
    nqb = seq // tq
    k_i = ki[:, :IDX_DIM].reshape(batch, seq, IDX_DIM)
    q_i_t = jnp.transpose(qi.reshape(batch, nqb, tq, IDX_HEADS, IDX_DIM), (0, 1, 4, 3, 2))
    q_i_t = q_i_t.reshape(batch, nqb, IDX_DIM, IDX_HEADS * tq)
    w_rows = jnp.transpose(wi[:, :IDX_HEADS].reshape(batch, nqb, tq, IDX_HEADS), (0, 1, 3, 2))
    w_rows = w_rows.reshape(batch, nqb, 1, IDX_HEADS * tq)
    n_sel = min(DSA_TOPK, seq // 4)
    bias = _dsa_select(k_i, q_i_t, w_rows, batch, seq, n_sel)
    o_t = _dsa_attn(jnp.transpose(q_cat), k_cat, v_t, bias, batch, seq)
    return o_t


def _router_kernel(x_ref, w_ref, idx_ref, gate_ref):
    logits = jnp.dot(x_ref[...], w_ref[...], precision=lax.Precision.HIGHEST, preferred_element_type=F32)
    lane = lax.broadcasted_iota(I32, logits.shape, 1)
    lg = jnp.where(lane < MOE_EXPERTS, logits, -jnp.inf)
    v1 = jnp.max(lg, axis=-1, keepdims=True)
    i1 = jnp.min(jnp.where(lg == v1, lane, LANES), axis=-1, keepdims=True)
    lg2 = jnp.where(lane == i1, -jnp.inf, lg)
    v2 = jnp.max(lg2, axis=-1, keepdims=True)
    i2 = jnp.min(jnp.where(lg2 == v2, lane, LANES), axis=-1, keepdims=True)
    e2 = jnp.exp(v2 - v1)
    den = 1.0 + e2
    idx_ref[...] = jnp.where(lane == 0, i1, jnp.where(lane == 1, i2, 0))
    gate_ref[...] = jnp.where(lane == 0, 1.0 / den, jnp.where(lane == 1, e2 / den, 0.0))


def _router(x2, w_router):
    t, d = x2.shape
    wr = jnp.pad(w_router.astype(F32), ((0, 0), (0, LANES - w_router.shape[1])))
    tm = min(t, 512)
    return pl.pallas_call(
        _router_kernel,
        name="moe_router",
        grid=(t // tm,),
        in_specs=[pl.BlockSpec((tm, d), lambda i: (i, 0)),
                  pl.BlockSpec((d, LANES), lambda i: (0, 0))],
        out_specs=[pl.BlockSpec((tm, LANES), lambda i: (i, 0)),
                   pl.BlockSpec((tm, LANES), lambda i: (i, 0))],
        out_shape=[jax.ShapeDtypeStruct((t, LANES), I32),
                   jax.ShapeDtypeStruct((t, LANES), F32)],
        compiler_params=_params(("parallel",)),
    )(x2, wr)


def _row_copy(src_hbm, src_row, dst, dst_row, sem):
    return pltpu.make_async_copy(src_hbm.at[pl.ds(src_row, 1)], dst.at[pl.ds(dst_row, 1)], sem)


def _moe_ffn_kernel(be_ref, nu_ref, tok_ref, tok_next_ref, x_hbm, w1_ref, w3_ref, w2_ref, o_ref,
                    xg_ref, xb_ref, acc_ref, sem):
    i = pl.program_id(0)
    f = pl.program_id(1)
    nf = pl.num_programs(1)
    n_used = nu_ref[0]
    used = i < n_used
    tm = xb_ref.shape[0]
    slot = i % 2

    def start_gather(ids_ref, buf):
        def body(r, carry):
            _row_copy(x_hbm, ids_ref[r], xg_ref.at[buf], r, sem.at[buf]).start()
            return carry

        lax.fori_loop(0, tm, body, 0, unroll=8)

    @pl.when(used & (f == 0))
    def _():
        @pl.when(i == 0)
        def _():
            start_gather(tok_ref, 0)

        pltpu.make_async_copy(x_hbm.at[pl.ds(0, tm)], xg_ref.at[slot], sem.at[slot]).wait()

        @pl.when(i + 1 < n_used)
        def _():
            start_gather(tok_next_ref, 1 - slot)

        xb_ref[...] = xg_ref[slot].astype(BF16)
        acc_ref[...] = jnp.zeros_like(acc_ref)

    @pl.when(used)
    def _():
        xb = xb_ref[...]
        gate = jnp.dot(xb, w1_ref[...], preferred_element_type=F32)
        up = jnp.dot(xb, w3_ref[...], preferred_element_type=F32)
        h = (jax.nn.silu(gate) * up).astype(BF16)
        acc_ref[...] += jnp.dot(h, w2_ref[...], preferred_element_type=F32)

    @pl.when(used & (f == nf - 1))
    def _():
        o_ref[...] = acc_ref[...]

    @pl.when(jnp.logical_not(used) & (f == nf - 1))
    def _():
        o_ref[...] = jnp.zeros_like(o_ref)


def _moe_ffn(x2, slot_tok, w1, w3, w2, blk_expert, n_used, tm):
    n_slots = slot_tok.shape[0]
    d = x2.shape[1]
    f_dim = w1.shape[2]
    tf = _ffn_tile(f_dim, largest=1024)
    nf = f_dim // tf
    nt = n_slots // tm

    def f_sel(i, f, nu):
        return jnp.where(i < nu[0], f, nf - 1)

    grid_spec = pltpu.PrefetchScalarGridSpec(
        num_scalar_prefetch=2,
        grid=(nt, nf),
        in_specs=[pl.BlockSpec((tm,), lambda i, f, be, nu: (i,), memory_space=pltpu.SMEM),
                  pl.BlockSpec((tm,), lambda i, f, be, nu: (jnp.minimum(i + 1, nt - 1),), memory_space=pltpu.SMEM),
                  pl.BlockSpec(memory_space=pl.ANY),
                  pl.BlockSpec((None, d, tf), lambda i, f, be, nu: (be[i], 0, f_sel(i, f, nu))),
                  pl.BlockSpec((None, d, tf), lambda i, f, be, nu: (be[i], 0, f_sel(i, f, nu))),
                  pl.BlockSpec((None, tf, d), lambda i, f, be, nu: (be[i], f_sel(i, f, nu), 0))],
        out_specs=pl.BlockSpec((tm, d), lambda i, f, be, nu: (i, 0)),
        scratch_shapes=[pltpu.VMEM((2, tm, d), F32), pltpu.VMEM((tm, d), BF16), pltpu.VMEM((tm, d), F32),
                        pltpu.SemaphoreType.DMA((2,))],
    )
    return pl.pallas_call(
        _moe_ffn_kernel,
        name="moe_ffn",
        grid_spec=grid_spec,
        out_shape=jax.ShapeDtypeStruct((n_slots, d), F32),
        compiler_params=_params(("arbitrary", "arbitrary")),
    )(blk_expert, n_used, slot_tok, slot_tok, x2, w1, w3, w2)


def _moe_combine_kernel(dest_ref, dest_next_ref, y_hbm, x_ref, gate_ref, g_ref, b_ref, o_ref, y_ref, sem, *,
                        alpha, tb):
    i = pl.program_id(0)
    slot = i % 2

    def start_gather(ids_ref, buf):
        def body(tok, carry):
            for k in range(MOE_TOP_K):
                _row_copy(y_hbm, ids_ref[MOE_TOP_K * tok + k], y_ref.at[buf, k], tok, sem.at[buf]).start()
            return carry

        lax.fori_loop(0, tb, body, 0, unroll=4)

    @pl.when(i == 0)
    def _():
        start_gather(dest_ref, 0)

    for k in range(MOE_TOP_K):
        pltpu.make_async_copy(y_hbm.at[pl.ds(0, tb)], y_ref.at[slot, k], sem.at[slot]).wait()

    @pl.when(i + 1 < pl.num_programs(0))
    def _():
        start_gather(dest_next_ref, 1 - slot)

    gates = gate_ref[...]
    ff = gates[:, 0:1] * y_ref[slot, 0] + gates[:, 1:2] * y_ref[slot, 1]
    o_ref[...] = _layer_norm(alpha * x_ref[...] + ff, g_ref[...], b_ref[...])


def _moe_combine(x2, y, dest, gates, g, b, alpha):
    t, d = x2.shape
    tb = min(t, 256)
    nt = t // tb
    return pl.pallas_call(
        functools.partial(_moe_combine_kernel, alpha=alpha, tb=tb),
        name="moe_combine",
        grid=(nt,),
        in_specs=[pl.BlockSpec((tb * MOE_TOP_K,), lambda i: (i,), memory_space=pltpu.SMEM),
                  pl.BlockSpec((tb * MOE_TOP_K,), lambda i: (jnp.minimum(i + 1, nt - 1),), memory_space=pltpu.SMEM),
                  pl.BlockSpec(memory_space=pl.ANY),
                  pl.BlockSpec((tb, d), lambda i: (i, 0)),
                  pl.BlockSpec((tb, LANES), lambda i: (i, 0)),
                  pl.BlockSpec((1, d), lambda i: (0, 0)),
                  pl.BlockSpec((1, d), lambda i: (0, 0))],
        out_specs=pl.BlockSpec((tb, d), lambda i: (i, 0)),
        out_shape=jax.ShapeDtypeStruct((t, d), F32),
        scratch_shapes=[pltpu.VMEM((2, MOE_TOP_K, tb, d), F32), pltpu.SemaphoreType.DMA((2,))],
        compiler_params=_params(("arbitrary",)),
    )(dest, dest, y, x2, gates, g[None, :], b[None, :])


def _moe_layer(x2, w_router, w1, w3, w2, g, b, alpha):
    t, d = x2.shape
    n = t * MOE_TOP_K
    tm = min(512, t)
    idx_pad, gates = _router(x2, w_router)
    e_flat = idx_pad[:, :MOE_TOP_K].reshape(n)
    flat_ids = jnp.arange(n, dtype=I32)
    _, order = lax.sort((e_flat, flat_ids), num_keys=1, is_stable=True)
    _, inv_order = lax.sort((order, flat_ids), num_keys=1)
    experts = jnp.arange(MOE_EXPERTS, dtype=I32)
    counts = jnp.sum((e_flat[:, None] == experts[None, :]).astype(I32), axis=0)
    starts = jnp.cumsum(counts) - counts
    padded = (counts + tm - 1) // tm * tm
    pends = jnp.cumsum(padded)
    pstarts = pends - padded
    n_tiles = (n + MOE_EXPERTS * (tm - 1) + tm - 1) // tm
    n_used = (pends[-1] // tm).astype(I32)
    blk_start = jnp.arange(n_tiles, dtype=I32) * tm
    blk_expert = jnp.minimum(jnp.sum((pends[None, :] <= blk_start[:, None]).astype(I32), axis=1), MOE_EXPERTS - 1)
    blk_expert = jnp.where(jnp.arange(n_tiles) < n_used, blk_expert, blk_expert[jnp.maximum(n_used - 1, 0)])
    n_slots = n_tiles * tm
    slot = jnp.arange(n_slots, dtype=I32)
    slot_e = jnp.repeat(blk_expert, tm)
    shift = pstarts - starts
    order_tok = jnp.pad(order // MOE_TOP_K, (0, n_slots - n))
    slot_tok = jnp.zeros((n_slots,), I32)
    for e in range(MOE_EXPERTS):
        in_group = (slot_e == e) & (slot - pstarts[e] < counts[e]) & (slot < n_used * tm)
        slot_tok = jnp.where(in_group, jnp.roll(order_tok, shift[e]), slot_tok)
    dest = (inv_order + jnp.sum(jnp.where(e_flat[:, None] == experts[None, :], shift[None, :], 0), axis=1)).astype(I32)

    y = _moe_ffn(x2, slot_tok, w1.astype(BF16), w3.astype(BF16), w2.astype(BF16), blk_expert.astype(I32),
                 n_used.reshape(1), tm)
    return _moe_combine(x2, y, dest, gates, g, b, alpha)


def kernel(x, p, positions, swa_w_qkv, swa_sinks, swa_w_o, dense_w1, dense_w3, dense_w2, dsa_w_in, dsa_kv_norm,
           dsa_w_uk, dsa_w_uv, dsa_w_o, moe_router, moe_w1, moe_w3, moe_w2, ln_g, ln_b, ple_w_p, ple_w_g):
    batch, seq, d = x.shape
    depth = ln_g.shape[0]
    alpha = (2.0 * depth) ** 0.25
    t = batch * seq
    x2 = x.reshape(t, d)
    cos_t, sin_t = _rope_tables(positions)
    for i in range(depth):
        j = i // 2
        if i % 2 == 0:
            q, k2, v_t = _swa_qkv(x2, swa_w_qkv[j], cos_t, sin_t)
            o = _swa_attn(q, k2, v_t, swa_sinks[j], batch, seq)
            x2 = _proj_ln(o, swa_w_o[j], x2, ln_g[i, 0], ln_b[i, 0], alpha)
            x2 = _ffn_dense(x2, dense_w1[j], dense_w3[j], dense_w2[j], ln_g[i, 1], ln_b[i, 1], alpha)
        else:
            o = _dsa_mixer_out(x2, cos_t, sin_t, dsa_w_in[j], dsa_kv_norm[j], dsa_w_uk[j], dsa_w_uv[j], batch, seq)
            x2 = _proj_ln(o, dsa_w_o[j], x2, ln_g[i, 0], ln_b[i, 0], alpha)
            x2 = _moe_layer(x2, moe_router[j], moe_w1[j], moe_w3[j], moe_w2[j], ln_g[i, 1], ln_b[i, 1], alpha)
        x2 = _ple(x2, p[i].reshape(t, -1), ple_w_g[i], ple_w_p[i])
    return x2.reshape(batch, seq, d)
```

```python
import functools

import jax
import jax.numpy as jnp
from jax import lax
from jax.experimental import pallas as pl
from jax.experimental.pallas import tpu as pltpu

F32 = jnp.float32
BF16 = jnp.bfloat16
I32 = jnp.int32

LANES = 128
ROPE_THETA = 10000.0
LN_EPS = 1e-5
RMS_EPS = 1e-6
SWA_HEADS = 32
SWA_KV_HEADS = 4
SWA_GROUP = SWA_HEADS // SWA_KV_HEADS
SWA_HEAD_DIM = 64
SWA_WINDOW = 128
SWA_V_PAD = SWA_HEAD_DIM + 16
MLA_HEADS = 16
MLA_NOPE = 128
MLA_ROPE = 64
MLA_V = 128
MLA_LATENT = 512
MLA_SCALE = (MLA_NOPE + MLA_ROPE) ** -0.5
MLA_QK_PAD = 256
MLA_V_PAD = MLA_V + 16
LOG2_E = 1.4426950408889634
IDX_HEADS = 16
IDX_DIM = 64
IDX_ROPE = 32
DSA_TOPK = 256
MOE_EXPERTS = 8
MOE_TOP_K = 2
NEG_BIG = -1e30
M_INIT = -1e20
VMEM_LIMIT = 56 * 1024 * 1024


def _params(sem, vmem=VMEM_LIMIT, flags=None):
    return pltpu.CompilerParams(dimension_semantics=sem, vmem_limit_bytes=vmem, flags=flags)


def _layer_norm(y, g, b):
    mu = jnp.mean(y, axis=-1, keepdims=True)
    d = y - mu
    var = jnp.mean(d * d, axis=-1, keepdims=True)
    return d * lax.rsqrt(var + LN_EPS) * g + b


def _rope_lanes(a, cs, sn, half):
    lane = lax.broadcasted_iota(I32, a.shape, 1)
    first = (lane % 64) < half
    rot = jnp.where(first, pltpu.roll(a, LANES - half, 1), pltpu.roll(a, half, 1))
    return a * cs + rot * sn


def _rope_table_kernel(pos_ref, invf_ref, sgn_ref, cos_ref, sin_ref):
    ang = pos_ref[...].astype(F32) * invf_ref[...]
    cos_ref[...] = jnp.cos(ang)
    sin_ref[...] = jnp.sin(ang) * sgn_ref[...]


def _rope_tables(positions):
    t = positions.size
    pos = positions.reshape(t, 1).astype(I32)
    lane = jnp.arange(LANES)
    in64 = lane % 64
    f64 = ROPE_THETA ** (-jnp.arange(0, 64, 2, dtype=F32) / 64)
    f32_ = ROPE_THETA ** (-jnp.arange(0, IDX_ROPE, 2, dtype=F32) / IDX_ROPE)
    invf_a = f64[in64 % 32]
    sgn_a = jnp.where(in64 < 32, -1.0, 1.0).astype(F32)
    invf_b = jnp.where(in64 < IDX_ROPE, f32_[in64 % (IDX_ROPE // 2)], 0.0)
    sgn_b = jnp.where(in64 < IDX_ROPE // 2, -1.0, 1.0).astype(F32)
    invf = jnp.concatenate([invf_a, invf_b])[None, :]
    sgn = jnp.concatenate([sgn_a, sgn_b])[None, :]
    tm = min(t, 1024)
    return pl.pallas_call(
        _rope_table_kernel,
        name="rope_tables",
        grid=(t // tm,),
        in_specs=[pl.BlockSpec((tm, 1), lambda i: (i, 0)),
                  pl.BlockSpec((1, 2 * LANES), lambda i: (0, 0)),
                  pl.BlockSpec((1, 2 * LANES), lambda i: (0, 0))],
        out_specs=[pl.BlockSpec((tm, 2 * LANES), lambda i: (i, 0)),
                   pl.BlockSpec((tm, 2 * LANES), lambda i: (i, 0))],
        out_shape=[jax.ShapeDtypeStruct((t, 2 * LANES), F32)] * 2,
        compiler_params=_params(("parallel",)),
    )(pos, invf, sgn)


def _swa_qkv_kernel(x_ref, w_ref, wvt_ref, cos_ref, sin_ref, q_ref, k_ref, vt_ref, *, nq, nk):
    xb = x_ref[...].astype(BF16)
    acc = jnp.dot(xb, w_ref[...], preferred_element_type=F32)
    cs = cos_ref[:, :LANES]
    sn = sin_ref[:, :LANES]
    scale = SWA_HEAD_DIM ** -0.5 * LOG2_E
    for c in range(nq // LANES):
        a = acc[:, c * LANES:(c + 1) * LANES]
        q_ref[:, c * LANES:(c + 1) * LANES] = (_rope_lanes(a, cs, sn, 32) * scale).astype(BF16)
    low = lax.broadcasted_iota(I32, (acc.shape[0], LANES), 1) < SWA_HEAD_DIM
    for c in range(nk // LANES):
        kr = _rope_lanes(acc[:, nq + c * LANES:nq + (c + 1) * LANES], cs, sn, 32)
        swapped = pltpu.roll(kr, SWA_HEAD_DIM, 1)
        placements = (jnp.where(low, kr, 0.0), jnp.where(low, 0.0, swapped),
                      jnp.where(low, swapped, 0.0), jnp.where(low, 0.0, kr))
        for n, slab in enumerate(placements):
            k_ref[:, (4 * c + n) * LANES:(4 * c + n + 1) * LANES] = slab.astype(BF16)
    vt = lax.dot_general(wvt_ref[...], xb, (((1,), (1,)), ((), ())), preferred_element_type=F32)
    ones = jnp.ones((SWA_V_PAD - SWA_HEAD_DIM, xb.shape[0]), BF16)
    for kh in range(SWA_KV_HEADS):
        vt_ref[kh * SWA_V_PAD:kh * SWA_V_PAD + SWA_HEAD_DIM, :] = (
            vt[kh * SWA_HEAD_DIM:(kh + 1) * SWA_HEAD_DIM, :].astype(BF16))
        vt_ref[kh * SWA_V_PAD + SWA_HEAD_DIM:(kh + 1) * SWA_V_PAD, :] = ones


def _swa_qkv(x2, w_qkv, cos_t, sin_t):
    t, d = x2.shape
    nq = SWA_HEADS * SWA_HEAD_DIM
    nk = SWA_KV_HEADS * SWA_HEAD_DIM
    nk2 = SWA_KV_HEADS * 2 * LANES
    nvt = SWA_KV_HEADS * SWA_V_PAD
    tm = min(t, 512)
    wb = w_qkv.astype(BF16)
    return pl.pallas_call(
        functools.partial(_swa_qkv_kernel, nq=nq, nk=nk),
        name="swa_qkv",
        grid=(t // tm,),
        in_specs=[pl.BlockSpec((tm, d), lambda i: (i, 0)),
                  pl.BlockSpec((d, nq + nk), lambda i: (0, 0)),
                  pl.BlockSpec((nk, d), lambda i: (0, 0)),
                  pl.BlockSpec((tm, 2 * LANES), lambda i: (i, 0)),
                  pl.BlockSpec((tm, 2 * LANES), lambda i: (i, 0))],
        out_specs=[pl.BlockSpec((tm, nq), lambda i: (i, 0)),
                   pl.BlockSpec((tm, nk2), lambda i: (i, 0)),
                   pl.BlockSpec((nvt, tm), lambda i: (0, i))],
        out_shape=[jax.ShapeDtypeStruct((t, nq), BF16),
                   jax.ShapeDtypeStruct((t, nk2), BF16),
                   jax.ShapeDtypeStruct((nvt, t), BF16)],
        compiler_params=_params(("parallel",)),
    )(x2, wb[:, :nq + nk], jnp.transpose(wb[:, nq + nk:]), cos_t, sin_t)


def _swa_attn_kernel(sink_ref, q_ref, kp_ref, kc_ref, vp_ref, vc_ref, o_ref):
    blk = pl.program_id(1)
    bq = q_ref.shape[0]
    dh = SWA_HEAD_DIM
    slabs = SWA_GROUP // 2
    cols = slabs * bq
    kj = lax.broadcasted_iota(I32, (2 * bq, cols), 0)
    qi = lax.broadcasted_iota(I32, (2 * bq, cols), 1) % bq
    rel = qi + bq - kj
    allowed = (rel >= 0) & (rel < SWA_WINDOW) & ((kj >= bq) | (blk > 0))
    for kh in range(SWA_KV_HEADS):
        base = kh * SWA_GROUP * dh
        qs = jnp.concatenate([q_ref[:, base + s * LANES:base + (s + 1) * LANES] for s in range(slabs)], axis=0)
        vcat = jnp.concatenate([vp_ref[kh * SWA_V_PAD:(kh + 1) * SWA_V_PAD, :],
                                vc_ref[kh * SWA_V_PAD:(kh + 1) * SWA_V_PAD, :]], axis=1)
        for half in range(2):
            ks = (2 * kh + half) * LANES
            kcat = jnp.concatenate([kp_ref[:, ks:ks + LANES], kc_ref[:, ks:ks + LANES]], axis=0)
            heads = [kh * SWA_GROUP + 2 * s + half for s in range(slabs)]
            sink = jnp.concatenate([jnp.full((1, bq), sink_ref[h] * LOG2_E, F32) for h in heads], axis=1)
            logits = lax.dot_general(kcat, qs, (((1,), (1,)), ((), ())), preferred_element_type=F32)
            logits = jnp.where(allowed, logits, NEG_BIG)
            m = jnp.maximum(jnp.max(logits, axis=0, keepdims=True), sink)
            p = jnp.exp2(logits - m).astype(BF16)
            pv = jnp.dot(vcat, p, preferred_element_type=F32)
            inv = 1.0 / (pv[dh:dh + 1, :] + jnp.exp2(sink - m))
            o = (pv[:dh, :] * inv).astype(BF16)
            for s, h in enumerate(heads):
                o_ref[h * dh:(h + 1) * dh, :] = o[:, s * bq:(s + 1) * bq]


def _swa_attn(q, k2, v_t, sinks, batch, seq):
    t, nq = q.shape
    nk2 = k2.shape[1]
    nvt = v_t.shape[0]
    bq = SWA_WINDOW
    nb = seq // bq

    def cur(b, i):
        return (b * nb + i, 0)

    def prev(b, i):
        return (b * nb + jnp.maximum(i - 1, 0), 0)

    def cur_t(b, i):
        return (0, b * nb + i)

    def prev_t(b, i):
        return (0, b * nb + jnp.maximum(i - 1, 0))

    return pl.pallas_call(
        _swa_attn_kernel,
        name="swa_attn",
        grid=(batch, nb),
        in_specs=[pl.BlockSpec(memory_space=pltpu.SMEM),
                  pl.BlockSpec((bq, nq), cur),
                  pl.BlockSpec((bq, nk2), prev),
                  pl.BlockSpec((bq, nk2), cur),
                  pl.BlockSpec((nvt, bq), prev_t),
                  pl.BlockSpec((nvt, bq), cur_t)],
        out_specs=pl.BlockSpec((nq, bq), cur_t),
        out_shape=jax.ShapeDtypeStruct((nq, t), BF16),
        compiler_params=_params(("parallel", "parallel")),
    )(sinks.astype(F32), q, k2, k2, v_t, v_t)


def _proj_ln_kernel(at_ref, w_ref, res_ref, g_ref, b_ref, o_ref, *, alpha):
    mix = lax.dot_general(at_ref[...], w_ref[...], (((0,), (0,)), ((), ())), preferred_element_type=F32)
    o_ref[...] = _layer_norm(alpha * res_ref[...] + mix, g_ref[...], b_ref[...])


def _proj_ln(a_t, w, res, g, b, alpha):
    k, t = a_t.shape
    d = w.shape[1]
    tm = min(t, 512)
    return pl.pallas_call(
        functools.partial(_proj_ln_kernel, alpha=alpha),
        name="proj_ln",
        grid=(t // tm,),
        in_specs=[pl.BlockSpec((k, tm), lambda i: (0, i)),
                  pl.BlockSpec((k, d), lambda i: (0, 0)),
                  pl.BlockSpec((tm, d), lambda i: (i, 0)),
                  pl.BlockSpec((1, d), lambda i: (0, 0)),
                  pl.BlockSpec((1, d), lambda i: (0, 0))],
        out_specs=pl.BlockSpec((tm, d), lambda i: (i, 0)),
        out_shape=jax.ShapeDtypeStruct((t, d), F32),
        compiler_params=_params(("parallel",)),
    )(a_t, w.astype(BF16), res, g[None, :], b[None, :])


def _ffn_dense_kernel(x_ref, w1_ref, w3_ref, w2_ref, g_ref, b_ref, o_ref, xb_ref, acc_ref, *, alpha):
    f = pl.program_id(1)

    @pl.when(f == 0)
    def _():
        xb_ref[...] = x_ref[...].astype(BF16)
        acc_ref[...] = jnp.zeros_like(acc_ref)

    xb = xb_ref[...]
    gate = jnp.dot(xb, w1_ref[...], preferred_element_type=F32)
    up = jnp.dot(xb, w3_ref[...], preferred_element_type=F32)
    h = (jax.nn.silu(gate) * up).astype(BF16)
    acc_ref[...] += jnp.dot(h, w2_ref[...], preferred_element_type=F32)

    @pl.when(f == pl.num_programs(1) - 1)
    def _():
        o_ref[...] = _layer_norm(alpha * x_ref[...] + acc_ref[...], g_ref[...], b_ref[...])


def _ffn_tile(f_dim, largest=512):
    for tf in (largest, 512, 256, 128):
        if f_dim % tf == 0:
            return tf
    return f_dim


def _ffn_dense(x2, w1, w3, w2, g, b, alpha):
    t, d = x2.shape
    f_dim = w1.shape[1]
    tm = min(t, 512)
    tf = _ffn_tile(f_dim)
    return pl.pallas_call(
        functools.partial(_ffn_dense_kernel, alpha=alpha),
        name="ffn_dense",
        grid=(t // tm, f_dim // tf),
        in_specs=[pl.BlockSpec((tm, d), lambda i, f: (i, 0)),
                  pl.BlockSpec((d, tf), lambda i, f: (0, f)),
                  pl.BlockSpec((d, tf), lambda i, f: (0, f)),
                  pl.BlockSpec((tf, d), lambda i, f: (f, 0)),
                  pl.BlockSpec((1, d), lambda i, f: (0, 0)),
                  pl.BlockSpec((1, d), lambda i, f: (0, 0))],
        out_specs=pl.BlockSpec((tm, d), lambda i, f: (i, 0)),
        out_shape=jax.ShapeDtypeStruct((t, d), F32),
        scratch_shapes=[pltpu.VMEM((tm, d), BF16), pltpu.VMEM((tm, d), F32)],
        compiler_params=_params(("parallel", "arbitrary")),
    )(x2, w1.astype(BF16), w3.astype(BF16), w2.astype(BF16), g[None, :], b[None, :])


def _ple_kernel(x_ref, p_ref, wg_ref, wp_ref, o_ref):
    x = x_ref[...]
    gate = jax.nn.sigmoid(jnp.dot(x.astype(BF16), wg_ref[...], preferred_element_type=F32))
    emb = jnp.dot(p_ref[...].astype(BF16), wp_ref[...], preferred_element_type=F32)
    o_ref[...] = x + gate * emb


def _ple(x2, p2, w_g, w_p):
    t, d = x2.shape
    pd = p2.shape[1]
    tm = min(t, 512)
    return pl.pallas_call(
        _ple_kernel,
        name="ple",
        grid=(t // tm,),
        in_specs=[pl.BlockSpec((tm, d), lambda i: (i, 0)),
                  pl.BlockSpec((tm, pd), lambda i: (i, 0)),
                  pl.BlockSpec((d, d), lambda i: (0, 0)),
                  pl.BlockSpec((pd, d), lambda i: (0, 0))],
        out_specs=pl.BlockSpec((tm, d), lambda i: (i, 0)),
        out_shape=jax.ShapeDtypeStruct((t, d), F32),
        compiler_params=_params(("parallel",)),
    )(x2, p2, w_g.astype(BF16), w_p.astype(BF16))


def _dsa_q_kernel(x_ref, w_ref, cos_ref, sin_ref, q_ref):
    xb = x_ref[...].astype(BF16)
    acc = jnp.dot(xb, w_ref[...], preferred_element_type=F32)
    cs = cos_ref[:, :LANES]
    sn = sin_ref[:, :LANES]
    scale = MLA_SCALE * LOG2_E
    for h in range(MLA_HEADS):
        base = h * MLA_QK_PAD
        q_ref[:, base:base + LANES] = (acc[:, base:base + LANES] * scale).astype(BF16)
        a = acc[:, base + LANES:base + 2 * LANES]
        q_ref[:, base + LANES:base + 2 * LANES] = (_rope_lanes(a, cs, sn, 32) * scale).astype(BF16)


def _dsa_q(x2, w_q, cos_t, sin_t):
    t, d = x2.shape
    n = w_q.shape[1]
    tm = min(t, 512)
    return pl.pallas_call(
        _dsa_q_kernel,
        name="dsa_q",
        grid=(t // tm,),
        in_specs=[pl.BlockSpec((tm, d), lambda i: (i, 0)),
                  pl.BlockSpec((d, n), lambda i: (0, 0)),
                  pl.BlockSpec((tm, 2 * LANES), lambda i: (i, 0)),
                  pl.BlockSpec((tm, 2 * LANES), lambda i: (i, 0))],
        out_specs=pl.BlockSpec((tm, n), lambda i: (i, 0)),
        out_shape=jax.ShapeDtypeStruct((t, n), BF16),
        compiler_params=_params(("parallel",)),
    )(x2, w_q, cos_t, sin_t)


def _dsa_small_kernel(x_ref, w_ref, cos_ref, sin_ref, kvn_ref, ckv_ref, kr_ref, ki_ref, wi_ref):
    xb = x_ref[...].astype(BF16)
    acc = jnp.dot(xb, w_ref[...], preferred_element_type=F32)
    c = acc[:, :MLA_LATENT]
    ms = jnp.mean(c * c, axis=-1, keepdims=True)
    ckv_ref[...] = (c * lax.rsqrt(ms + RMS_EPS) * kvn_ref[...]).astype(BF16)
    o = MLA_LATENT
    kr_ref[...] = _rope_lanes(acc[:, o:o + LANES], cos_ref[:, :LANES], sin_ref[:, :LANES], 32).astype(BF16)
    ki_ref[...] = _rope_lanes(acc[:, o + LANES:o + 2 * LANES], cos_ref[:, LANES:], sin_ref[:, LANES:],
                              IDX_ROPE // 2).astype(BF16)
    wi_ref[...] = acc[:, o + 2 * LANES:o + 3 * LANES] * (IDX_HEADS ** -0.5)


def _dsa_small(x2, w_small, kv_norm, cos_t, sin_t):
    t, d = x2.shape
    n = w_small.shape[1]
    tm = min(t, 512)
    return pl.pallas_call(
        _dsa_small_kernel,
        name="dsa_small",
        grid=(t // tm,),
        in_specs=[pl.BlockSpec((tm, d), lambda i: (i, 0)),
                  pl.BlockSpec((d, n), lambda i: (0, 0)),
                  pl.BlockSpec((tm, 2 * LANES), lambda i: (i, 0)),
                  pl.BlockSpec((tm, 2 * LANES), lambda i: (i, 0)),
                  pl.BlockSpec((1, MLA_LATENT), lambda i: (0, 0))],
        out_specs=[pl.BlockSpec((tm, MLA_LATENT), lambda i: (i, 0)),
                   pl.BlockSpec((tm, LANES), lambda i: (i, 0)),
                   pl.BlockSpec((tm, LANES), lambda i: (i, 0)),
                   pl.BlockSpec((tm, LANES), lambda i: (i, 0))],
        out_shape=[jax.ShapeDtypeStruct((t, MLA_LATENT), BF16),
                   jax.ShapeDtypeStruct((t, LANES), BF16),
                   jax.ShapeDtypeStruct((t, LANES), BF16),
                   jax.ShapeDtypeStruct((t, LANES), F32)],
        compiler_params=_params(("parallel",)),
    )(x2, w_small, cos_t, sin_t, kv_norm[None, :].astype(F32))


def _dsa_qi_kernel(x_ref, w_ref, cos_ref, sin_ref, q_ref):
    xb = x_ref[...].astype(BF16)
    acc = jnp.dot(xb, w_ref[...], preferred_element_type=F32)
    cs = cos_ref[:, LANES:]
    sn = sin_ref[:, LANES:]
    scale = IDX_DIM ** -0.5
    for c in range(acc.shape[1] // LANES):
        a = acc[:, c * LANES:(c + 1) * LANES]
        q_ref[:, c * LANES:(c + 1) * LANES] = (_rope_lanes(a, cs, sn, IDX_ROPE // 2) * scale).astype(BF16)


def _dsa_qi(x2, w_qi, cos_t, sin_t):
    t, d = x2.shape
    n = w_qi.shape[1]
    tm = min(t, 512)
    return pl.pallas_call(
        _dsa_qi_kernel,
        name="dsa_qi",
        grid=(t // tm,),
        in_specs=[pl.BlockSpec((tm, d), lambda i: (i, 0)),
                  pl.BlockSpec((d, n), lambda i: (0, 0)),
                  pl.BlockSpec((tm, 2 * LANES), lambda i: (i, 0)),
                  pl.BlockSpec((tm, 2 * LANES), lambda i: (i, 0))],
        out_specs=pl.BlockSpec((tm, n), lambda i: (i, 0)),
        out_shape=jax.ShapeDtypeStruct((t, n), BF16),
        compiler_params=_params(("parallel",)),
    )(x2, w_qi, cos_t, sin_t)


def _dsa_kv_kernel(c_ref, kr_ref, wk_ref, wvt_ref, k_ref, vt_ref):
    c = c_ref[...]
    kn = jnp.dot(c, wk_ref[...], preferred_element_type=F32)
    kr = kr_ref[...]
    for h in range(MLA_HEADS):
        base = h * MLA_QK_PAD
        k_ref[:, base:base + LANES] = kn[:, h * LANES:(h + 1) * LANES].astype(BF16)
        k_ref[:, base + LANES:base + 2 * LANES] = kr
    vt = lax.dot_general(wvt_ref[...], c, (((1,), (1,)), ((), ())), preferred_element_type=F32)
    ones = jnp.ones((MLA_V_PAD - MLA_V, c.shape[0]), BF16)
    for h in range(MLA_HEADS):
        vt_ref[h * MLA_V_PAD:h * MLA_V_PAD + MLA_V, :] = vt[h * MLA_V:(h + 1) * MLA_V, :].astype(BF16)
        vt_ref[h * MLA_V_PAD + MLA_V:(h + 1) * MLA_V_PAD, :] = ones


def _dsa_kv(ckv, kr, w_uk_all, w_uv_t):
    t, c_dim = ckv.shape
    nk = MLA_HEADS * MLA_QK_PAD
    nv = w_uv_t.shape[0]
    nv_pad = MLA_HEADS * MLA_V_PAD
    tm = min(t, 512)
    return pl.pallas_call(
        _dsa_kv_kernel,
        name="dsa_kv",
        grid=(t // tm,),
        in_specs=[pl.BlockSpec((tm, c_dim), lambda i: (i, 0)),
                  pl.BlockSpec((tm, LANES), lambda i: (i, 0)),
                  pl.BlockSpec((c_dim, MLA_HEADS * MLA_NOPE), lambda i: (0, 0)),
                  pl.BlockSpec((nv, c_dim), lambda i: (0, 0))],
        out_specs=[pl.BlockSpec((tm, nk), lambda i: (i, 0)),
                   pl.BlockSpec((nv_pad, tm), lambda i: (0, i))],
        out_shape=[jax.ShapeDtypeStruct((t, nk), BF16),
                   jax.ShapeDtypeStruct((nv_pad, t), BF16)],
        compiler_params=_params(("parallel",)),
    )(ckv, kr, w_uk_all, w_uv_t)


def _float_sort_key(s):
    bits = lax.bitcast_convert_type(s, I32)
    return bits ^ ((bits >> 31) & jnp.int32(0x7FFFFFFF))


def _dsa_select_kernel(k_ref, q_ref, w_ref, bias_ref, key_ref, tie_ref, *, sc, cc, unroll, n_sel):
    j = pl.program_id(1)
    s_len, tq = key_ref.shape
    n_chunks = s_len // sc
    q_lo = j * tq
    n_need = (q_lo + tq + sc * unroll - 1) // (sc * unroll) * unroll
    qt = q_ref[...]
    w_row = w_ref[...]
    q_pos = q_lo + lax.broadcasted_iota(I32, (sc, tq), 1)
    k_off = lax.broadcasted_iota(I32, (sc, tq), 0)
    min_key = jnp.int32(-2 ** 31)

    def score_chunks(i, carry):
        for u in range(unroll):
            row0 = pl.multiple_of((i * unroll + u) * sc, sc)
            kc = k_ref[pl.ds(row0, sc), :]
            d = jnp.dot(kc, qt, preferred_element_type=F32)
            s = jnp.zeros((sc, tq), F32)
            for h in range(IDX_HEADS):
                s = s + jnp.maximum(d[:, h * tq:(h + 1) * tq], 0.0) * w_row[:, h * tq:(h + 1) * tq]
            causal = (row0 + k_off) <= q_pos
            key_ref[pl.ds(row0, sc), :] = jnp.where(causal, _float_sort_key(s), min_key)
        return carry

    lax.fori_loop(0, n_need // unroll, score_chunks, 0)

    n_cnt = (n_need * sc + cc - 1) // cc

    def fill_chunk(c, carry):
        row0 = pl.multiple_of(c * sc, sc)
        key_ref[pl.ds(row0, sc), :] = jnp.full((sc, tq), min_key, I32)
        return carry

    lax.fori_loop(n_need, n_cnt * (cc // sc), fill_chunk, 0)

    def count_ge(cand):
        def body(c, cnt):
            row0 = pl.multiple_of(c * cc, cc)
            hit = jnp.where(key_ref[pl.ds(row0, cc), :] >= cand, 1, 0).astype(I32)
            return cnt + jnp.sum(hit.reshape(cc // 32, 32, tq), axis=0)

        cnt = lax.fori_loop(0, n_cnt, body, jnp.zeros((32, tq), I32))
        return jnp.sum(cnt, axis=0, keepdims=True)

    zero = jnp.zeros((1, tq), I32)
    c0 = count_ge(zero)
    thr = jnp.where(c0 >= n_sel, zero, jnp.full((1, tq), min_key, I32))
    cnt_thr = jnp.where(c0 >= n_sel, c0, 0)

    def bit_step(it, carry):
        thr, cnt_thr = carry
        cand = thr + (jnp.int32(1) << (30 - it))
        cnt = count_ge(cand)
        take = cnt >= n_sel
        return jnp.where(take, cand, thr), jnp.where(take, cnt, cnt_thr)

    thr, cnt_thr = lax.fori_loop(0, 31, bit_step, (thr, cnt_thr))

    excess = cnt_thr - n_sel
    tie_ref[0:1, :] = jnp.full((1, tq), s_len, I32)

    @pl.when(jnp.max(excess) > 0)
    def _():
        def count_tie_below(x):
            def body(c, cnt):
                row0 = pl.multiple_of(c * cc, cc)
                idx = row0 + lax.broadcasted_iota(I32, (cc, tq), 0)
                hit = jnp.where((key_ref[pl.ds(row0, cc), :] == thr) & (idx < x), 1, 0).astype(I32)
                return cnt + jnp.sum(hit.reshape(cc // 32, 32, tq), axis=0)

            cnt = lax.fori_loop(0, n_cnt, body, jnp.zeros((32, tq), I32))
            return jnp.sum(cnt, axis=0, keepdims=True)

        n_tie = count_tie_below(jnp.full((1, tq), s_len, I32))
        need = n_sel - (cnt_thr - n_tie)
        n_bits = s_len.bit_length()

        def idx_step(it, x):
            cand = x + (jnp.int32(1) << (n_bits - 1 - it))
            return jnp.where(count_tie_below(cand) <= need, cand, x)

        x = lax.fori_loop(0, n_bits, idx_step, zero)
        tie_ref[0:1, :] = jnp.where(excess > 0, x, s_len)

    tie_below = tie_ref[0:1, :]
    thr_low = jnp.maximum(thr, min_key + 1)

    def write_chunk(c, carry):
        row0 = pl.multiple_of(c * sc, sc)
        thr_here = thr_low + jnp.where(k_off >= tie_below - row0, 1, 0)
        keep = key_ref[pl.ds(row0, sc), :] >= thr_here
        bias_ref[pl.ds(row0, sc), :] = jnp.where(keep, 0.0, NEG_BIG).astype(F32)
        return carry

    def mask_chunk(c, carry):
        row0 = pl.multiple_of(c * sc, sc)
        bias_ref[pl.ds(row0, sc), :] = jnp.full((sc, tq), NEG_BIG, F32)
        return carry

    lax.fori_loop(0, n_need, write_chunk, 0)
    lax.fori_loop(n_need, n_chunks, mask_chunk, 0)


def _dsa_select(k_i, q_i_t, w_rows, batch, seq, n_sel):
    tq = LANES
    nq = seq // tq
    sc = min(seq, 256)
    cc = min(seq, 1024)
    return pl.pallas_call(
        functools.partial(_dsa_select_kernel, sc=sc, cc=cc, unroll=2 if (seq // sc) % 2 == 0 else 1, n_sel=n_sel),
        name="dsa_select",
        grid=(batch, nq),
        in_specs=[pl.BlockSpec((None, seq, IDX_DIM), lambda b, j: (b, 0, 0)),
                  pl.BlockSpec((None, None, IDX_DIM, IDX_HEADS * tq), lambda b, j: (b, j, 0, 0)),
                  pl.BlockSpec((None, None, 1, IDX_HEADS * tq), lambda b, j: (b, j, 0, 0))],
        out_specs=pl.BlockSpec((None, seq, tq), lambda b, j: (b, 0, j)),
        out_shape=jax.ShapeDtypeStruct((batch, seq, seq), F32),
        scratch_shapes=[pltpu.VMEM((seq, tq), I32), pltpu.VMEM((8, tq), I32)],
        compiler_params=_params(("parallel", "parallel")),
    )(k_i, q_i_t, w_rows)


def _dsa_attn_kernel(qt_ref, k_ref, vta_ref, vtb_ref, bias_ref, o_ref, m_ref, a_ref, *head_refs, sc):
    s_refs = head_refs[:MLA_HEADS]
    acc_refs = head_refs[MLA_HEADS:]
    j = pl.program_id(1)
    g = pl.program_id(2)
    tq = qt_ref.shape[1]
    last_pair = (j * tq + tq - 1) // (2 * sc)

    @pl.when(g == 0)
    def _():
        for h in range(MLA_HEADS):
            acc_refs[h][...] = jnp.zeros_like(acc_refs[h])
            s_refs[h][1] = jnp.full((sc, tq), NEG_BIG, F32)
        m_ref[...] = jnp.full_like(m_ref, M_INIT)
        a_ref[...] = jnp.ones_like(a_ref)

    def finish(slot, vt_ref, m_now, a_now):
        for h in range(MLA_HEADS):
            p = jnp.exp2(s_refs[h][slot] - m_now[h:h + 1, :]).astype(BF16)
            pv = jnp.dot(vt_ref[h * MLA_V_PAD:(h + 1) * MLA_V_PAD, :], p, preferred_element_type=F32)
            acc_refs[h][...] = a_now[h:h + 1, :] * acc_refs[h][...] + pv

    def score(slot, half, m_now):
        rows = pl.ds(half * sc, sc)
        chunk_max = []
        for h in range(MLA_HEADS):
            qh = qt_ref[h * MLA_QK_PAD:(h + 1) * MLA_QK_PAD, :]
            kh = k_ref[rows, h * MLA_QK_PAD:(h + 1) * MLA_QK_PAD]
            s = jnp.dot(kh, qh, preferred_element_type=F32) + bias_ref[rows, :]
            s_refs[h][slot] = s
            chunk_max.append(jnp.max(s, axis=0, keepdims=True))
        m_new = jnp.maximum(m_now, jnp.concatenate(chunk_max, axis=0))
        return m_new, jnp.exp2(m_now - m_new)

    @pl.when(g <= last_pair)
    def _():
        m0 = m_ref[...]
        finish(1, vta_ref, m0, a_ref[...])
        m1, a1 = score(0, 0, m0)
        finish(0, vtb_ref, m1, a1)
        m2, a2 = score(1, 1, m1)
        m_ref[...] = m2
        a_ref[...] = a2

    @pl.when(g == last_pair + 1)
    def _():
        finish(1, vta_ref, m_ref[...], a_ref[...])
        for h in range(MLA_HEADS):
            inv = 1.0 / acc_refs[h][MLA_V:MLA_V + 1, :]
            o_ref[h * MLA_V:(h + 1) * MLA_V, :] = (acc_refs[h][:MLA_V, :] * inv).astype(BF16)


def _dsa_attn(q_cat_t, k_cat, v_t, bias, batch, seq):
    nqk, t = q_cat_t.shape
    nv_pad = v_t.shape[0]
    nv = MLA_HEADS * MLA_V
    tq = min(seq, 512)
    sc = min(256, tq // 2)
    nq = seq // tq
    nc = seq // sc
    npair = nc // 2

    def last_pair(j):
        return (j * tq + tq - 1) // (2 * sc)

    return pl.pallas_call(
        functools.partial(_dsa_attn_kernel, sc=sc),
        name="dsa_attn",
        grid=(batch, nq, npair + 1),
        in_specs=[pl.BlockSpec((nqk, tq), lambda b, j, g: (0, b * nq + j)),
                  pl.BlockSpec((2 * sc, nqk), lambda b, j, g: (b * npair + jnp.minimum(g, last_pair(j)), 0)),
                  pl.BlockSpec((nv_pad, sc), lambda b, j, g: (0, b * nc + jnp.clip(2 * g - 1, 0, 2 * last_pair(j) + 1))),
                  pl.BlockSpec((nv_pad, sc), lambda b, j, g: (0, b * nc + jnp.minimum(2 * g, 2 * last_pair(j)))),
                  pl.BlockSpec((None, 2 * sc, tq), lambda b, j, g: (b, jnp.minimum(g, last_pair(j)), j))],
        out_specs=pl.BlockSpec((nv, tq), lambda b, j, g: (0, b * nq + j)),
        out_shape=jax.ShapeDtypeStruct((nv, t), BF16),
        scratch_shapes=([pltpu.VMEM((MLA_HEADS, tq), F32), pltpu.VMEM((MLA_HEADS, tq), F32)]
                        + [pltpu.VMEM((2, sc, tq), F32)] * MLA_HEADS
                        + [pltpu.VMEM((MLA_V_PAD, tq), F32)] * MLA_HEADS),
        compiler_params=_params(("parallel", "parallel", "arbitrary")),
    )(q_cat_t, k_cat, v_t, v_t, bias)


def _dsa_mixer_out(x2, cos_t, sin_t, w_in, kv_norm, w_uk, w_uv, batch, seq):
    t, d = x2.shape
    nq = MLA_HEADS * (MLA_NOPE + MLA_ROPE)
    o_c = nq
    o_kr = o_c + MLA_LATENT
    o_qi = o_kr + MLA_ROPE
    o_ki = o_qi + IDX_HEADS * IDX_DIM
    o_wi = o_ki + IDX_DIM
    wb = w_in.astype(BF16)
    wq = wb[:, :nq].reshape(d, MLA_HEADS, MLA_NOPE + MLA_ROPE)
    wq = jnp.pad(wq, ((0, 0), (0, 0), (0, MLA_QK_PAD - MLA_NOPE - MLA_ROPE))).reshape(d, MLA_HEADS * MLA_QK_PAD)
    pad_to_lanes = lambda w: jnp.pad(w, ((0, 0), (0, LANES - w.shape[1])))
    w_small = jnp.concatenate([wb[:, o_c:o_kr], pad_to_lanes(wb[:, o_kr:o_qi]), pad_to_lanes(wb[:, o_ki:o_wi]),
                               pad_to_lanes(wb[:, o_wi:])], axis=1)
    w_qi = wb[:, o_qi:o_ki]
    w_uk_all = jnp.transpose(w_uk, (1, 0, 2)).reshape(MLA_LATENT, MLA_HEADS * MLA_NOPE).astype(BF16)
    w_uv_t = jnp.transpose(w_uv, (0, 2, 1)).reshape(MLA_HEADS * MLA_V, MLA_LATENT).astype(BF16)

    q_cat = _dsa_q(x2, wq, cos_t, sin_t)
    ckv, kr, ki, wi = _dsa_small(x2, w_small, kv_norm, cos_t, sin_t)
    qi = _dsa_qi(x2, w_qi, cos_t, sin_t)
    k_cat, v_t = _dsa_kv(ckv, kr, w_uk_all, w_uv_t)

    tq = LANES
    nqb = seq // tq
    k_i = ki[:, :IDX_DIM].reshape(batch, seq, IDX_DIM)
    q_i_t = jnp.transpose(qi.reshape(batch, nqb, tq, IDX_HEADS, IDX_DIM), (0, 1, 4, 3, 2))
    q_i_t = q_i_t.reshape(batch, nqb, IDX_DIM, IDX_HEADS * tq)
    w_rows = jnp.transpose(wi[:, :IDX_HEADS].reshape(batch, nqb, tq, IDX_HEADS), (0, 1, 3, 2))
    w_rows = w_rows.reshape(batch, nqb, 1, IDX_HEADS * tq)
    n_sel = min(DSA_TOPK, seq // 4)
    bias = _dsa_select(k_i, q_i_t, w_rows, batch, seq, n_sel)
    o_t = _dsa_attn(jnp.transpose(q_cat), k_cat, v_t, bias, batch, seq)
    return o_t


def _router_kernel(x_ref, w_ref, idx_ref, gate_ref):
    logits = jnp.dot(x_ref[...], w_ref[...], precision=lax.Precision.HIGHEST, preferred_element_type=F32)
    lane = lax.broadcasted_iota(I32, logits.shape, 1)
    lg = jnp.where(lane < MOE_EXPERTS, logits, -jnp.inf)
    v1 = jnp.max(lg, axis=-1, keepdims=True)
    i1 = jnp.min(jnp.where(lg == v1, lane, LANES), axis=-1, keepdims=True)
    lg2 = jnp.where(lane == i1, -jnp.inf, lg)
    v2 = jnp.max(lg2, axis=-1, keepdims=True)
    i2 = jnp.min(jnp.where(lg2 == v2, lane, LANES), axis=-1, keepdims=True)
    e2 = jnp.exp(v2 - v1)
    den = 1.0 + e2
    idx_ref[...] = jnp.where(lane == 0, i1, jnp.where(lane == 1, i2, 0))
    gate_ref[...] = jnp.where(lane == 0, 1.0 / den, jnp.where(lane == 1, e2 / den, 0.0))


def _router(x2, w_router):
    t, d = x2.shape
    wr = jnp.pad(w_router.astype(F32), ((0, 0), (0, LANES - w_router.shape[1])))
    tm = min(t, 512)
    return pl.pallas_call(
        _router_kernel,
        name="moe_router",
        grid=(t // tm,),
        in_specs=[pl.BlockSpec((tm, d), lambda i: (i, 0)),
                  pl.BlockSpec((d, LANES), lambda i: (0, 0))],
        out_specs=[pl.BlockSpec((tm, LANES), lambda i: (i, 0)),
                   pl.BlockSpec((tm, LANES), lambda i: (i, 0))],
        out_shape=[jax.ShapeDtypeStruct((t, LANES), I32),
                   jax.ShapeDtypeStruct((t, LANES), F32)],
        compiler_params=_params(("parallel",)),
    )(x2, wr)


def _row_copy(src_hbm, src_row, dst, dst_row, sem):
    return pltpu.make_async_copy(src_hbm.at[pl.ds(src_row, 1)], dst.at[pl.ds(dst_row, 1)], sem)


def _moe_ffn_kernel(be_ref, nu_ref, tok_ref, tok_next_ref, x_hbm, w1_ref, w3_ref, w2_ref, o_ref,
                    xg_ref, xb_ref, acc_ref, sem):
    i = pl.program_id(0)
    f = pl.program_id(1)
    nf = pl.num_programs(1)
    n_used = nu_ref[0]
    used = i < n_used
    tm = xb_ref.shape[0]
    slot = i % 2

    def start_gather(ids_ref, buf):
        def body(r, carry):
            _row_copy(x_hbm, ids_ref[r], xg_ref.at[buf], r, sem.at[buf]).start()
            return carry

        lax.fori_loop(0, tm, body, 0, unroll=8)

    @pl.when(used & (f == 0))
    def _():
        @pl.when(i == 0)
        def _():
            start_gather(tok_ref, 0)

        pltpu.make_async_copy(x_hbm.at[pl.ds(0, tm)], xg_ref.at[slot], sem.at[slot]).wait()

        @pl.when(i + 1 < n_used)
        def _():
            start_gather(tok_next_ref, 1 - slot)

        xb_ref[...] = xg_ref[slot].astype(BF16)
        acc_ref[...] = jnp.zeros_like(acc_ref)

    @pl.when(used)
    def _():
        xb = xb_ref[...]
        gate = jnp.dot(xb, w1_ref[...], preferred_element_type=F32)
        up = jnp.dot(xb, w3_ref[...], preferred_element_type=F32)
        h = (jax.nn.silu(gate) * up).astype(BF16)
        acc_ref[...] += jnp.dot(h, w2_ref[...], preferred_element_type=F32)

    @pl.when(used & (f == nf - 1))
    def _():
        o_ref[...] = acc_ref[...]

    @pl.when(jnp.logical_not(used) & (f == nf - 1))
    def _():
        o_ref[...] = jnp.zeros_like(o_ref)


def _moe_ffn(x2, slot_tok, w1, w3, w2, blk_expert, n_used, tm):
    n_slots = slot_tok.shape[0]
    d = x2.shape[1]
    f_dim = w1.shape[2]
    tf = _ffn_tile(f_dim, largest=1024)
    nf = f_dim // tf
    nt = n_slots // tm

    def f_sel(i, f, nu):
        return jnp.where(i < nu[0], f, nf - 1)

    grid_spec = pltpu.PrefetchScalarGridSpec(
        num_scalar_prefetch=2,
        grid=(nt, nf),
        in_specs=[pl.BlockSpec((tm,), lambda i, f, be, nu: (i,), memory_space=pltpu.SMEM),
                  pl.BlockSpec((tm,), lambda i, f, be, nu: (jnp.minimum(i + 1, nt - 1),), memory_space=pltpu.SMEM),
                  pl.BlockSpec(memory_space=pl.ANY),
                  pl.BlockSpec((None, d, tf), lambda i, f, be, nu: (be[i], 0, f_sel(i, f, nu))),
                  pl.BlockSpec((None, d, tf), lambda i, f, be, nu: (be[i], 0, f_sel(i, f, nu))),
                  pl.BlockSpec((None, tf, d), lambda i, f, be, nu: (be[i], f_sel(i, f, nu), 0))],
        out_specs=pl.BlockSpec((tm, d), lambda i, f, be, nu: (i, 0)),
        scratch_shapes=[pltpu.VMEM((2, tm, d), F32), pltpu.VMEM((tm, d), BF16), pltpu.VMEM((tm, d), F32),
                        pltpu.SemaphoreType.DMA((2,))],
    )
    return pl.pallas_call(
        _moe_ffn_kernel,
        name="moe_ffn",
        grid_spec=grid_spec,
        out_shape=jax.ShapeDtypeStruct((n_slots, d), F32),
        compiler_params=_params(("arbitrary", "arbitrary")),
    )(blk_expert, n_used, slot_tok, slot_tok, x2, w1, w3, w2)


def _moe_combine_kernel(dest_ref, dest_next_ref, y_hbm, x_ref, gate_ref, g_ref, b_ref, o_ref, y_ref, sem, *,
                        alpha, tb):
    i = pl.program_id(0)
    slot = i % 2

    def start_gather(ids_ref, buf):
        def body(tok, carry):
            for k in range(MOE_TOP_K):
                _row_copy(y_hbm, ids_ref[MOE_TOP_K * tok + k], y_ref.at[buf, k], tok, sem.at[buf]).start()
            return carry

        lax.fori_loop(0, tb, body, 0, unroll=4)

    @pl.when(i == 0)
    def _():
        start_gather(dest_ref, 0)

    for k in range(MOE_TOP_K):
        pltpu.make_async_copy(y_hbm.at[pl.ds(0, tb)], y_ref.at[slot, k], sem.at[slot]).wait()

    @pl.when(i + 1 < pl.num_programs(0))
    def _():
        start_gather(dest_next_ref, 1 - slot)

    gates = gate_ref[...]
    ff = gates[:, 0:1] * y_ref[slot, 0] + gates[:, 1:2] * y_ref[slot, 1]
    o_ref[...] = _layer_norm(alpha * x_ref[...] + ff, g_ref[...], b_ref[...])


def _moe_combine(x2, y, dest, gates, g, b, alpha):
    t, d = x2.shape
    tb = min(t, 256)
    nt = t // tb
    return pl.pallas_call(
        functools.partial(_moe_combine_kernel, alpha=alpha, tb=tb),
        name="moe_combine",
        grid=(nt,),
        in_specs=[pl.BlockSpec((tb * MOE_TOP_K,), lambda i: (i,), memory_space=pltpu.SMEM),
                  pl.BlockSpec((tb * MOE_TOP_K,), lambda i: (jnp.minimum(i + 1, nt - 1),), memory_space=pltpu.SMEM),
                  pl.BlockSpec(memory_space=pl.ANY),
                  pl.BlockSpec((tb, d), lambda i: (i, 0)),
                  pl.BlockSpec((tb, LANES), lambda i: (i, 0)),
                  pl.BlockSpec((1, d), lambda i: (0, 0)),
                  pl.BlockSpec((1, d), lambda i: (0, 0))],
        out_specs=pl.BlockSpec((tb, d), lambda i: (i, 0)),
        out_shape=jax.ShapeDtypeStruct((t, d), F32),
        scratch_shapes=[pltpu.VMEM((2, MOE_TOP_K, tb, d), F32), pltpu.SemaphoreType.DMA((2,))],
        compiler_params=_params(("arbitrary",)),
    )(dest, dest, y, x2, gates, g[None, :], b[None, :])


def _moe_layer(x2, w_router, w1, w3, w2, g, b, alpha):
    t, d = x2.shape
    n = t * MOE_TOP_K
    tm = min(512, t)
    idx_pad, gates = _router(x2, w_router)
    e_flat = idx_pad[:, :MOE_TOP_K].reshape(n)
    flat_ids = jnp.arange(n, dtype=I32)
    _, order = lax.sort((e_flat, flat_ids), num_keys=1, is_stable=True)
    _, inv_order = lax.sort((order, flat_ids), num_keys=1)
    experts = jnp.arange(MOE_EXPERTS, dtype=I32)
    counts = jnp.sum((e_flat[:, None] == experts[None, :]).astype(I32), axis=0)
    starts = jnp.cumsum(counts) - counts
    padded = (counts + tm - 1) // tm * tm
    pends = jnp.cumsum(padded)
    pstarts = pends - padded
    n_tiles = (n + MOE_EXPERTS * (tm - 1) + tm - 1) // tm
    n_used = (pends[-1] // tm).astype(I32)
    blk_start = jnp.arange(n_tiles, dtype=I32) * tm
    blk_expert = jnp.minimum(jnp.sum((pends[None, :] <= blk_start[:, None]).astype(I32), axis=1), MOE_EXPERTS - 1)
    blk_expert = jnp.where(jnp.arange(n_tiles) < n_used, blk_expert, blk_expert[jnp.maximum(n_used - 1, 0)])
    n_slots = n_tiles * tm
    slot = jnp.arange(n_slots, dtype=I32)
    slot_e = jnp.repeat(blk_expert, tm)
    shift = pstarts - starts
    order_tok = jnp.pad(order // MOE_TOP_K, (0, n_slots - n))
    slot_tok = jnp.zeros((n_slots,), I32)
    for e in range(MOE_EXPERTS):
        in_group = (slot_e == e) & (slot - pstarts[e] < counts[e]) & (slot < n_used * tm)
        slot_tok = jnp.where(in_group, jnp.roll(order_tok, shift[e]), slot_tok)
    dest = (inv_order + jnp.sum(jnp.where(e_flat[:, None] == experts[None, :], shift[None, :], 0), axis=1)).astype(I32)

    y = _moe_ffn(x2, slot_tok, w1.astype(BF16), w3.astype(BF16), w2.astype(BF16), blk_expert.astype(I32),
                 n_used.reshape(1), tm)
    return _moe_combine(x2, y, dest, gates, g, b, alpha)


def kernel(x, p, positions, swa_w_qkv, swa_sinks, swa_w_o, dense_w1, dense_w3, dense_w2, dsa_w_in, dsa_kv_norm,
           dsa_w_uk, dsa_w_uv, dsa_w_o, moe_router, moe_w1, moe_w3, moe_w2, ln_g, ln_b, ple_w_p, ple_w_g):
    batch, seq, d = x.shape
    depth = ln_g.shape[0]
    alpha = (2.0 * depth) ** 0.25
    t = batch * seq
    x2 = x.reshape(t, d)
    cos_t, sin_t = _rope_tables(positions)
    for i in range(depth):
        j = i // 2
        if i % 2 == 0:
            q, k2, v_t = _swa_qkv(x2, swa_w_qkv[j], cos_t, sin_t)
            o = _swa_attn(q, k2, v_t, swa_sinks[j], batch, seq)
            x2 = _proj_ln(o, swa_w_o[j], x2, ln_g[i, 0], ln_b[i, 0], alpha)
            x2 = _ffn_dense(x2, dense_w1[j], dense_w3[j], dense_w2[j], ln_g[i, 1], ln_b[i, 1], alpha)
        else:
            o = _dsa_mixer_out(x2, cos_t, sin_t, dsa_w_in[j], dsa_kv_norm[j], dsa_w_uk[j], dsa_w_uv[j], batch, seq)
            x2 = _proj_ln(o, dsa_w_o[j], x2, ln_g[i, 0], ln_b[i, 0], alpha)
            x2 = _moe_layer(x2, moe_router[j], moe_w1[j], moe_w3[j], moe_w2[j], ln_g[i, 1], ln_b[i, 1], alpha)
        x2 = _ple(x2, p[i].reshape(t, -1), ple_w_g[i], ple_w_p[i])
    return x2.reshape(batch, seq, d)
```

```python
import functools

import jax
import jax.numpy as jnp
from jax import lax
from jax.experimental import pallas as pl
from jax.experimental.pallas import tpu as pltpu

F32 = jnp.float32
BF16 = jnp.bfloat16
I32 = jnp.int32

LANES = 128
ROPE_THETA = 10000.0
LN_EPS = 1e-5
RMS_EPS = 1e-6
SWA_HEADS = 32
SWA_KV_HEADS = 4
SWA_GROUP = SWA_HEADS // SWA_KV_HEADS
SWA_HEAD_DIM = 64
SWA_WINDOW = 128
SWA_V_PAD = SWA_HEAD_DIM + 16
MLA_HEADS = 16
MLA_NOPE = 128
MLA_ROPE = 64
MLA_V = 128
MLA_LATENT = 512
MLA_SCALE = (MLA_NOPE + MLA_ROPE) ** -0.5
MLA_QK_PAD = 256
MLA_V_PAD = MLA_V + 16
LOG2_E = 1.4426950408889634
IDX_HEADS = 16
IDX_DIM = 64
IDX_ROPE = 32
DSA_TOPK = 256
MOE_EXPERTS = 8
MOE_TOP_K = 2
NEG_BIG = -1e30
M_INIT = -1e20
VMEM_LIMIT = 56 * 1024 * 1024
ROW_TILE = 512
ROPE_ROW_TILE = 1024
FFN_COL_TILE = 512
MOE_COL_TILE = 1024
COMBINE_ROW_TILE = 256
SELECT_SCORE_ROWS = 256
SELECT_COUNT_ROWS = 1024
ATTN_QUERY_TILE = 512
ATTN_KEY_CHUNK = 256
ROPE_HALF = 32


def _params(sem):
    return pltpu.CompilerParams(dimension_semantics=sem, vmem_limit_bytes=VMEM_LIMIT)


def _layer_norm(y, g, b):
    mu = jnp.mean(y, axis=-1, keepdims=True)
    d = y - mu
    var = jnp.mean(d * d, axis=-1, keepdims=True)
    return d * lax.rsqrt(var + LN_EPS) * g + b


def _rope_lanes(a, cs, sn, half):
    lane = lax.broadcasted_iota(I32, a.shape, 1)
    first = (lane % 64) < half
    rot = jnp.where(first, pltpu.roll(a, LANES - half, 1), pltpu.roll(a, half, 1))
    return a * cs + rot * sn


def _rope_table_kernel(pos_ref, invf_ref, sgn_ref, cos_ref, sin_ref):
    ang = pos_ref[...].astype(F32) * invf_ref[...]
    cos_ref[...] = jnp.cos(ang)
    sin_ref[...] = jnp.sin(ang) * sgn_ref[...]


def _rope_tables(positions):
    t = positions.size
    pos = positions.reshape(t, 1).astype(I32)
    lane = jnp.arange(LANES)
    in64 = lane % 64
    f64 = ROPE_THETA ** (-jnp.arange(0, 64, 2, dtype=F32) / 64)
    f32_ = ROPE_THETA ** (-jnp.arange(0, IDX_ROPE, 2, dtype=F32) / IDX_ROPE)
    invf_a = f64[in64 % 32]
    sgn_a = jnp.where(in64 < 32, -1.0, 1.0).astype(F32)
    invf_b = jnp.where(in64 < IDX_ROPE, f32_[in64 % (IDX_ROPE // 2)], 0.0)
    sgn_b = jnp.where(in64 < IDX_ROPE // 2, -1.0, 1.0).astype(F32)
    invf = jnp.concatenate([invf_a, invf_b])[None, :]
    sgn = jnp.concatenate([sgn_a, sgn_b])[None, :]
    tm = min(t, ROPE_ROW_TILE)
    return pl.pallas_call(
        _rope_table_kernel,
        name="rope_tables",
        grid=(t // tm,),
        in_specs=[pl.BlockSpec((tm, 1), lambda i: (i, 0)),
                  pl.BlockSpec((1, 2 * LANES), lambda i: (0, 0)),
                  pl.BlockSpec((1, 2 * LANES), lambda i: (0, 0))],
        out_specs=[pl.BlockSpec((tm, 2 * LANES), lambda i: (i, 0)),
                   pl.BlockSpec((tm, 2 * LANES), lambda i: (i, 0))],
        out_shape=[jax.ShapeDtypeStruct((t, 2 * LANES), F32)] * 2,
        compiler_params=_params(("parallel",)),
    )(pos, invf, sgn)


def _swa_qkv_kernel(x_ref, w_ref, wvt_ref, cos_ref, sin_ref, q_ref, k_ref, vt_ref, *, nq, nk):
    xb = x_ref[...].astype(BF16)
    acc = jnp.dot(xb, w_ref[...], preferred_element_type=F32)
    cs = cos_ref[:, :LANES]
    sn = sin_ref[:, :LANES]
    scale = SWA_HEAD_DIM ** -0.5 * LOG2_E
    for c in range(nq // LANES):
        a = acc[:, c * LANES:(c + 1) * LANES]
        q_ref[:, c * LANES:(c + 1) * LANES] = (_rope_lanes(a, cs, sn, ROPE_HALF) * scale).astype(BF16)
    low = lax.broadcasted_iota(I32, (acc.shape[0], LANES), 1) < SWA_HEAD_DIM
    for c in range(nk // LANES):
        kr = _rope_lanes(acc[:, nq + c * LANES:nq + (c + 1) * LANES], cs, sn, ROPE_HALF)
        swapped = pltpu.roll(kr, SWA_HEAD_DIM, 1)
        placements = (jnp.where(low, kr, 0.0), jnp.where(low, 0.0, swapped),
                      jnp.where(low, swapped, 0.0), jnp.where(low, 0.0, kr))
        for n, slab in enumerate(placements):
            k_ref[:, (4 * c + n) * LANES:(4 * c + n + 1) * LANES] = slab.astype(BF16)
    vt = lax.dot_general(wvt_ref[...], xb, (((1,), (1,)), ((), ())), preferred_element_type=F32)
    ones = jnp.ones((SWA_V_PAD - SWA_HEAD_DIM, xb.shape[0]), BF16)
    for kh in range(SWA_KV_HEADS):
        vt_ref[kh * SWA_V_PAD:kh * SWA_V_PAD + SWA_HEAD_DIM, :] = (
            vt[kh * SWA_HEAD_DIM:(kh + 1) * SWA_HEAD_DIM, :].astype(BF16))
        vt_ref[kh * SWA_V_PAD + SWA_HEAD_DIM:(kh + 1) * SWA_V_PAD, :] = ones


def _swa_qkv(x2, w_qkv, cos_t, sin_t):
    t, d = x2.shape
    nq = SWA_HEADS * SWA_HEAD_DIM
    nk = SWA_KV_HEADS * SWA_HEAD_DIM
    nk2 = SWA_KV_HEADS * 2 * LANES
    nvt = SWA_KV_HEADS * SWA_V_PAD
    tm = min(t, ROW_TILE)
    wb = w_qkv.astype(BF16)
    return pl.pallas_call(
        functools.partial(_swa_qkv_kernel, nq=nq, nk=nk),
        name="swa_qkv",
        grid=(t // tm,),
        in_specs=[pl.BlockSpec((tm, d), lambda i: (i, 0)),
                  pl.BlockSpec((d, nq + nk), lambda i: (0, 0)),
                  pl.BlockSpec((nk, d), lambda i: (0, 0)),
                  pl.BlockSpec((tm, 2 * LANES), lambda i: (i, 0)),
                  pl.BlockSpec((tm, 2 * LANES), lambda i: (i, 0))],
        out_specs=[pl.BlockSpec((tm, nq), lambda i: (i, 0)),
                   pl.BlockSpec((tm, nk2), lambda i: (i, 0)),
                   pl.BlockSpec((nvt, tm), lambda i: (0, i))],
        out_shape=[jax.ShapeDtypeStruct((t, nq), BF16),
                   jax.ShapeDtypeStruct((t, nk2), BF16),
                   jax.ShapeDtypeStruct((nvt, t), BF16)],
        compiler_params=_params(("parallel",)),
    )(x2, wb[:, :nq + nk], jnp.transpose(wb[:, nq + nk:]), cos_t, sin_t)


def _swa_attn_kernel(sink_ref, q_ref, kp_ref, kc_ref, vp_ref, vc_ref, o_ref):
    blk = pl.program_id(1)
    bq = q_ref.shape[0]
    dh = SWA_HEAD_DIM
    slabs = SWA_GROUP // 2
    cols = slabs * bq
    kj = lax.broadcasted_iota(I32, (2 * bq, cols), 0)
    qi = lax.broadcasted_iota(I32, (2 * bq, cols), 1) % bq
    rel = qi + bq - kj
    allowed = (rel >= 0) & (rel < SWA_WINDOW) & ((kj >= bq) | (blk > 0))
    for kh in range(SWA_KV_HEADS):
        base = kh * SWA_GROUP * dh
        qs = jnp.concatenate([q_ref[:, base + s * LANES:base + (s + 1) * LANES] for s in range(slabs)], axis=0)
        vcat = jnp.concatenate([vp_ref[kh * SWA_V_PAD:(kh + 1) * SWA_V_PAD, :],
                                vc_ref[kh * SWA_V_PAD:(kh + 1) * SWA_V_PAD, :]], axis=1)
        for half in range(2):
            ks = (2 * kh + half) * LANES
            kcat = jnp.concatenate([kp_ref[:, ks:ks + LANES], kc_ref[:, ks:ks + LANES]], axis=0)
            heads = [kh * SWA_GROUP + 2 * s + half for s in range(slabs)]
            sink = jnp.concatenate([jnp.full((1, bq), sink_ref[h] * LOG2_E, F32) for h in heads], axis=1)
            logits = lax.dot_general(kcat, qs, (((1,), (1,)), ((), ())), preferred_element_type=F32)
            logits = jnp.where(allowed, logits, NEG_BIG)
            m = jnp.maximum(jnp.max(logits, axis=0, keepdims=True), sink)
            p = jnp.exp2(logits - m).astype(BF16)
            pv = jnp.dot(vcat, p, preferred_element_type=F32)
            inv = 1.0 / (pv[dh:dh + 1, :] + jnp.exp2(sink - m))
            o = (pv[:dh, :] * inv).astype(BF16)
            for s, h in enumerate(heads):
                o_ref[h * dh:(h + 1) * dh, :] = o[:, s * bq:(s + 1) * bq]


def _swa_attn(q, k2, v_t, sinks, batch, seq):
    t, nq = q.shape
    nk2 = k2.shape[1]
    nvt = v_t.shape[0]
    bq = SWA_WINDOW
    nb = seq // bq

    def cur(b, i):
        return (b * nb + i, 0)

    def prev(b, i):
        return (b * nb + jnp.maximum(i - 1, 0), 0)

    def cur_t(b, i):
        return (0, b * nb + i)

    def prev_t(b, i):
        return (0, b * nb + jnp.maximum(i - 1, 0))

    return pl.pallas_call(
        _swa_attn_kernel,
        name="swa_attn",
        grid=(batch, nb),
        in_specs=[pl.BlockSpec(memory_space=pltpu.SMEM),
                  pl.BlockSpec((bq, nq), cur),
                  pl.BlockSpec((bq, nk2), prev),
                  pl.BlockSpec((bq, nk2), cur),
                  pl.BlockSpec((nvt, bq), prev_t),
                  pl.BlockSpec((nvt, bq), cur_t)],
        out_specs=pl.BlockSpec((nq, bq), cur_t),
        out_shape=jax.ShapeDtypeStruct((nq, t), BF16),
        compiler_params=_params(("parallel", "parallel")),
    )(sinks.astype(F32), q, k2, k2, v_t, v_t)


def _proj_ln_kernel(at_ref, w_ref, res_ref, g_ref, b_ref, o_ref, *, alpha):
    mix = lax.dot_general(at_ref[...], w_ref[...], (((0,), (0,)), ((), ())), preferred_element_type=F32)
    o_ref[...] = _layer_norm(alpha * res_ref[...] + mix, g_ref[...], b_ref[...])


def _proj_ln(a_t, w, res, g, b, alpha):
    k, t = a_t.shape
    d = w.shape[1]
    tm = min(t, ROW_TILE)
    return pl.pallas_call(
        functools.partial(_proj_ln_kernel, alpha=alpha),
        name="proj_ln",
        grid=(t // tm,),
        in_specs=[pl.BlockSpec((k, tm), lambda i: (0, i)),
                  pl.BlockSpec((k, d), lambda i: (0, 0)),
                  pl.BlockSpec((tm, d), lambda i: (i, 0)),
                  pl.BlockSpec((1, d), lambda i: (0, 0)),
                  pl.BlockSpec((1, d), lambda i: (0, 0))],
        out_specs=pl.BlockSpec((tm, d), lambda i: (i, 0)),
        out_shape=jax.ShapeDtypeStruct((t, d), F32),
        compiler_params=_params(("parallel",)),
    )(a_t, w.astype(BF16), res, g[None, :], b[None, :])


def _ffn_dense_kernel(x_ref, w1_ref, w3_ref, w2_ref, g_ref, b_ref, o_ref, xb_ref, acc_ref, *, alpha):
    f = pl.program_id(1)

    @pl.when(f == 0)
    def _():
        xb_ref[...] = x_ref[...].astype(BF16)
        acc_ref[...] = jnp.zeros_like(acc_ref)

    xb = xb_ref[...]
    gate = jnp.dot(xb, w1_ref[...], preferred_element_type=F32)
    up = jnp.dot(xb, w3_ref[...], preferred_element_type=F32)
    h = (jax.nn.silu(gate) * up).astype(BF16)
    acc_ref[...] += jnp.dot(h, w2_ref[...], preferred_element_type=F32)

    @pl.when(f == pl.num_programs(1) - 1)
    def _():
        o_ref[...] = _layer_norm(alpha * x_ref[...] + acc_ref[...], g_ref[...], b_ref[...])


def _ffn_tile(f_dim, largest):
    for tf in (largest, 512, 256, 128):
        if f_dim % tf == 0:
            return tf
    return f_dim


def _ffn_dense(x2, w1, w3, w2, g, b, alpha):
    t, d = x2.shape
    f_dim = w1.shape[1]
    tm = min(t, ROW_TILE)
    tf = _ffn_tile(f_dim, FFN_COL_TILE)
    return pl.pallas_call(
        functools.partial(_ffn_dense_kernel, alpha=alpha),
        name="ffn_dense",
        grid=(t // tm, f_dim // tf),
        in_specs=[pl.BlockSpec((tm, d), lambda i, f: (i, 0)),
                  pl.BlockSpec((d, tf), lambda i, f: (0, f)),
                  pl.BlockSpec((d, tf), lambda i, f: (0, f)),
                  pl.BlockSpec((tf, d), lambda i, f: (f, 0)),
                  pl.BlockSpec((1, d), lambda i, f: (0, 0)),
                  pl.BlockSpec((1, d), lambda i, f: (0, 0))],
        out_specs=pl.BlockSpec((tm, d), lambda i, f: (i, 0)),
        out_shape=jax.ShapeDtypeStruct((t, d), F32),
        scratch_shapes=[pltpu.VMEM((tm, d), BF16), pltpu.VMEM((tm, d), F32)],
        compiler_params=_params(("parallel", "arbitrary")),
    )(x2, w1.astype(BF16), w3.astype(BF16), w2.astype(BF16), g[None, :], b[None, :])


def _ple_kernel(x_ref, p_ref, wg_ref, wp_ref, o_ref):
    x = x_ref[...]
    gate = jax.nn.sigmoid(jnp.dot(x.astype(BF16), wg_ref[...], preferred_element_type=F32))
    emb = jnp.dot(p_ref[...].astype(BF16), wp_ref[...], preferred_element_type=F32)
    o_ref[...] = x + gate * emb


def _ple(x2, p2, w_g, w_p):
    t, d = x2.shape
    pd = p2.shape[1]
    tm = min(t, ROW_TILE)
    return pl.pallas_call(
        _ple_kernel,
        name="ple",
        grid=(t // tm,),
        in_specs=[pl.BlockSpec((tm, d), lambda i: (i, 0)),
                  pl.BlockSpec((tm, pd), lambda i: (i, 0)),
                  pl.BlockSpec((d, d), lambda i: (0, 0)),
                  pl.BlockSpec((pd, d), lambda i: (0, 0))],
        out_specs=pl.BlockSpec((tm, d), lambda i: (i, 0)),
        out_shape=jax.ShapeDtypeStruct((t, d), F32),
        compiler_params=_params(("parallel",)),
    )(x2, p2, w_g.astype(BF16), w_p.astype(BF16))


def _dsa_q_kernel(x_ref, w_ref, cos_ref, sin_ref, q_ref):
    xb = x_ref[...].astype(BF16)
    acc = jnp.dot(xb, w_ref[...], preferred_element_type=F32)
    cs = cos_ref[:, :LANES]
    sn = sin_ref[:, :LANES]
    scale = MLA_SCALE * LOG2_E
    for h in range(MLA_HEADS):
        base = h * MLA_QK_PAD
        q_ref[:, base:base + LANES] = (acc[:, base:base + LANES] * scale).astype(BF16)
        a = acc[:, base + LANES:base + 2 * LANES]
        q_ref[:, base + LANES:base + 2 * LANES] = (_rope_lanes(a, cs, sn, ROPE_HALF) * scale).astype(BF16)


def _dsa_q(x2, w_q, cos_t, sin_t):
    t, d = x2.shape
    n = w_q.shape[1]
    tm = min(t, ROW_TILE)
    return pl.pallas_call(
        _dsa_q_kernel,
        name="dsa_q",
        grid=(t // tm,),
        in_specs=[pl.BlockSpec((tm, d), lambda i: (i, 0)),
                  pl.BlockSpec((d, n), lambda i: (0, 0)),
                  pl.BlockSpec((tm, 2 * LANES), lambda i: (i, 0)),
                  pl.BlockSpec((tm, 2 * LANES), lambda i: (i, 0))],
        out_specs=pl.BlockSpec((tm, n), lambda i: (i, 0)),
        out_shape=jax.ShapeDtypeStruct((t, n), BF16),
        compiler_params=_params(("parallel",)),
    )(x2, w_q, cos_t, sin_t)


def _dsa_small_kernel(x_ref, w_ref, cos_ref, sin_ref, kvn_ref, ckv_ref, kr_ref, ki_ref, wi_ref):
    xb = x_ref[...].astype(BF16)
    acc = jnp.dot(xb, w_ref[...], preferred_element_type=F32)
    c = acc[:, :MLA_LATENT]
    ms = jnp.mean(c * c, axis=-1, keepdims=True)
    ckv_ref[...] = (c * lax.rsqrt(ms + RMS_EPS) * kvn_ref[...]).astype(BF16)
    o = MLA_LATENT
    kr_ref[...] = _rope_lanes(acc[:, o:o + LANES], cos_ref[:, :LANES], sin_ref[:, :LANES], ROPE_HALF).astype(BF16)
    ki_ref[...] = _rope_lanes(acc[:, o + LANES:o + 2 * LANES], cos_ref[:, LANES:], sin_ref[:, LANES:],
                              IDX_ROPE // 2).astype(BF16)
    wi_ref[...] = acc[:, o + 2 * LANES:o + 3 * LANES] * (IDX_HEADS ** -0.5)


def _dsa_small(x2, w_small, kv_norm, cos_t, sin_t):
    t, d = x2.shape
    n = w_small.shape[1]
    tm = min(t, ROW_TILE)
    return pl.pallas_call(
        _dsa_small_kernel,
        name="dsa_small",
        grid=(t // tm,),
        in_specs=[pl.BlockSpec((tm, d), lambda i: (i, 0)),
                  pl.BlockSpec((d, n), lambda i: (0, 0)),
                  pl.BlockSpec((tm, 2 * LANES), lambda i: (i, 0)),
                  pl.BlockSpec((tm, 2 * LANES), lambda i: (i, 0)),
                  pl.BlockSpec((1, MLA_LATENT), lambda i: (0, 0))],
        out_specs=[pl.BlockSpec((tm, MLA_LATENT), lambda i: (i, 0)),
                   pl.BlockSpec((tm, LANES), lambda i: (i, 0)),
                   pl.BlockSpec((tm, LANES), lambda i: (i, 0)),
                   pl.BlockSpec((tm, LANES), lambda i: (i, 0))],
        out_shape=[jax.ShapeDtypeStruct((t, MLA_LATENT), BF16),
                   jax.ShapeDtypeStruct((t, LANES), BF16),
                   jax.ShapeDtypeStruct((t, LANES), BF16),
                   jax.ShapeDtypeStruct((t, LANES), F32)],
        compiler_params=_params(("parallel",)),
    )(x2, w_small, cos_t, sin_t, kv_norm[None, :].astype(F32))


def _dsa_qi_kernel(x_ref, w_ref, cos_ref, sin_ref, q_ref):
    xb = x_ref[...].astype(BF16)
    acc = jnp.dot(xb, w_ref[...], preferred_element_type=F32)
    cs = cos_ref[:, LANES:]
    sn = sin_ref[:, LANES:]
    scale = IDX_DIM ** -0.5
    for c in range(acc.shape[1] // LANES):
        a = acc[:, c * LANES:(c + 1) * LANES]
        q_ref[:, c * LANES:(c + 1) * LANES] = (_rope_lanes(a, cs, sn, IDX_ROPE // 2) * scale).astype(BF16)


def _dsa_qi(x2, w_qi, cos_t, sin_t):
    t, d = x2.shape
    n = w_qi.shape[1]
    tm = min(t, ROW_TILE)
    return pl.pallas_call(
        _dsa_qi_kernel,
        name="dsa_qi",
        grid=(t // tm,),
        in_specs=[pl.BlockSpec((tm, d), lambda i: (i, 0)),
                  pl.BlockSpec((d, n), lambda i: (0, 0)),
                  pl.BlockSpec((tm, 2 * LANES), lambda i: (i, 0)),
                  pl.BlockSpec((tm, 2 * LANES), lambda i: (i, 0))],
        out_specs=pl.BlockSpec((tm, n), lambda i: (i, 0)),
        out_shape=jax.ShapeDtypeStruct((t, n), BF16),
        compiler_params=_params(("parallel",)),
    )(x2, w_qi, cos_t, sin_t)


def _dsa_kv_kernel(c_ref, kr_ref, wk_ref, wvt_ref, k_ref, vt_ref):
    c = c_ref[...]
    kn = jnp.dot(c, wk_ref[...], preferred_element_type=F32)
    kr = kr_ref[...]
    for h in range(MLA_HEADS):
        base = h * MLA_QK_PAD
        k_ref[:, base:base + LANES] = kn[:, h * LANES:(h + 1) * LANES].astype(BF16)
        k_ref[:, base + LANES:base + 2 * LANES] = kr
    vt = lax.dot_general(wvt_ref[...], c, (((1,), (1,)), ((), ())), preferred_element_type=F32)
    ones = jnp.ones((MLA_V_PAD - MLA_V, c.shape[0]), BF16)
    for h in range(MLA_HEADS):
        vt_ref[h * MLA_V_PAD:h * MLA_V_PAD + MLA_V, :] = vt[h * MLA_V:(h + 1) * MLA_V, :].astype(BF16)
        vt_ref[h * MLA_V_PAD + MLA_V:(h + 1) * MLA_V_PAD, :] = ones


def _dsa_kv(ckv, kr, w_uk_all, w_uv_t):
    t, c_dim = ckv.shape
    nk = MLA_HEADS * MLA_QK_PAD
    nv = w_uv_t.shape[0]
    nv_pad = MLA_HEADS * MLA_V_PAD
    tm = min(t, ROW_TILE)
    return pl.pallas_call(
        _dsa_kv_kernel,
        name="dsa_kv",
        grid=(t // tm,),
        in_specs=[pl.BlockSpec((tm, c_dim), lambda i: (i, 0)),
                  pl.BlockSpec((tm, LANES), lambda i: (i, 0)),
                  pl.BlockSpec((c_dim, MLA_HEADS * MLA_NOPE), lambda i: (0, 0)),
                  pl.BlockSpec((nv, c_dim), lambda i: (0, 0))],
        out_specs=[pl.BlockSpec((tm, nk), lambda i: (i, 0)),
                   pl.BlockSpec((nv_pad, tm), lambda i: (0, i))],
        out_shape=[jax.ShapeDtypeStruct((t, nk), BF16),
                   jax.ShapeDtypeStruct((nv_pad, t), BF16)],
        compiler_params=_params(("parallel",)),
    )(ckv, kr, w_uk_all, w_uv_t)


def _float_sort_key(s):
    bits = lax.bitcast_convert_type(s, I32)
    return bits ^ ((bits >> 31) & jnp.int32(0x7FFFFFFF))


def _dsa_select_kernel(k_ref, q_ref, w_ref, bias_ref, key_ref, tie_ref, *, sc, cc, unroll, n_sel):
    j = pl.program_id(1)
    s_len, tq = key_ref.shape
    n_chunks = s_len // sc
    q_lo = j * tq
    n_need = (q_lo + tq + sc * unroll - 1) // (sc * unroll) * unroll
    qt = q_ref[...]
    w_row = w_ref[...]
    q_pos = q_lo + lax.broadcasted_iota(I32, (sc, tq), 1)
    k_off = lax.broadcasted_iota(I32, (sc, tq), 0)
    min_key = jnp.int32(-2 ** 31)

    def score_chunks(i, carry):
        for u in range(unroll):
            row0 = pl.multiple_of((i * unroll + u) * sc, sc)
            kc = k_ref[pl.ds(row0, sc), :]
            d = jnp.dot(kc, qt, preferred_element_type=F32)
            s = jnp.zeros((sc, tq), F32)
            for h in range(IDX_HEADS):
                s = s + jnp.maximum(d[:, h * tq:(h + 1) * tq], 0.0) * w_row[:, h * tq:(h + 1) * tq]
            causal = (row0 + k_off) <= q_pos
            key_ref[pl.ds(row0, sc), :] = jnp.where(causal, _float_sort_key(s), min_key)
        return carry

    lax.fori_loop(0, n_need // unroll, score_chunks, 0)

    n_cnt = (n_need * sc + cc - 1) // cc

    def fill_chunk(c, carry):
        row0 = pl.multiple_of(c * sc, sc)
        key_ref[pl.ds(row0, sc), :] = jnp.full((sc, tq), min_key, I32)
        return carry

    lax.fori_loop(n_need, n_cnt * (cc // sc), fill_chunk, 0)

    def count_ge(cand):
        def body(c, cnt):
            row0 = pl.multiple_of(c * cc, cc)
            hit = jnp.where(key_ref[pl.ds(row0, cc), :] >= cand, 1, 0).astype(I32)
            return cnt + jnp.sum(hit.reshape(cc // 32, 32, tq), axis=0)

        cnt = lax.fori_loop(0, n_cnt, body, jnp.zeros((32, tq), I32))
        return jnp.sum(cnt, axis=0, keepdims=True)

    zero = jnp.zeros((1, tq), I32)
    c0 = count_ge(zero)
    thr = jnp.where(c0 >= n_sel, zero, jnp.full((1, tq), min_key, I32))
    cnt_thr = jnp.where(c0 >= n_sel, c0, 0)

    def bit_step(it, carry):
        thr, cnt_thr = carry
        cand = thr + (jnp.int32(1) << (30 - it))
        cnt = count_ge(cand)
        take = cnt >= n_sel
        return jnp.where(take, cand, thr), jnp.where(take, cnt, cnt_thr)

    thr, cnt_thr = lax.fori_loop(0, 31, bit_step, (thr, cnt_thr))

    excess = cnt_thr - n_sel
    tie_ref[0:1, :] = jnp.full((1, tq), s_len, I32)

    @pl.when(jnp.max(excess) > 0)
    def _():
        def count_tie_below(x):
            def body(c, cnt):
                row0 = pl.multiple_of(c * cc, cc)
                idx = row0 + lax.broadcasted_iota(I32, (cc, tq), 0)
                hit = jnp.where((key_ref[pl.ds(row0, cc), :] == thr) & (idx < x), 1, 0).astype(I32)
                return cnt + jnp.sum(hit.reshape(cc // 32, 32, tq), axis=0)

            cnt = lax.fori_loop(0, n_cnt, body, jnp.zeros((32, tq), I32))
            return jnp.sum(cnt, axis=0, keepdims=True)

        n_tie = count_tie_below(jnp.full((1, tq), s_len, I32))
        need = n_sel - (cnt_thr - n_tie)
        n_bits = s_len.bit_length()

        def idx_step(it, x):
            cand = x + (jnp.int32(1) << (n_bits - 1 - it))
            return jnp.where(count_tie_below(cand) <= need, cand, x)

        x = lax.fori_loop(0, n_bits, idx_step, zero)
        tie_ref[0:1, :] = jnp.where(excess > 0, x, s_len)

    tie_below = tie_ref[0:1, :]
    thr_low = jnp.maximum(thr, min_key + 1)

    def write_chunk(c, carry):
        row0 = pl.multiple_of(c * sc, sc)
        thr_here = thr_low + jnp.where(k_off >= tie_below - row0, 1, 0)
        keep = key_ref[pl.ds(row0, sc), :] >= thr_here
        bias_ref[pl.ds(row0, sc), :] = jnp.where(keep, 0.0, NEG_BIG).astype(F32)
        return carry

    def mask_chunk(c, carry):
        row0 = pl.multiple_of(c * sc, sc)
        bias_ref[pl.ds(row0, sc), :] = jnp.full((sc, tq), NEG_BIG, F32)
        return carry

    lax.fori_loop(0, n_need, write_chunk, 0)
    lax.fori_loop(n_need, n_chunks, mask_chunk, 0)


def _dsa_select(k_i, q_i_t, w_rows, batch, seq, n_sel):
    tq = LANES
    nq = seq // tq
    sc = min(seq, SELECT_SCORE_ROWS)
    cc = min(seq, SELECT_COUNT_ROWS)
    return pl.pallas_call(
        functools.partial(_dsa_select_kernel, sc=sc, cc=cc, unroll=2 if (seq // sc) % 2 == 0 else 1, n_sel=n_sel),
        name="dsa_select",
        grid=(batch, nq),
        in_specs=[pl.BlockSpec((None, seq, IDX_DIM), lambda b, j: (b, 0, 0)),
                  pl.BlockSpec((None, None, IDX_DIM, IDX_HEADS * tq), lambda b, j: (b, j, 0, 0)),
                  pl.BlockSpec((None, None, 1, IDX_HEADS * tq), lambda b, j: (b, j, 0, 0))],
        out_specs=pl.BlockSpec((None, seq, tq), lambda b, j: (b, 0, j)),
        out_shape=jax.ShapeDtypeStruct((batch, seq, seq), F32),
        scratch_shapes=[pltpu.VMEM((seq, tq), I32), pltpu.VMEM((8, tq), I32)],
        compiler_params=_params(("parallel", "parallel")),
    )(k_i, q_i_t, w_rows)


def _dsa_attn_kernel(qt_ref, k_ref, vta_ref, vtb_ref, bias_ref, o_ref, m_ref, a_ref, *head_refs, sc):
    s_refs = head_refs[:MLA_HEADS]
    acc_refs = head_refs[MLA_HEADS:]
    j = pl.program_id(1)
    g = pl.program_id(2)
    tq = qt_ref.shape[1]
    last_pair = (j * tq + tq - 1) // (2 * sc)

    @pl.when(g == 0)
    def _():
        for h in range(MLA_HEADS):
            acc_refs[h][...] = jnp.zeros_like(acc_refs[h])
            s_refs[h][1] = jnp.full((sc, tq), NEG_BIG, F32)
        m_ref[...] = jnp.full_like(m_ref, M_INIT)
        a_ref[...] = jnp.ones_like(a_ref)

    def finish(slot, vt_ref, m_now, a_now):
        for h in range(MLA_HEADS):
            p = jnp.exp2(s_refs[h][slot] - m_now[h:h + 1, :]).astype(BF16)
            pv = jnp.dot(vt_ref[h * MLA_V_PAD:(h + 1) * MLA_V_PAD, :], p, preferred_element_type=F32)
            acc_refs[h][...] = a_now[h:h + 1, :] * acc_refs[h][...] + pv

    def score(slot, half, m_now):
        rows = pl.ds(half * sc, sc)
        chunk_max = []
        for h in range(MLA_HEADS):
            qh = qt_ref[h * MLA_QK_PAD:(h + 1) * MLA_QK_PAD, :]
            kh = k_ref[rows, h * MLA_QK_PAD:(h + 1) * MLA_QK_PAD]
            s = jnp.dot(kh, qh, preferred_element_type=F32) + bias_ref[rows, :]
            s_refs[h][slot] = s
            chunk_max.append(jnp.max(s, axis=0, keepdims=True))
        m_new = jnp.maximum(m_now, jnp.concatenate(chunk_max, axis=0))
        return m_new, jnp.exp2(m_now - m_new)

    @pl.when(g <= last_pair)
    def _():
        m0 = m_ref[...]
        finish(1, vta_ref, m0, a_ref[...])
        m1, a1 = score(0, 0, m0)
        finish(0, vtb_ref, m1, a1)
        m2, a2 = score(1, 1, m1)
        m_ref[...] = m2
        a_ref[...] = a2

    @pl.when(g == last_pair + 1)
    def _():
        finish(1, vta_ref, m_ref[...], a_ref[...])
        for h in range(MLA_HEADS):
            inv = 1.0 / acc_refs[h][MLA_V:MLA_V + 1, :]
            o_ref[h * MLA_V:(h + 1) * MLA_V, :] = (acc_refs[h][:MLA_V, :] * inv).astype(BF16)


def _dsa_attn(q_cat_t, k_cat, v_t, bias, batch, seq):
    nqk, t = q_cat_t.shape
    nv_pad = v_t.shape[0]
    nv = MLA_HEADS * MLA_V
    tq = min(seq, ATTN_QUERY_TILE)
    sc = min(ATTN_KEY_CHUNK, tq // 2)
    nq = seq // tq
    nc = seq // sc
    npair = nc // 2

    def last_pair(j):
        return (j * tq + tq - 1) // (2 * sc)

    return pl.pallas_call(
        functools.partial(_dsa_attn_kernel, sc=sc),
        name="dsa_attn",
        grid=(batch, nq, npair + 1),
        in_specs=[pl.BlockSpec((nqk, tq), lambda b, j, g: (0, b * nq + j)),
                  pl.BlockSpec((2 * sc, nqk), lambda b, j, g: (b * npair + jnp.minimum(g, last_pair(j)), 0)),
                  pl.BlockSpec((nv_pad, sc), lambda b, j, g: (0, b * nc + jnp.clip(2 * g - 1, 0, 2 * last_pair(j) + 1))),
                  pl.BlockSpec((nv_pad, sc), lambda b, j, g: (0, b * nc + jnp.minimum(2 * g, 2 * last_pair(j)))),
                  pl.BlockSpec((None, 2 * sc, tq), lambda b, j, g: (b, jnp.minimum(g, last_pair(j)), j))],
        out_specs=pl.BlockSpec((nv, tq), lambda b, j, g: (0, b * nq + j)),
        out_shape=jax.ShapeDtypeStruct((nv, t), BF16),
        scratch_shapes=([pltpu.VMEM((MLA_HEADS, tq), F32), pltpu.VMEM((MLA_HEADS, tq), F32)]
                        + [pltpu.VMEM((2, sc, tq), F32)] * MLA_HEADS
                        + [pltpu.VMEM((MLA_V_PAD, tq), F32)] * MLA_HEADS),
        compiler_params=_params(("parallel", "parallel", "arbitrary")),
    )(q_cat_t, k_cat, v_t, v_t, bias)


def _dsa_mixer_out(x2, cos_t, sin_t, w_in, kv_norm, w_uk, w_uv, batch, seq):
    t, d = x2.shape
    nq = MLA_HEADS * (MLA_NOPE + MLA_ROPE)
    o_c = nq
    o_kr = o_c + MLA_LATENT
    o_qi = o_kr + MLA_ROPE
    o_ki = o_qi + IDX_HEADS * IDX_DIM
    o_wi = o_ki + IDX_DIM
    wb = w_in.astype(BF16)
    wq = wb[:, :nq].reshape(d, MLA_HEADS, MLA_NOPE + MLA_ROPE)
    wq = jnp.pad(wq, ((0, 0), (0, 0), (0, MLA_QK_PAD - MLA_NOPE - MLA_ROPE))).reshape(d, MLA_HEADS * MLA_QK_PAD)
    pad_to_lanes = lambda w: jnp.pad(w, ((0, 0), (0, LANES - w.shape[1])))
    w_small = jnp.concatenate([wb[:, o_c:o_kr], pad_to_lanes(wb[:, o_kr:o_qi]), pad_to_lanes(wb[:, o_ki:o_wi]),
                               pad_to_lanes(wb[:, o_wi:])], axis=1)
    w_qi = wb[:, o_qi:o_ki]
    w_uk_all = jnp.transpose(w_uk, (1, 0, 2)).reshape(MLA_LATENT, MLA_HEADS * MLA_NOPE).astype(BF16)
    w_uv_t = jnp.transpose(w_uv, (0, 2, 1)).reshape(MLA_HEADS * MLA_V, MLA_LATENT).astype(BF16)

    q_cat = _dsa_q(x2, wq, cos_t, sin_t)
    ckv, kr, ki, wi = _dsa_small(x2, w_small, kv_norm, cos_t, sin_t)
    qi = _dsa_qi(x2, w_qi, cos_t, sin_t)
    k_cat, v_t = _dsa_kv(ckv, kr, w_uk_all, w_uv_t)

    tq = LANES
    nqb = seq // tq
    k_i = ki[:, :IDX_DIM].reshape(batch, seq, IDX_DIM)
    q_i_t = jnp.transpose(qi.reshape(batch, nqb, tq, IDX_HEADS, IDX_DIM), (0, 1, 4, 3, 2))
    q_i_t = q_i_t.reshape(batch, nqb, IDX_DIM, IDX_HEADS * tq)
    w_rows = jnp.transpose(wi[:, :IDX_HEADS].reshape(batch, nqb, tq, IDX_HEADS), (0, 1, 3, 2))
    w_rows = w_rows.reshape(batch, nqb, 1, IDX_HEADS * tq)
    n_sel = min(DSA_TOPK, seq // 4)
    bias = _dsa_select(k_i, q_i_t, w_rows, batch, seq, n_sel)
    o_t = _dsa_attn(jnp.transpose(q_cat), k_cat, v_t, bias, batch, seq)
    return o_t


def _router_kernel(x_ref, w_ref, idx_ref, gate_ref):
    logits = jnp.dot(x_ref[...], w_ref[...], precision=lax.Precision.HIGHEST, preferred_element_type=F32)
    lane = lax.broadcasted_iota(I32, logits.shape, 1)
    lg = jnp.where(lane < MOE_EXPERTS, logits, -jnp.inf)
    v1 = jnp.max(lg, axis=-1, keepdims=True)
    i1 = jnp.min(jnp.where(lg == v1, lane, LANES), axis=-1, keepdims=True)
    lg2 = jnp.where(lane == i1, -jnp.inf, lg)
    v2 = jnp.max(lg2, axis=-1, keepdims=True)
    i2 = jnp.min(jnp.where(lg2 == v2, lane, LANES), axis=-1, keepdims=True)
    e2 = jnp.exp(v2 - v1)
    den = 1.0 + e2
    idx_ref[...] = jnp.where(lane == 0, i1, jnp.where(lane == 1, i2, 0))
    gate_ref[...] = jnp.where(lane == 0, 1.0 / den, jnp.where(lane == 1, e2 / den, 0.0))


def _router(x2, w_router):
    t, d = x2.shape
    wr = jnp.pad(w_router.astype(F32), ((0, 0), (0, LANES - w_router.shape[1])))
    tm = min(t, ROW_TILE)
    return pl.pallas_call(
        _router_kernel,
        name="moe_router",
        grid=(t // tm,),
        in_specs=[pl.BlockSpec((tm, d), lambda i: (i, 0)),
                  pl.BlockSpec((d, LANES), lambda i: (0, 0))],
        out_specs=[pl.BlockSpec((tm, LANES), lambda i: (i, 0)),
                   pl.BlockSpec((tm, LANES), lambda i: (i, 0))],
        out_shape=[jax.ShapeDtypeStruct((t, LANES), I32),
                   jax.ShapeDtypeStruct((t, LANES), F32)],
        compiler_params=_params(("parallel",)),
    )(x2, wr)


def _row_copy(src_hbm, src_row, dst, dst_row, sem):
    return pltpu.make_async_copy(src_hbm.at[pl.ds(src_row, 1)], dst.at[pl.ds(dst_row, 1)], sem)


def _moe_ffn_kernel(be_ref, nu_ref, tok_ref, tok_next_ref, x_hbm, w1_ref, w3_ref, w2_ref, o_ref,
                    xg_ref, xb_ref, acc_ref, sem):
    i = pl.program_id(0)
    f = pl.program_id(1)
    nf = pl.num_programs(1)
    n_used = nu_ref[0]
    used = i < n_used
    tm = xb_ref.shape[0]
    slot = i % 2

    def start_gather(ids_ref, buf):
        def body(r, carry):
            _row_copy(x_hbm, ids_ref[r], xg_ref.at[buf], r, sem.at[buf]).start()
            return carry

        lax.fori_loop(0, tm, body, 0, unroll=8)

    @pl.when(used & (f == 0))
    def _():
        @pl.when(i == 0)
        def _():
            start_gather(tok_ref, 0)

        pltpu.make_async_copy(x_hbm.at[pl.ds(0, tm)], xg_ref.at[slot], sem.at[slot]).wait()

        @pl.when(i + 1 < n_used)
        def _():
            start_gather(tok_next_ref, 1 - slot)

        xb_ref[...] = xg_ref[slot].astype(BF16)
        acc_ref[...] = jnp.zeros_like(acc_ref)

    @pl.when(used)
    def _():
        xb = xb_ref[...]
        gate = jnp.dot(xb, w1_ref[...], preferred_element_type=F32)
        up = jnp.dot(xb, w3_ref[...], preferred_element_type=F32)
        h = (jax.nn.silu(gate) * up).astype(BF16)
        acc_ref[...] += jnp.dot(h, w2_ref[...], preferred_element_type=F32)

    @pl.when(used & (f == nf - 1))
    def _():
        o_ref[...] = acc_ref[...]

    @pl.when(jnp.logical_not(used) & (f == nf - 1))
    def _():
        o_ref[...] = jnp.zeros_like(o_ref)


def _moe_ffn(x2, slot_tok, w1, w3, w2, blk_expert, n_used, tm):
    n_slots = slot_tok.shape[0]
    d = x2.shape[1]
    f_dim = w1.shape[2]
    tf = _ffn_tile(f_dim, MOE_COL_TILE)
    nf = f_dim // tf
    nt = n_slots // tm

    def f_sel(i, f, nu):
        return jnp.where(i < nu[0], f, nf - 1)

    grid_spec = pltpu.PrefetchScalarGridSpec(
        num_scalar_prefetch=2,
        grid=(nt, nf),
        in_specs=[pl.BlockSpec((tm,), lambda i, f, be, nu: (i,), memory_space=pltpu.SMEM),
                  pl.BlockSpec((tm,), lambda i, f, be, nu: (jnp.minimum(i + 1, nt - 1),), memory_space=pltpu.SMEM),
                  pl.BlockSpec(memory_space=pl.ANY),
                  pl.BlockSpec((None, d, tf), lambda i, f, be, nu: (be[i], 0, f_sel(i, f, nu))),
                  pl.BlockSpec((None, d, tf), lambda i, f, be, nu: (be[i], 0, f_sel(i, f, nu))),
                  pl.BlockSpec((None, tf, d), lambda i, f, be, nu: (be[i], f_sel(i, f, nu), 0))],
        out_specs=pl.BlockSpec((tm, d), lambda i, f, be, nu: (i, 0)),
        scratch_shapes=[pltpu.VMEM((2, tm, d), F32), pltpu.VMEM((tm, d), BF16), pltpu.VMEM((tm, d), F32),
                        pltpu.SemaphoreType.DMA((2,))],
    )
    return pl.pallas_call(
        _moe_ffn_kernel,
        name="moe_ffn",
        grid_spec=grid_spec,
        out_shape=jax.ShapeDtypeStruct((n_slots, d), F32),
        compiler_params=_params(("arbitrary", "arbitrary")),
    )(blk_expert, n_used, slot_tok, slot_tok, x2, w1, w3, w2)


def _moe_combine_kernel(dest_ref, dest_next_ref, y_hbm, x_ref, gate_ref, g_ref, b_ref, o_ref, y_ref, sem, *,
                        alpha, tb):
    i = pl.program_id(0)
    slot = i % 2

    def start_gather(ids_ref, buf):
        def body(tok, carry):
            for k in range(MOE_TOP_K):
                _row_copy(y_hbm, ids_ref[MOE_TOP_K * tok + k], y_ref.at[buf, k], tok, sem.at[buf]).start()
            return carry

        lax.fori_loop(0, tb, body, 0, unroll=4)

    @pl.when(i == 0)
    def _():
        start_gather(dest_ref, 0)

    for k in range(MOE_TOP_K):
        pltpu.make_async_copy(y_hbm.at[pl.ds(0, tb)], y_ref.at[slot, k], sem.at[slot]).wait()

    @pl.when(i + 1 < pl.num_programs(0))
    def _():
        start_gather(dest_next_ref, 1 - slot)

    gates = gate_ref[...]
    ff = gates[:, 0:1] * y_ref[slot, 0] + gates[:, 1:2] * y_ref[slot, 1]
    o_ref[...] = _layer_norm(alpha * x_ref[...] + ff, g_ref[...], b_ref[...])


def _moe_combine(x2, y, dest, gates, g, b, alpha):
    t, d = x2.shape
    tb = min(t, COMBINE_ROW_TILE)
    nt = t // tb
    return pl.pallas_call(
        functools.partial(_moe_combine_kernel, alpha=alpha, tb=tb),
        name="moe_combine",
        grid=(nt,),
        in_specs=[pl.BlockSpec((tb * MOE_TOP_K,), lambda i: (i,), memory_space=pltpu.SMEM),
                  pl.BlockSpec((tb * MOE_TOP_K,), lambda i: (jnp.minimum(i + 1, nt - 1),), memory_space=pltpu.SMEM),
                  pl.BlockSpec(memory_space=pl.ANY),
                  pl.BlockSpec((tb, d), lambda i: (i, 0)),
                  pl.BlockSpec((tb, LANES), lambda i: (i, 0)),
                  pl.BlockSpec((1, d), lambda i: (0, 0)),
                  pl.BlockSpec((1, d), lambda i: (0, 0))],
        out_specs=pl.BlockSpec((tb, d), lambda i: (i, 0)),
        out_shape=jax.ShapeDtypeStruct((t, d), F32),
        scratch_shapes=[pltpu.VMEM((2, MOE_TOP_K, tb, d), F32), pltpu.SemaphoreType.DMA((2,))],
        compiler_params=_params(("arbitrary",)),
    )(dest, dest, y, x2, gates, g[None, :], b[None, :])


def _moe_layer(x2, w_router, w1, w3, w2, g, b, alpha):
    t, d = x2.shape
    n = t * MOE_TOP_K
    tm = min(ROW_TILE, t)
    idx_pad, gates = _router(x2, w_router)
    e_flat = idx_pad[:, :MOE_TOP_K].reshape(n)
    flat_ids = jnp.arange(n, dtype=I32)
    _, order = lax.sort((e_flat, flat_ids), num_keys=1, is_stable=True)
    _, inv_order = lax.sort((order, flat_ids), num_keys=1)
    experts = jnp.arange(MOE_EXPERTS, dtype=I32)
    counts = jnp.sum((e_flat[:, None] == experts[None, :]).astype(I32), axis=0)
    starts = jnp.cumsum(counts) - counts
    padded = (counts + tm - 1) // tm * tm
    pends = jnp.cumsum(padded)
    pstarts = pends - padded
    n_tiles = (n + MOE_EXPERTS * (tm - 1) + tm - 1) // tm
    n_used = (pends[-1] // tm).astype(I32)
    blk_start = jnp.arange(n_tiles, dtype=I32) * tm
    blk_expert = jnp.minimum(jnp.sum((pends[None, :] <= blk_start[:, None]).astype(I32), axis=1), MOE_EXPERTS - 1)
    blk_expert = jnp.where(jnp.arange(n_tiles) < n_used, blk_expert, blk_expert[jnp.maximum(n_used - 1, 0)])
    n_slots = n_tiles * tm
    slot = jnp.arange(n_slots, dtype=I32)
    slot_e = jnp.repeat(blk_expert, tm)
    shift = pstarts - starts
    order_tok = jnp.pad(order // MOE_TOP_K, (0, n_slots - n))
    slot_tok = jnp.zeros((n_slots,), I32)
    for e in range(MOE_EXPERTS):
        in_group = (slot_e == e) & (slot - pstarts[e] < counts[e]) & (slot < n_used * tm)
        slot_tok = jnp.where(in_group, jnp.roll(order_tok, shift[e]), slot_tok)
    dest = (inv_order + jnp.sum(jnp.where(e_flat[:, None] == experts[None, :], shift[None, :], 0), axis=1)).astype(I32)

    y = _moe_ffn(x2, slot_tok, w1.astype(BF16), w3.astype(BF16), w2.astype(BF16), blk_expert.astype(I32),
                 n_used.reshape(1), tm)
    return _moe_combine(x2, y, dest, gates, g, b, alpha)


def kernel(x, p, positions, swa_w_qkv, swa_sinks, swa_w_o, dense_w1, dense_w3, dense_w2, dsa_w_in, dsa_kv_norm,
           dsa_w_uk, dsa_w_uv, dsa_w_o, moe_router, moe_w1, moe_w3, moe_w2, ln_g, ln_b, ple_w_p, ple_w_g):
    batch, seq, d = x.shape
    depth = ln_g.shape[0]
    alpha = (2.0 * depth) ** 0.25
    t = batch * seq
    assert seq % (2 * LANES) == 0 and d % LANES == 0, (seq, d)
    assert t % min(t, ROPE_ROW_TILE) == 0, t
    x2 = x.reshape(t, d)
    cos_t, sin_t = _rope_tables(positions)
    for i in range(depth):
        j = i // 2
        if i % 2 == 0:
            q, k2, v_t = _swa_qkv(x2, swa_w_qkv[j], cos_t, sin_t)
            o = _swa_attn(q, k2, v_t, swa_sinks[j], batch, seq)
            x2 = _proj_ln(o, swa_w_o[j], x2, ln_g[i, 0], ln_b[i, 0], alpha)
            x2 = _ffn_dense(x2, dense_w1[j], dense_w3[j], dense_w2[j], ln_g[i, 1], ln_b[i, 1], alpha)
        else:
            o = _dsa_mixer_out(x2, cos_t, sin_t, dsa_w_in[j], dsa_kv_norm[j], dsa_w_uk[j], dsa_w_uv[j], batch, seq)
            x2 = _proj_ln(o, dsa_w_o[j], x2, ln_g[i, 0], ln_b[i, 0], alpha)
            x2 = _moe_layer(x2, moe_router[j], moe_w1[j], moe_w3[j], moe_w2[j], ln_g[i, 1], ln_b[i, 1], alpha)
        x2 = _ple(x2, p[i].reshape(t, -1), ple_w_g[i], ple_w_p[i])
    return x2.reshape(batch, seq, d)
```

```python
import functools

import jax
import jax.numpy as jnp
from jax import lax
from jax.experimental import pallas as pl
from jax.experimental.pallas import tpu as pltpu

F32 = jnp.float32
BF16 = jnp.bfloat16
I32 = jnp.int32

LANES = 128
ROPE_THETA = 10000.0
LN_EPS = 1e-5
RMS_EPS = 1e-6
SWA_HEADS = 32
SWA_KV_HEADS = 4
SWA_GROUP = SWA_HEADS // SWA_KV_HEADS
SWA_HEAD_DIM = 64
SWA_WINDOW = 128
SWA_V_PAD = SWA_HEAD_DIM + 16
MLA_HEADS = 16
MLA_NOPE = 128
MLA_ROPE = 64
MLA_V = 128
MLA_LATENT = 512
MLA_SCALE = (MLA_NOPE + MLA_ROPE) ** -0.5
MLA_QK_PAD = 256
MLA_V_PAD = MLA_V + 16
LOG2_E = 1.4426950408889634
IDX_HEADS = 16
IDX_DIM = 64
IDX_ROPE = 32
DSA_TOPK = 256
MOE_EXPERTS = 8
MOE_TOP_K = 2
NEG_BIG = -1e30
M_INIT = -1e20
VMEM_LIMIT = 56 * 1024 * 1024
ROW_TILE = 512
ROPE_ROW_TILE = 1024
FFN_COL_TILE = 512
MOE_COL_TILE = 1024
COMBINE_ROW_TILE = 256
SELECT_SCORE_ROWS = 256
SELECT_COUNT_ROWS = 1024
ATTN_QUERY_TILE = 512
ATTN_KEY_CHUNK = 256
ROPE_HALF = 32


def _params(sem):
    return pltpu.CompilerParams(dimension_semantics=sem, vmem_limit_bytes=VMEM_LIMIT)


def _layer_norm(y, g, b):
    mu = jnp.mean(y, axis=-1, keepdims=True)
    d = y - mu
    var = jnp.mean(d * d, axis=-1, keepdims=True)
    return d * lax.rsqrt(var + LN_EPS) * g + b


def _rope_lanes(a, cs, sn, half):
    lane = lax.broadcasted_iota(I32, a.shape, 1)
    first = (lane % 64) < half
    rot = jnp.where(first, pltpu.roll(a, LANES - half, 1), pltpu.roll(a, half, 1))
    return a * cs + rot * sn


def _rope_table_kernel(pos_ref, invf_ref, sgn_ref, cos_ref, sin_ref):
    ang = pos_ref[...].astype(F32) * invf_ref[...]
    cos_ref[...] = jnp.cos(ang)
    sin_ref[...] = jnp.sin(ang) * sgn_ref[...]


def _rope_tables(positions):
    t = positions.size
    pos = positions.reshape(t, 1).astype(I32)
    lane = jnp.arange(LANES)
    in64 = lane % 64
    f64 = ROPE_THETA ** (-jnp.arange(0, 64, 2, dtype=F32) / 64)
    f32_ = ROPE_THETA ** (-jnp.arange(0, IDX_ROPE, 2, dtype=F32) / IDX_ROPE)
    invf_a = f64[in64 % 32]
    sgn_a = jnp.where(in64 < 32, -1.0, 1.0).astype(F32)
    invf_b = jnp.where(in64 < IDX_ROPE, f32_[in64 % (IDX_ROPE // 2)], 0.0)
    sgn_b = jnp.where(in64 < IDX_ROPE // 2, -1.0, 1.0).astype(F32)
    invf = jnp.concatenate([invf_a, invf_b])[None, :]
    sgn = jnp.concatenate([sgn_a, sgn_b])[None, :]
    tm = min(t, ROPE_ROW_TILE)
    return pl.pallas_call(
        _rope_table_kernel,
        name="rope_tables",
        grid=(t // tm,),
        in_specs=[pl.BlockSpec((tm, 1), lambda i: (i, 0)),
                  pl.BlockSpec((1, 2 * LANES), lambda i: (0, 0)),
                  pl.BlockSpec((1, 2 * LANES), lambda i: (0, 0))],
        out_specs=[pl.BlockSpec((tm, 2 * LANES), lambda i: (i, 0)),
                   pl.BlockSpec((tm, 2 * LANES), lambda i: (i, 0))],
        out_shape=[jax.ShapeDtypeStruct((t, 2 * LANES), F32)] * 2,
        compiler_params=_params(("parallel",)),
    )(pos, invf, sgn)


def _swa_qkv_kernel(x_ref, w_ref, wvt_ref, cos_ref, sin_ref, q_ref, k_ref, vt_ref, *, nq, nk):
    xb = x_ref[...].astype(BF16)
    acc = jnp.dot(xb, w_ref[...], preferred_element_type=F32)
    cs = cos_ref[:, :LANES]
    sn = sin_ref[:, :LANES]
    scale = SWA_HEAD_DIM ** -0.5 * LOG2_E
    for c in range(nq // LANES):
        a = acc[:, c * LANES:(c + 1) * LANES]
        q_ref[:, c * LANES:(c + 1) * LANES] = (_rope_lanes(a, cs, sn, ROPE_HALF) * scale).astype(BF16)
    low = lax.broadcasted_iota(I32, (acc.shape[0], LANES), 1) < SWA_HEAD_DIM
    for c in range(nk // LANES):
        kr = _rope_lanes(acc[:, nq + c * LANES:nq + (c + 1) * LANES], cs, sn, ROPE_HALF)
        swapped = pltpu.roll(kr, SWA_HEAD_DIM, 1)
        placements = (jnp.where(low, kr, 0.0), jnp.where(low, 0.0, swapped),
                      jnp.where(low, swapped, 0.0), jnp.where(low, 0.0, kr))
        for n, slab in enumerate(placements):
            k_ref[:, (4 * c + n) * LANES:(4 * c + n + 1) * LANES] = slab.astype(BF16)
    vt = lax.dot_general(wvt_ref[...], xb, (((1,), (1,)), ((), ())), preferred_element_type=F32)
    ones = jnp.ones((SWA_V_PAD - SWA_HEAD_DIM, xb.shape[0]), BF16)
    for kh in range(SWA_KV_HEADS):
        vt_ref[kh * SWA_V_PAD:kh * SWA_V_PAD + SWA_HEAD_DIM, :] = (
            vt[kh * SWA_HEAD_DIM:(kh + 1) * SWA_HEAD_DIM, :].astype(BF16))
        vt_ref[kh * SWA_V_PAD + SWA_HEAD_DIM:(kh + 1) * SWA_V_PAD, :] = ones


def _swa_qkv(x2, w_qkv, cos_t, sin_t):
    t, d = x2.shape
    nq = SWA_HEADS * SWA_HEAD_DIM
    nk = SWA_KV_HEADS * SWA_HEAD_DIM
    nk2 = SWA_KV_HEADS * 2 * LANES
    nvt = SWA_KV_HEADS * SWA_V_PAD
    tm = min(t, ROW_TILE)
    wb = w_qkv.astype(BF16)
    return pl.pallas_call(
        functools.partial(_swa_qkv_kernel, nq=nq, nk=nk),
        name="swa_qkv",
        grid=(t // tm,),
        in_specs=[pl.BlockSpec((tm, d), lambda i: (i, 0)),
                  pl.BlockSpec((d, nq + nk), lambda i: (0, 0)),
                  pl.BlockSpec((nk, d), lambda i: (0, 0)),
                  pl.BlockSpec((tm, 2 * LANES), lambda i: (i, 0)),
                  pl.BlockSpec((tm, 2 * LANES), lambda i: (i, 0))],
        out_specs=[pl.BlockSpec((tm, nq), lambda i: (i, 0)),
                   pl.BlockSpec((tm, nk2), lambda i: (i, 0)),
                   pl.BlockSpec((nvt, tm), lambda i: (0, i))],
        out_shape=[jax.ShapeDtypeStruct((t, nq), BF16),
                   jax.ShapeDtypeStruct((t, nk2), BF16),
                   jax.ShapeDtypeStruct((nvt, t), BF16)],
        compiler_params=_params(("parallel",)),
    )(x2, wb[:, :nq + nk], jnp.transpose(wb[:, nq + nk:]), cos_t, sin_t)


def _swa_attn_kernel(sink_ref, q_ref, kp_ref, kc_ref, vp_ref, vc_ref, o_ref):
    blk = pl.program_id(1)
    bq = q_ref.shape[0]
    dh = SWA_HEAD_DIM
    slabs = SWA_GROUP // 2
    cols = slabs * bq
    kj = lax.broadcasted_iota(I32, (2 * bq, cols), 0)
    qi = lax.broadcasted_iota(I32, (2 * bq, cols), 1) % bq
    rel = qi + bq - kj
    allowed = (rel >= 0) & (rel < SWA_WINDOW) & ((kj >= bq) | (blk > 0))
    for kh in range(SWA_KV_HEADS):
        base = kh * SWA_GROUP * dh
        qs = jnp.concatenate([q_ref[:, base + s * LANES:base + (s + 1) * LANES] for s in range(slabs)], axis=0)
        vcat = jnp.concatenate([vp_ref[kh * SWA_V_PAD:(kh + 1) * SWA_V_PAD, :],
                                vc_ref[kh * SWA_V_PAD:(kh + 1) * SWA_V_PAD, :]], axis=1)
        for half in range(2):
            ks = (2 * kh + half) * LANES
            kcat = jnp.concatenate([kp_ref[:, ks:ks + LANES], kc_ref[:, ks:ks + LANES]], axis=0)
            heads = [kh * SWA_GROUP + 2 * s + half for s in range(slabs)]
            sink = jnp.concatenate([jnp.full((1, bq), sink_ref[h] * LOG2_E, F32) for h in heads], axis=1)
            logits = lax.dot_general(kcat, qs, (((1,), (1,)), ((), ())), preferred_element_type=F32)
            logits = jnp.where(allowed, logits, NEG_BIG)
            m = jnp.maximum(jnp.max(logits, axis=0, keepdims=True), sink)
            p = jnp.exp2(logits - m).astype(BF16)
            pv = jnp.dot(vcat, p, preferred_element_type=F32)
            inv = 1.0 / (pv[dh:dh + 1, :] + jnp.exp2(sink - m))
            o = (pv[:dh, :] * inv).astype(BF16)
            for s, h in enumerate(heads):
                o_ref[h * dh:(h + 1) * dh, :] = o[:, s * bq:(s + 1) * bq]


def _swa_attn(q, k2, v_t, sinks, batch, seq):
    t, nq = q.shape
    nk2 = k2.shape[1]
    nvt = v_t.shape[0]
    bq = SWA_WINDOW
    nb = seq // bq

    def cur(b, i):
        return (b * nb + i, 0)

    def prev(b, i):
        return (b * nb + jnp.maximum(i - 1, 0), 0)

    def cur_t(b, i):
        return (0, b * nb + i)

    def prev_t(b, i):
        return (0, b * nb + jnp.maximum(i - 1, 0))

    return pl.pallas_call(
        _swa_attn_kernel,
        name="swa_attn",
        grid=(batch, nb),
        in_specs=[pl.BlockSpec(memory_space=pltpu.SMEM),
                  pl.BlockSpec((bq, nq), cur),
                  pl.BlockSpec((bq, nk2), prev),
                  pl.BlockSpec((bq, nk2), cur),
                  pl.BlockSpec((nvt, bq), prev_t),
                  pl.BlockSpec((nvt, bq), cur_t)],
        out_specs=pl.BlockSpec((nq, bq), cur_t),
        out_shape=jax.ShapeDtypeStruct((nq, t), BF16),
        compiler_params=_params(("parallel", "parallel")),
    )(sinks.astype(F32), q, k2, k2, v_t, v_t)


def _proj_ln_kernel(at_ref, w_ref, res_ref, g_ref, b_ref, o_ref, *, alpha):
    mix = lax.dot_general(at_ref[...], w_ref[...], (((0,), (0,)), ((), ())), preferred_element_type=F32)
    o_ref[...] = _layer_norm(alpha * res_ref[...] + mix, g_ref[...], b_ref[...])


def _proj_ln(a_t, w, res, g, b, alpha):
    k, t = a_t.shape
    d = w.shape[1]
    tm = min(t, ROW_TILE)
    return pl.pallas_call(
        functools.partial(_proj_ln_kernel, alpha=alpha),
        name="proj_ln",
        grid=(t // tm,),
        in_specs=[pl.BlockSpec((k, tm), lambda i: (0, i)),
                  pl.BlockSpec((k, d), lambda i: (0, 0)),
                  pl.BlockSpec((tm, d), lambda i: (i, 0)),
                  pl.BlockSpec((1, d), lambda i: (0, 0)),
                  pl.BlockSpec((1, d), lambda i: (0, 0))],
        out_specs=pl.BlockSpec((tm, d), lambda i: (i, 0)),
        out_shape=jax.ShapeDtypeStruct((t, d), F32),
        compiler_params=_params(("parallel",)),
    )(a_t, w.astype(BF16), res, g[None, :], b[None, :])


def _ffn_dense_kernel(x_ref, w1_ref, w3_ref, w2_ref, g_ref, b_ref, o_ref, xb_ref, acc_ref, *, alpha):
    f = pl.program_id(1)

    @pl.when(f == 0)
    def _():
        xb_ref[...] = x_ref[...].astype(BF16)
        acc_ref[...] = jnp.zeros_like(acc_ref)

    xb = xb_ref[...]
    gate = jnp.dot(xb, w1_ref[...], preferred_element_type=F32)
    up = jnp.dot(xb, w3_ref[...], preferred_element_type=F32)
    h = (jax.nn.silu(gate) * up).astype(BF16)
    acc_ref[...] += jnp.dot(h, w2_ref[...], preferred_element_type=F32)

    @pl.when(f == pl.num_programs(1) - 1)
    def _():
        o_ref[...] = _layer_norm(alpha * x_ref[...] + acc_ref[...], g_ref[...], b_ref[...])


def _ffn_tile(f_dim, largest):
    for tf in (largest, 512, 256, 128):
        if f_dim % tf == 0:
            return tf
    return f_dim


def _ffn_dense(x2, w1, w3, w2, g, b, alpha):
    t, d = x2.shape
    f_dim = w1.shape[1]
    tm = min(t, ROW_TILE)
    tf = _ffn_tile(f_dim, FFN_COL_TILE)
    return pl.pallas_call(
        functools.partial(_ffn_dense_kernel, alpha=alpha),
        name="ffn_dense",
        grid=(t // tm, f_dim // tf),
        in_specs=[pl.BlockSpec((tm, d), lambda i, f: (i, 0)),
                  pl.BlockSpec((d, tf), lambda i, f: (0, f)),
                  pl.BlockSpec((d, tf), lambda i, f: (0, f)),
                  pl.BlockSpec((tf, d), lambda i, f: (f, 0)),
                  pl.BlockSpec((1, d), lambda i, f: (0, 0)),
                  pl.BlockSpec((1, d), lambda i, f: (0, 0))],
        out_specs=pl.BlockSpec((tm, d), lambda i, f: (i, 0)),
        out_shape=jax.ShapeDtypeStruct((t, d), F32),
        scratch_shapes=[pltpu.VMEM((tm, d), BF16), pltpu.VMEM((tm, d), F32)],
        compiler_params=_params(("parallel", "arbitrary")),
    )(x2, w1.astype(BF16), w3.astype(BF16), w2.astype(BF16), g[None, :], b[None, :])


def _ple_kernel(x_ref, p_ref, wg_ref, wp_ref, o_ref):
    x = x_ref[...]
    gate = jax.nn.sigmoid(jnp.dot(x.astype(BF16), wg_ref[...], preferred_element_type=F32))
    emb = jnp.dot(p_ref[...].astype(BF16), wp_ref[...], preferred_element_type=F32)
    o_ref[...] = x + gate * emb


def _ple(x2, p2, w_g, w_p):
    t, d = x2.shape
    pd = p2.shape[1]
    tm = min(t, ROW_TILE)
    return pl.pallas_call(
        _ple_kernel,
        name="ple",
        grid=(t // tm,),
        in_specs=[pl.BlockSpec((tm, d), lambda i: (i, 0)),
                  pl.BlockSpec((tm, pd), lambda i: (i, 0)),
                  pl.BlockSpec((d, d), lambda i: (0, 0)),
                  pl.BlockSpec((pd, d), lambda i: (0, 0))],
        out_specs=pl.BlockSpec((tm, d), lambda i: (i, 0)),
        out_shape=jax.ShapeDtypeStruct((t, d), F32),
        compiler_params=_params(("parallel",)),
    )(x2, p2, w_g.astype(BF16), w_p.astype(BF16))


def _dsa_q_kernel(x_ref, w_ref, cos_ref, sin_ref, q_ref):
    xb = x_ref[...].astype(BF16)
    acc = jnp.dot(xb, w_ref[...], preferred_element_type=F32)
    cs = cos_ref[:, :LANES]
    sn = sin_ref[:, :LANES]
    scale = MLA_SCALE * LOG2_E
    for h in range(MLA_HEADS):
        base = h * MLA_QK_PAD
        q_ref[:, base:base + LANES] = (acc[:, base:base + LANES] * scale).astype(BF16)
        a = acc[:, base + LANES:base + 2 * LANES]
        q_ref[:, base + LANES:base + 2 * LANES] = (_rope_lanes(a, cs, sn, ROPE_HALF) * scale).astype(BF16)


def _dsa_q(x2, w_q, cos_t, sin_t):
    t, d = x2.shape
    n = w_q.shape[1]
    tm = min(t, ROW_TILE)
    return pl.pallas_call(
        _dsa_q_kernel,
        name="dsa_q",
        grid=(t // tm,),
        in_specs=[pl.BlockSpec((tm, d), lambda i: (i, 0)),
                  pl.BlockSpec((d, n), lambda i: (0, 0)),
                  pl.BlockSpec((tm, 2 * LANES), lambda i: (i, 0)),
                  pl.BlockSpec((tm, 2 * LANES), lambda i: (i, 0))],
        out_specs=pl.BlockSpec((tm, n), lambda i: (i, 0)),
        out_shape=jax.ShapeDtypeStruct((t, n), BF16),
        compiler_params=_params(("parallel",)),
    )(x2, w_q, cos_t, sin_t)


def _dsa_small_kernel(x_ref, w_ref, cos_ref, sin_ref, kvn_ref, ckv_ref, kr_ref, ki_ref, wi_ref):
    xb = x_ref[...].astype(BF16)
    acc = jnp.dot(xb, w_ref[...], preferred_element_type=F32)
    c = acc[:, :MLA_LATENT]
    ms = jnp.mean(c * c, axis=-1, keepdims=True)
    ckv_ref[...] = (c * lax.rsqrt(ms + RMS_EPS) * kvn_ref[...]).astype(BF16)
    o = MLA_LATENT
    kr_ref[...] = _rope_lanes(acc[:, o:o + LANES], cos_ref[:, :LANES], sin_ref[:, :LANES], ROPE_HALF).astype(BF16)
    ki_ref[...] = _rope_lanes(acc[:, o + LANES:o + 2 * LANES], cos_ref[:, LANES:], sin_ref[:, LANES:],
                              IDX_ROPE // 2).astype(BF16)
    wi_ref[...] = acc[:, o + 2 * LANES:o + 3 * LANES] * (IDX_HEADS ** -0.5)


def _dsa_small(x2, w_small, kv_norm, cos_t, sin_t):
    t, d = x2.shape
    n = w_small.shape[1]
    tm = min(t, ROW_TILE)
    return pl.pallas_call(
        _dsa_small_kernel,
        name="dsa_small",
        grid=(t // tm,),
        in_specs=[pl.BlockSpec((tm, d), lambda i: (i, 0)),
                  pl.BlockSpec((d, n), lambda i: (0, 0)),
                  pl.BlockSpec((tm, 2 * LANES), lambda i: (i, 0)),
                  pl.BlockSpec((tm, 2 * LANES), lambda i: (i, 0)),
                  pl.BlockSpec((1, MLA_LATENT), lambda i: (0, 0))],
        out_specs=[pl.BlockSpec((tm, MLA_LATENT), lambda i: (i, 0)),
                   pl.BlockSpec((tm, LANES), lambda i: (i, 0)),
                   pl.BlockSpec((tm, LANES), lambda i: (i, 0)),
                   pl.BlockSpec((tm, LANES), lambda i: (i, 0))],
        out_shape=[jax.ShapeDtypeStruct((t, MLA_LATENT), BF16),
                   jax.ShapeDtypeStruct((t, LANES), BF16),
                   jax.ShapeDtypeStruct((t, LANES), BF16),
                   jax.ShapeDtypeStruct((t, LANES), F32)],
        compiler_params=_params(("parallel",)),
    )(x2, w_small, cos_t, sin_t, kv_norm[None, :].astype(F32))


def _dsa_qi_kernel(x_ref, w_ref, cos_ref, sin_ref, q_ref):
    xb = x_ref[...].astype(BF16)
    acc = jnp.dot(xb, w_ref[...], preferred_element_type=F32)
    cs = cos_ref[:, LANES:]
    sn = sin_ref[:, LANES:]
    scale = IDX_DIM ** -0.5
    for c in range(acc.shape[1] // LANES):
        a = acc[:, c * LANES:(c + 1) * LANES]
        q_ref[:, c * LANES:(c + 1) * LANES] = (_rope_lanes(a, cs, sn, IDX_ROPE // 2) * scale).astype(BF16)


def _dsa_qi(x2, w_qi, cos_t, sin_t):
    t, d = x2.shape
    n = w_qi.shape[1]
    tm = min(t, ROW_TILE)
    return pl.pallas_call(
        _dsa_qi_kernel,
        name="dsa_qi",
        grid=(t // tm,),
        in_specs=[pl.BlockSpec((tm, d), lambda i: (i, 0)),
                  pl.BlockSpec((d, n), lambda i: (0, 0)),
                  pl.BlockSpec((tm, 2 * LANES), lambda i: (i, 0)),
                  pl.BlockSpec((tm, 2 * LANES), lambda i: (i, 0))],
        out_specs=pl.BlockSpec((tm, n), lambda i: (i, 0)),
        out_shape=jax.ShapeDtypeStruct((t, n), BF16),
        compiler_params=_params(("parallel",)),
    )(x2, w_qi, cos_t, sin_t)


def _dsa_kv_kernel(c_ref, kr_ref, wk_ref, wvt_ref, k_ref, vt_ref):
    c = c_ref[...]
    kn = jnp.dot(c, wk_ref[...], preferred_element_type=F32)
    kr = kr_ref[...]
    for h in range(MLA_HEADS):
        base = h * MLA_QK_PAD
        k_ref[:, base:base + LANES] = kn[:, h * LANES:(h + 1) * LANES].astype(BF16)
        k_ref[:, base + LANES:base + 2 * LANES] = kr
    vt = lax.dot_general(wvt_ref[...], c, (((1,), (1,)), ((), ())), preferred_element_type=F32)
    ones = jnp.ones((MLA_V_PAD - MLA_V, c.shape[0]), BF16)
    for h in range(MLA_HEADS):
        vt_ref[h * MLA_V_PAD:h * MLA_V_PAD + MLA_V, :] = vt[h * MLA_V:(h + 1) * MLA_V, :].astype(BF16)
        vt_ref[h * MLA_V_PAD + MLA_V:(h + 1) * MLA_V_PAD, :] = ones


def _dsa_kv(ckv, kr, w_uk_all, w_uv_t):
    t, c_dim = ckv.shape
    nk = MLA_HEADS * MLA_QK_PAD
    nv = w_uv_t.shape[0]
    nv_pad = MLA_HEADS * MLA_V_PAD
    tm = min(t, ROW_TILE)
    return pl.pallas_call(
        _dsa_kv_kernel,
        name="dsa_kv",
        grid=(t // tm,),
        in_specs=[pl.BlockSpec((tm, c_dim), lambda i: (i, 0)),
                  pl.BlockSpec((tm, LANES), lambda i: (i, 0)),
                  pl.BlockSpec((c_dim, MLA_HEADS * MLA_NOPE), lambda i: (0, 0)),
                  pl.BlockSpec((nv, c_dim), lambda i: (0, 0))],
        out_specs=[pl.BlockSpec((tm, nk), lambda i: (i, 0)),
                   pl.BlockSpec((nv_pad, tm), lambda i: (0, i))],
        out_shape=[jax.ShapeDtypeStruct((t, nk), BF16),
                   jax.ShapeDtypeStruct((nv_pad, t), BF16)],
        compiler_params=_params(("parallel",)),
    )(ckv, kr, w_uk_all, w_uv_t)


def _float_sort_key(s):
    bits = lax.bitcast_convert_type(s, I32)
    return bits ^ ((bits >> 31) & jnp.int32(0x7FFFFFFF))


def _dsa_select_kernel(k_ref, q_ref, w_ref, bias_ref, key_ref, tie_ref, *, sc, cc, unroll, n_sel):
    j = pl.program_id(1)
    s_len, tq = key_ref.shape
    n_chunks = s_len // sc
    q_lo = j * tq
    n_need = (q_lo + tq + sc * unroll - 1) // (sc * unroll) * unroll
    qt = q_ref[...]
    w_row = w_ref[...]
    q_pos = q_lo + lax.broadcasted_iota(I32, (sc, tq), 1)
    k_off = lax.broadcasted_iota(I32, (sc, tq), 0)
    min_key = jnp.int32(-2 ** 31)

    def score_chunks(i, carry):
        for u in range(unroll):
            row0 = pl.multiple_of((i * unroll + u) * sc, sc)
            kc = k_ref[pl.ds(row0, sc), :]
            d = jnp.dot(kc, qt, preferred_element_type=F32)
            s = jnp.zeros((sc, tq), F32)
            for h in range(IDX_HEADS):
                s = s + jnp.maximum(d[:, h * tq:(h + 1) * tq], 0.0) * w_row[:, h * tq:(h + 1) * tq]
            causal = (row0 + k_off) <= q_pos
            key_ref[pl.ds(row0, sc), :] = jnp.where(causal, _float_sort_key(s), min_key)
        return carry

    lax.fori_loop(0, n_need // unroll, score_chunks, 0)

    n_cnt = (n_need * sc + cc - 1) // cc

    def fill_chunk(c, carry):
        row0 = pl.multiple_of(c * sc, sc)
        key_ref[pl.ds(row0, sc), :] = jnp.full((sc, tq), min_key, I32)
        return carry

    lax.fori_loop(n_need, n_cnt * (cc // sc), fill_chunk, 0)

    def count_ge(cand):
        def body(c, cnt):
            row0 = pl.multiple_of(c * cc, cc)
            hit = jnp.where(key_ref[pl.ds(row0, cc), :] >= cand, 1, 0).astype(I32)
            return cnt + jnp.sum(hit.reshape(cc // 32, 32, tq), axis=0)

        cnt = lax.fori_loop(0, n_cnt, body, jnp.zeros((32, tq), I32))
        return jnp.sum(cnt, axis=0, keepdims=True)

    zero = jnp.zeros((1, tq), I32)
    c0 = count_ge(zero)
    thr = jnp.where(c0 >= n_sel, zero, jnp.full((1, tq), min_key, I32))
    cnt_thr = jnp.where(c0 >= n_sel, c0, 0)

    def bit_step(it, carry):
        thr, cnt_thr = carry
        cand = thr + (jnp.int32(1) << (30 - it))
        cnt = count_ge(cand)
        take = cnt >= n_sel
        return jnp.where(take, cand, thr), jnp.where(take, cnt, cnt_thr)

    thr, cnt_thr = lax.fori_loop(0, 31, bit_step, (thr, cnt_thr))

    excess = cnt_thr - n_sel
    tie_ref[0:1, :] = jnp.full((1, tq), s_len, I32)

    @pl.when(jnp.max(excess) > 0)
    def _():
        def count_tie_below(x):
            def body(c, cnt):
                row0 = pl.multiple_of(c * cc, cc)
                idx = row0 + lax.broadcasted_iota(I32, (cc, tq), 0)
                hit = jnp.where((key_ref[pl.ds(row0, cc), :] == thr) & (idx < x), 1, 0).astype(I32)
                return cnt + jnp.sum(hit.reshape(cc // 32, 32, tq), axis=0)

            cnt = lax.fori_loop(0, n_cnt, body, jnp.zeros((32, tq), I32))
            return jnp.sum(cnt, axis=0, keepdims=True)

        n_tie = count_tie_below(jnp.full((1, tq), s_len, I32))
        need = n_sel - (cnt_thr - n_tie)
        n_bits = s_len.bit_length()

        def idx_step(it, x):
            cand = x + (jnp.int32(1) << (n_bits - 1 - it))
            return jnp.where(count_tie_below(cand) <= need, cand, x)

        x = lax.fori_loop(0, n_bits, idx_step, zero)
        tie_ref[0:1, :] = jnp.where(excess > 0, x, s_len)

    tie_below = tie_ref[0:1, :]
    thr_low = jnp.maximum(thr, min_key + 1)

    def write_chunk(c, carry):
        row0 = pl.multiple_of(c * sc, sc)
        thr_here = thr_low + jnp.where(k_off >= tie_below - row0, 1, 0)
        keep = key_ref[pl.ds(row0, sc), :] >= thr_here
        bias_ref[pl.ds(row0, sc), :] = jnp.where(keep, 0.0, NEG_BIG).astype(F32)
        return carry

    def mask_chunk(c, carry):
        row0 = pl.multiple_of(c * sc, sc)
        bias_ref[pl.ds(row0, sc), :] = jnp.full((sc, tq), NEG_BIG, F32)
        return carry

    lax.fori_loop(0, n_need, write_chunk, 0)
    lax.fori_loop(n_need, n_chunks, mask_chunk, 0)


def _dsa_select(k_i, q_i_t, w_rows, batch, seq, n_sel):
    tq = LANES
    nq = seq // tq
    sc = min(seq, SELECT_SCORE_ROWS)
    cc = min(seq, SELECT_COUNT_ROWS)
    return pl.pallas_call(
        functools.partial(_dsa_select_kernel, sc=sc, cc=cc, unroll=max(u for u in (4, 2, 1) if (seq // sc) % u == 0), n_sel=n_sel),
        name="dsa_select",
        grid=(batch, nq),
        in_specs=[pl.BlockSpec((None, seq, IDX_DIM), lambda b, j: (b, 0, 0)),
                  pl.BlockSpec((None, None, IDX_DIM, IDX_HEADS * tq), lambda b, j: (b, j, 0, 0)),
                  pl.BlockSpec((None, None, 1, IDX_HEADS * tq), lambda b, j: (b, j, 0, 0))],
        out_specs=pl.BlockSpec((None, seq, tq), lambda b, j: (b, 0, j)),
        out_shape=jax.ShapeDtypeStruct((batch, seq, seq), F32),
        scratch_shapes=[pltpu.VMEM((seq, tq), I32), pltpu.VMEM((8, tq), I32)],
        compiler_params=_params(("parallel", "parallel")),
    )(k_i, q_i_t, w_rows)


def _dsa_attn_kernel(qt_ref, k_ref, vta_ref, vtb_ref, bias_ref, o_ref, m_ref, a_ref, *head_refs, sc):
    s_refs = head_refs[:MLA_HEADS]
    acc_refs = head_refs[MLA_HEADS:]
    j = pl.program_id(1)
    g = pl.program_id(2)
    tq = qt_ref.shape[1]
    last_pair = (j * tq + tq - 1) // (2 * sc)

    @pl.when(g == 0)
    def _():
        for h in range(MLA_HEADS):
            acc_refs[h][...] = jnp.zeros_like(acc_refs[h])
            s_refs[h][1] = jnp.full((sc, tq), NEG_BIG, F32)
        m_ref[...] = jnp.full_like(m_ref, M_INIT)
        a_ref[...] = jnp.ones_like(a_ref)

    def finish(slot, vt_ref, m_now, a_now):
        for h in range(MLA_HEADS):
            p = jnp.exp2(s_refs[h][slot] - m_now[h:h + 1, :]).astype(BF16)
            pv = jnp.dot(vt_ref[h * MLA_V_PAD:(h + 1) * MLA_V_PAD, :], p, preferred_element_type=F32)
            acc_refs[h][...] = a_now[h:h + 1, :] * acc_refs[h][...] + pv

    def score(slot, half, m_now):
        rows = pl.ds(half * sc, sc)
        chunk_max = []
        for h in range(MLA_HEADS):
            qh = qt_ref[h * MLA_QK_PAD:(h + 1) * MLA_QK_PAD, :]
            kh = k_ref[rows, h * MLA_QK_PAD:(h + 1) * MLA_QK_PAD]
            s = jnp.dot(kh, qh, preferred_element_type=F32) + bias_ref[rows, :]
            s_refs[h][slot] = s
            chunk_max.append(jnp.max(s, axis=0, keepdims=True))
        m_new = jnp.maximum(m_now, jnp.concatenate(chunk_max, axis=0))
        return m_new, jnp.exp2(m_now - m_new)

    @pl.when(g <= last_pair)
    def _():
        m0 = m_ref[...]
        finish(1, vta_ref, m0, a_ref[...])
        m1, a1 = score(0, 0, m0)
        finish(0, vtb_ref, m1, a1)
        m2, a2 = score(1, 1, m1)
        m_ref[...] = m2
        a_ref[...] = a2

    @pl.when(g == last_pair + 1)
    def _():
        finish(1, vta_ref, m_ref[...], a_ref[...])
        for h in range(MLA_HEADS):
            inv = 1.0 / acc_refs[h][MLA_V:MLA_V + 1, :]
            o_ref[h * MLA_V:(h + 1) * MLA_V, :] = (acc_refs[h][:MLA_V, :] * inv).astype(BF16)


def _dsa_attn(q_cat_t, k_cat, v_t, bias, batch, seq):
    nqk, t = q_cat_t.shape
    nv_pad = v_t.shape[0]
    nv = MLA_HEADS * MLA_V
    tq = min(seq, ATTN_QUERY_TILE)
    sc = min(ATTN_KEY_CHUNK, tq // 2)
    nq = seq // tq
    nc = seq // sc
    npair = nc // 2

    def last_pair(j):
        return (j * tq + tq - 1) // (2 * sc)

    return pl.pallas_call(
        functools.partial(_dsa_attn_kernel, sc=sc),
        name="dsa_attn",
        grid=(batch, nq, npair + 1),
        in_specs=[pl.BlockSpec((nqk, tq), lambda b, j, g: (0, b * nq + j)),
                  pl.BlockSpec((2 * sc, nqk), lambda b, j, g: (b * npair + jnp.minimum(g, last_pair(j)), 0)),
                  pl.BlockSpec((nv_pad, sc), lambda b, j, g: (0, b * nc + jnp.clip(2 * g - 1, 0, 2 * last_pair(j) + 1))),
                  pl.BlockSpec((nv_pad, sc), lambda b, j, g: (0, b * nc + jnp.minimum(2 * g, 2 * last_pair(j)))),
                  pl.BlockSpec((None, 2 * sc, tq), lambda b, j, g: (b, jnp.minimum(g, last_pair(j)), j))],
        out_specs=pl.BlockSpec((nv, tq), lambda b, j, g: (0, b * nq + j)),
        out_shape=jax.ShapeDtypeStruct((nv, t), BF16),
        scratch_shapes=([pltpu.VMEM((MLA_HEADS, tq), F32), pltpu.VMEM((MLA_HEADS, tq), F32)]
                        + [pltpu.VMEM((2, sc, tq), F32)] * MLA_HEADS
                        + [pltpu.VMEM((MLA_V_PAD, tq), F32)] * MLA_HEADS),
        compiler_params=_params(("parallel", "parallel", "arbitrary")),
    )(q_cat_t, k_cat, v_t, v_t, bias)


def _dsa_mixer_out(x2, cos_t, sin_t, w_in, kv_norm, w_uk, w_uv, batch, seq):
    t, d = x2.shape
    nq = MLA_HEADS * (MLA_NOPE + MLA_ROPE)
    o_c = nq
    o_kr = o_c + MLA_LATENT
    o_qi = o_kr + MLA_ROPE
    o_ki = o_qi + IDX_HEADS * IDX_DIM
    o_wi = o_ki + IDX_DIM
    wb = w_in.astype(BF16)
    wq = wb[:, :nq].reshape(d, MLA_HEADS, MLA_NOPE + MLA_ROPE)
    wq = jnp.pad(wq, ((0, 0), (0, 0), (0, MLA_QK_PAD - MLA_NOPE - MLA_ROPE))).reshape(d, MLA_HEADS * MLA_QK_PAD)
    pad_to_lanes = lambda w: jnp.pad(w, ((0, 0), (0, LANES - w.shape[1])))
    w_small = jnp.concatenate([wb[:, o_c:o_kr], pad_to_lanes(wb[:, o_kr:o_qi]), pad_to_lanes(wb[:, o_ki:o_wi]),
                               pad_to_lanes(wb[:, o_wi:])], axis=1)
    w_qi = wb[:, o_qi:o_ki]
    w_uk_all = jnp.transpose(w_uk, (1, 0, 2)).reshape(MLA_LATENT, MLA_HEADS * MLA_NOPE).astype(BF16)
    w_uv_t = jnp.transpose(w_uv, (0, 2, 1)).reshape(MLA_HEADS * MLA_V, MLA_LATENT).astype(BF16)

    q_cat = _dsa_q(x2, wq, cos_t, sin_t)
    ckv, kr, ki, wi = _dsa_small(x2, w_small, kv_norm, cos_t, sin_t)
    qi = _dsa_qi(x2, w_qi, cos_t, sin_t)
    k_cat, v_t = _dsa_kv(ckv, kr, w_uk_all, w_uv_t)

    tq = LANES
    nqb = seq // tq
    k_i = ki[:, :IDX_DIM].reshape(batch, seq, IDX_DIM)
    q_i_t = jnp.transpose(qi.reshape(batch, nqb, tq, IDX_HEADS, IDX_DIM), (0, 1, 4, 3, 2))
    q_i_t = q_i_t.reshape(batch, nqb, IDX_DIM, IDX_HEADS * tq)
    w_rows = jnp.transpose(wi[:, :IDX_HEADS].reshape(batch, nqb, tq, IDX_HEADS), (0, 1, 3, 2))
    w_rows = w_rows.reshape(batch, nqb, 1, IDX_HEADS * tq)
    n_sel = min(DSA_TOPK, seq // 4)
    bias = _dsa_select(k_i, q_i_t, w_rows, batch, seq, n_sel)
    o_t = _dsa_attn(jnp.transpose(q_cat), k_cat, v_t, bias, batch, seq)
    return o_t


def _router_kernel(x_ref, w_ref, idx_ref, gate_ref):
    logits = jnp.dot(x_ref[...], w_ref[...], precision=lax.Precision.HIGHEST, preferred_element_type=F32)
    lane = lax.broadcasted_iota(I32, logits.shape, 1)
    lg = jnp.where(lane < MOE_EXPERTS, logits, -jnp.inf)
    v1 = jnp.max(lg, axis=-1, keepdims=True)
    i1 = jnp.min(jnp.where(lg == v1, lane, LANES), axis=-1, keepdims=True)
    lg2 = jnp.where(lane == i1, -jnp.inf, lg)
    v2 = jnp.max(lg2, axis=-1, keepdims=True)
    i2 = jnp.min(jnp.where(lg2 == v2, lane, LANES), axis=-1, keepdims=True)
    e2 = jnp.exp(v2 - v1)
    den = 1.0 + e2
    idx_ref[...] = jnp.where(lane == 0, i1, jnp.where(lane == 1, i2, 0))
    gate_ref[...] = jnp.where(lane == 0, 1.0 / den, jnp.where(lane == 1, e2 / den, 0.0))


def _router(x2, w_router):
    t, d = x2.shape
    wr = jnp.pad(w_router.astype(F32), ((0, 0), (0, LANES - w_router.shape[1])))
    tm = min(t, ROW_TILE)
    return pl.pallas_call(
        _router_kernel,
        name="moe_router",
        grid=(t // tm,),
        in_specs=[pl.BlockSpec((tm, d), lambda i: (i, 0)),
                  pl.BlockSpec((d, LANES), lambda i: (0, 0))],
        out_specs=[pl.BlockSpec((tm, LANES), lambda i: (i, 0)),
                   pl.BlockSpec((tm, LANES), lambda i: (i, 0))],
        out_shape=[jax.ShapeDtypeStruct((t, LANES), I32),
                   jax.ShapeDtypeStruct((t, LANES), F32)],
        compiler_params=_params(("parallel",)),
    )(x2, wr)


def _row_copy(src_hbm, src_row, dst, dst_row, sem):
    return pltpu.make_async_copy(src_hbm.at[pl.ds(src_row, 1)], dst.at[pl.ds(dst_row, 1)], sem)


def _moe_ffn_kernel(be_ref, nu_ref, tok_ref, tok_next_ref, x_hbm, w1_ref, w3_ref, w2_ref, o_ref,
                    xg_ref, xb_ref, acc_ref, sem):
    i = pl.program_id(0)
    f = pl.program_id(1)
    nf = pl.num_programs(1)
    n_used = nu_ref[0]
    used = i < n_used
    tm = xb_ref.shape[0]
    slot = i % 2

    def start_gather(ids_ref, buf):
        def body(r, carry):
            _row_copy(x_hbm, ids_ref[r], xg_ref.at[buf], r, sem.at[buf]).start()
            return carry

        lax.fori_loop(0, tm, body, 0, unroll=8)

    @pl.when(used & (f == 0))
    def _():
        @pl.when(i == 0)
        def _():
            start_gather(tok_ref, 0)

        pltpu.make_async_copy(x_hbm.at[pl.ds(0, tm)], xg_ref.at[slot], sem.at[slot]).wait()

        @pl.when(i + 1 < n_used)
        def _():
            start_gather(tok_next_ref, 1 - slot)

        xb_ref[...] = xg_ref[slot].astype(BF16)
        acc_ref[...] = jnp.zeros_like(acc_ref)

    @pl.when(used)
    def _():
        xb = xb_ref[...]
        gate = jnp.dot(xb, w1_ref[...], preferred_element_type=F32)
        up = jnp.dot(xb, w3_ref[...], preferred_element_type=F32)
        h = (jax.nn.silu(gate) * up).astype(BF16)
        acc_ref[...] += jnp.dot(h, w2_ref[...], preferred_element_type=F32)

    @pl.when(used & (f == nf - 1))
    def _():
        o_ref[...] = acc_ref[...]

    @pl.when(jnp.logical_not(used) & (f == nf - 1))
    def _():
        o_ref[...] = jnp.zeros_like(o_ref)


def _moe_ffn(x2, slot_tok, w1, w3, w2, blk_expert, n_used, tm):
    n_slots = slot_tok.shape[0]
    d = x2.shape[1]
    f_dim = w1.shape[2]
    tf = _ffn_tile(f_dim, MOE_COL_TILE)
    nf = f_dim // tf
    nt = n_slots // tm

    def f_sel(i, f, nu):
        return jnp.where(i < nu[0], f, nf - 1)

    grid_spec = pltpu.PrefetchScalarGridSpec(
        num_scalar_prefetch=2,
        grid=(nt, nf),
        in_specs=[pl.BlockSpec((tm,), lambda i, f, be, nu: (i,), memory_space=pltpu.SMEM),
                  pl.BlockSpec((tm,), lambda i, f, be, nu: (jnp.minimum(i + 1, nt - 1),), memory_space=pltpu.SMEM),
                  pl.BlockSpec(memory_space=pl.ANY),
                  pl.BlockSpec((None, d, tf), lambda i, f, be, nu: (be[i], 0, f_sel(i, f, nu))),
                  pl.BlockSpec((None, d, tf), lambda i, f, be, nu: (be[i], 0, f_sel(i, f, nu))),
                  pl.BlockSpec((None, tf, d), lambda i, f, be, nu: (be[i], f_sel(i, f, nu), 0))],
        out_specs=pl.BlockSpec((tm, d), lambda i, f, be, nu: (i, 0)),
        scratch_shapes=[pltpu.VMEM((2, tm, d), F32), pltpu.VMEM((tm, d), BF16), pltpu.VMEM((tm, d), F32),
                        pltpu.SemaphoreType.DMA((2,))],
    )
    return pl.pallas_call(
        _moe_ffn_kernel,
        name="moe_ffn",
        grid_spec=grid_spec,
        out_shape=jax.ShapeDtypeStruct((n_slots, d), F32),
        compiler_params=_params(("arbitrary", "arbitrary")),
    )(blk_expert, n_used, slot_tok, slot_tok, x2, w1, w3, w2)


def _moe_combine_kernel(dest_ref, dest_next_ref, y_hbm, x_ref, gate_ref, g_ref, b_ref, o_ref, y_ref, sem, *,
                        alpha, tb):
    i = pl.program_id(0)
    slot = i % 2

    def start_gather(ids_ref, buf):
        def body(tok, carry):
            for k in range(MOE_TOP_K):
                _row_copy(y_hbm, ids_ref[MOE_TOP_K * tok + k], y_ref.at[buf, k], tok, sem.at[buf]).start()
            return carry

        lax.fori_loop(0, tb, body, 0, unroll=4)

    @pl.when(i == 0)
    def _():
        start_gather(dest_ref, 0)

    for k in range(MOE_TOP_K):
        pltpu.make_async_copy(y_hbm.at[pl.ds(0, tb)], y_ref.at[slot, k], sem.at[slot]).wait()

    @pl.when(i + 1 < pl.num_programs(0))
    def _():
        start_gather(dest_next_ref, 1 - slot)

    gates = gate_ref[...]
    ff = gates[:, 0:1] * y_ref[slot, 0] + gates[:, 1:2] * y_ref[slot, 1]
    o_ref[...] = _layer_norm(alpha * x_ref[...] + ff, g_ref[...], b_ref[...])


def _moe_combine(x2, y, dest, gates, g, b, alpha):
    t, d = x2.shape
    tb = min(t, COMBINE_ROW_TILE)
    nt = t // tb
    return pl.pallas_call(
        functools.partial(_moe_combine_kernel, alpha=alpha, tb=tb),
        name="moe_combine",
        grid=(nt,),
        in_specs=[pl.BlockSpec((tb * MOE_TOP_K,), lambda i: (i,), memory_space=pltpu.SMEM),
                  pl.BlockSpec((tb * MOE_TOP_K,), lambda i: (jnp.minimum(i + 1, nt - 1),), memory_space=pltpu.SMEM),
                  pl.BlockSpec(memory_space=pl.ANY),
                  pl.BlockSpec((tb, d), lambda i: (i, 0)),
                  pl.BlockSpec((tb, LANES), lambda i: (i, 0)),
                  pl.BlockSpec((1, d), lambda i: (0, 0)),
                  pl.BlockSpec((1, d), lambda i: (0, 0))],
        out_specs=pl.BlockSpec((tb, d), lambda i: (i, 0)),
        out_shape=jax.ShapeDtypeStruct((t, d), F32),
        scratch_shapes=[pltpu.VMEM((2, MOE_TOP_K, tb, d), F32), pltpu.SemaphoreType.DMA((2,))],
        compiler_params=_params(("arbitrary",)),
    )(dest, dest, y, x2, gates, g[None, :], b[None, :])


def _moe_layer(x2, w_router, w1, w3, w2, g, b, alpha):
    t, d = x2.shape
    n = t * MOE_TOP_K
    tm = min(ROW_TILE, t)
    idx_pad, gates = _router(x2, w_router)
    e_flat = idx_pad[:, :MOE_TOP_K].reshape(n)
    flat_ids = jnp.arange(n, dtype=I32)
    _, order = lax.sort((e_flat, flat_ids), num_keys=1, is_stable=True)
    _, inv_order = lax.sort((order, flat_ids), num_keys=1)
    experts = jnp.arange(MOE_EXPERTS, dtype=I32)
    counts = jnp.sum((e_flat[:, None] == experts[None, :]).astype(I32), axis=0)
    starts = jnp.cumsum(counts) - counts
    padded = (counts + tm - 1) // tm * tm
    pends = jnp.cumsum(padded)
    pstarts = pends - padded
    n_tiles = (n + MOE_EXPERTS * (tm - 1) + tm - 1) // tm
    n_used = (pends[-1] // tm).astype(I32)
    blk_start = jnp.arange(n_tiles, dtype=I32) * tm
    blk_expert = jnp.minimum(jnp.sum((pends[None, :] <= blk_start[:, None]).astype(I32), axis=1), MOE_EXPERTS - 1)
    blk_expert = jnp.where(jnp.arange(n_tiles) < n_used, blk_expert, blk_expert[jnp.maximum(n_used - 1, 0)])
    n_slots = n_tiles * tm
    slot = jnp.arange(n_slots, dtype=I32)
    slot_e = jnp.repeat(blk_expert, tm)
    shift = pstarts - starts
    order_tok = jnp.pad(order // MOE_TOP_K, (0, n_slots - n))
    slot_tok = jnp.zeros((n_slots,), I32)
    for e in range(MOE_EXPERTS):
        in_group = (slot_e == e) & (slot - pstarts[e] < counts[e]) & (slot < n_used * tm)
        slot_tok = jnp.where(in_group, jnp.roll(order_tok, shift[e]), slot_tok)
    dest = (inv_order + jnp.sum(jnp.where(e_flat[:, None] == experts[None, :], shift[None, :], 0), axis=1)).astype(I32)

    y = _moe_ffn(x2, slot_tok, w1.astype(BF16), w3.astype(BF16), w2.astype(BF16), blk_expert.astype(I32),
                 n_used.reshape(1), tm)
    return _moe_combine(x2, y, dest, gates, g, b, alpha)


def kernel(x, p, positions, swa_w_qkv, swa_sinks, swa_w_o, dense_w1, dense_w3, dense_w2, dsa_w_in, dsa_kv_norm,
           dsa_w_uk, dsa_w_uv, dsa_w_o, moe_router, moe_w1, moe_w3, moe_w2, ln_g, ln_b, ple_w_p, ple_w_g):
    batch, seq, d = x.shape
    depth = ln_g.shape[0]
    alpha = (2.0 * depth) ** 0.25
    t = batch * seq
    assert seq % (2 * LANES) == 0 and d % LANES == 0, (seq, d)
    assert t % min(t, ROPE_ROW_TILE) == 0, t
    x2 = x.reshape(t, d)
    cos_t, sin_t = _rope_tables(positions)
    for i in range(depth):
        j = i // 2
        if i % 2 == 0:
            q, k2, v_t = _swa_qkv(x2, swa_w_qkv[j], cos_t, sin_t)
            o = _swa_attn(q, k2, v_t, swa_sinks[j], batch, seq)
            x2 = _proj_ln(o, swa_w_o[j], x2, ln_g[i, 0], ln_b[i, 0], alpha)
            x2 = _ffn_dense(x2, dense_w1[j], dense_w3[j], dense_w2[j], ln_g[i, 1], ln_b[i, 1], alpha)
        else:
            o = _dsa_mixer_out(x2, cos_t, sin_t, dsa_w_in[j], dsa_kv_norm[j], dsa_w_uk[j], dsa_w_uv[j], batch, seq)
            x2 = _proj_ln(o, dsa_w_o[j], x2, ln_g[i, 0], ln_b[i, 0], alpha)
            x2 = _moe_layer(x2, moe_router[j], moe_w1[j], moe_w3[j], moe_w2[j], ln_g[i, 1], ln_b[i, 1], alpha)
        x2 = _ple(x2, p[i].reshape(t, -1), ple_w_g[i], ple_w_p[i])
    return x2.reshape(batch, seq, d)
```

```python
import functools

import jax
import jax.numpy as jnp
from jax import lax
from jax.experimental import pallas as pl
from jax.experimental.pallas import tpu as pltpu

F32 = jnp.float32
BF16 = jnp.bfloat16
I32 = jnp.int32

LANES = 128
ROPE_THETA = 10000.0
LN_EPS = 1e-5
RMS_EPS = 1e-6
SWA_HEADS = 32
SWA_KV_HEADS = 4
SWA_GROUP = SWA_HEADS // SWA_KV_HEADS
SWA_HEAD_DIM = 64
SWA_WINDOW = 128
SWA_V_PAD = SWA_HEAD_DIM + 16
MLA_HEADS = 16
MLA_NOPE = 128
MLA_ROPE = 64
MLA_V = 128
MLA_LATENT = 512
MLA_SCALE = (MLA_NOPE + MLA_ROPE) ** -0.5
MLA_QK_PAD = 256
MLA_V_PAD = MLA_V + 16
LOG2_E = 1.4426950408889634
IDX_HEADS = 16
IDX_DIM = 64
IDX_ROPE = 32
DSA_TOPK = 256
MOE_EXPERTS = 8
MOE_TOP_K = 2
NEG_BIG = -1e30
M_INIT = -1e20
VMEM_LIMIT = 56 * 1024 * 1024
ROW_TILE = 512
ROPE_ROW_TILE = 1024
FFN_COL_TILE = 512
MOE_COL_TILE = 1024
SWIGLU_COL_CHUNK = 256
COMBINE_ROW_TILE = 256
SELECT_SCORE_ROWS = 256
SELECT_COUNT_ROWS = 1024
ATTN_QUERY_TILE = 512
ATTN_KEY_CHUNK = 256
ROPE_HALF = 32


def _params(sem):
    return pltpu.CompilerParams(dimension_semantics=sem, vmem_limit_bytes=VMEM_LIMIT)


def _layer_norm(y, g, b):
    mu = jnp.mean(y, axis=-1, keepdims=True)
    d = y - mu
    var = jnp.mean(d * d, axis=-1, keepdims=True)
    return d * lax.rsqrt(var + LN_EPS) * g + b


def _rope_lanes(a, cs, sn, half):
    lane = lax.broadcasted_iota(I32, a.shape, 1)
    first = (lane % 64) < half
    rot = jnp.where(first, pltpu.roll(a, LANES - half, 1), pltpu.roll(a, half, 1))
    return a * cs + rot * sn


def _rope_table_kernel(pos_ref, invf_ref, sgn_ref, cos_ref, sin_ref):
    ang = pos_ref[...].astype(F32) * invf_ref[...]
    cos_ref[...] = jnp.cos(ang)
    sin_ref[...] = jnp.sin(ang) * sgn_ref[...]


def _rope_tables(positions):
    t = positions.size
    pos = positions.reshape(t, 1).astype(I32)
    lane = jnp.arange(LANES)
    in64 = lane % 64
    f64 = ROPE_THETA ** (-jnp.arange(0, 64, 2, dtype=F32) / 64)
    f32_ = ROPE_THETA ** (-jnp.arange(0, IDX_ROPE, 2, dtype=F32) / IDX_ROPE)
    invf_a = f64[in64 % 32]
    sgn_a = jnp.where(in64 < 32, -1.0, 1.0).astype(F32)
    invf_b = jnp.where(in64 < IDX_ROPE, f32_[in64 % (IDX_ROPE // 2)], 0.0)
    sgn_b = jnp.where(in64 < IDX_ROPE // 2, -1.0, 1.0).astype(F32)
    invf = jnp.concatenate([invf_a, invf_b])[None, :]
    sgn = jnp.concatenate([sgn_a, sgn_b])[None, :]
    tm = min(t, ROPE_ROW_TILE)
    return pl.pallas_call(
        _rope_table_kernel,
        name="rope_tables",
        grid=(t // tm,),
        in_specs=[pl.BlockSpec((tm, 1), lambda i: (i, 0)),
                  pl.BlockSpec((1, 2 * LANES), lambda i: (0, 0)),
                  pl.BlockSpec((1, 2 * LANES), lambda i: (0, 0))],
        out_specs=[pl.BlockSpec((tm, 2 * LANES), lambda i: (i, 0)),
                   pl.BlockSpec((tm, 2 * LANES), lambda i: (i, 0))],
        out_shape=[jax.ShapeDtypeStruct((t, 2 * LANES), F32)] * 2,
        compiler_params=_params(("parallel",)),
    )(pos, invf, sgn)


def _swa_qkv_kernel(x_ref, w_ref, wvt_ref, cos_ref, sin_ref, q_ref, k_ref, vt_ref, *, nq, nk):
    xb = x_ref[...].astype(BF16)
    acc = jnp.dot(xb, w_ref[...], preferred_element_type=F32)
    cs = cos_ref[:, :LANES]
    sn = sin_ref[:, :LANES]
    scale = SWA_HEAD_DIM ** -0.5 * LOG2_E
    for c in range(nq // LANES):
        a = acc[:, c * LANES:(c + 1) * LANES]
        q_ref[:, c * LANES:(c + 1) * LANES] = (_rope_lanes(a, cs, sn, ROPE_HALF) * scale).astype(BF16)
    low = lax.broadcasted_iota(I32, (acc.shape[0], LANES), 1) < SWA_HEAD_DIM
    for c in range(nk // LANES):
        kr = _rope_lanes(acc[:, nq + c * LANES:nq + (c + 1) * LANES], cs, sn, ROPE_HALF)
        swapped = pltpu.roll(kr, SWA_HEAD_DIM, 1)
        placements = (jnp.where(low, kr, 0.0), jnp.where(low, 0.0, swapped),
                      jnp.where(low, swapped, 0.0), jnp.where(low, 0.0, kr))
        for n, slab in enumerate(placements):
            k_ref[:, (4 * c + n) * LANES:(4 * c + n + 1) * LANES] = slab.astype(BF16)
    vt = lax.dot_general(wvt_ref[...], xb, (((1,), (1,)), ((), ())), preferred_element_type=F32)
    ones = jnp.ones((SWA_V_PAD - SWA_HEAD_DIM, xb.shape[0]), BF16)
    for kh in range(SWA_KV_HEADS):
        vt_ref[kh * SWA_V_PAD:kh * SWA_V_PAD + SWA_HEAD_DIM, :] = (
            vt[kh * SWA_HEAD_DIM:(kh + 1) * SWA_HEAD_DIM, :].astype(BF16))
        vt_ref[kh * SWA_V_PAD + SWA_HEAD_DIM:(kh + 1) * SWA_V_PAD, :] = ones


def _swa_qkv(x2, w_qkv, cos_t, sin_t):
    t, d = x2.shape
    nq = SWA_HEADS * SWA_HEAD_DIM
    nk = SWA_KV_HEADS * SWA_HEAD_DIM
    nk2 = SWA_KV_HEADS * 2 * LANES
    nvt = SWA_KV_HEADS * SWA_V_PAD
    tm = min(t, ROW_TILE)
    wb = w_qkv.astype(BF16)
    return pl.pallas_call(
        functools.partial(_swa_qkv_kernel, nq=nq, nk=nk),
        name="swa_qkv",
        grid=(t // tm,),
        in_specs=[pl.BlockSpec((tm, d), lambda i: (i, 0)),
                  pl.BlockSpec((d, nq + nk), lambda i: (0, 0)),
                  pl.BlockSpec((nk, d), lambda i: (0, 0)),
                  pl.BlockSpec((tm, 2 * LANES), lambda i: (i, 0)),
                  pl.BlockSpec((tm, 2 * LANES), lambda i: (i, 0))],
        out_specs=[pl.BlockSpec((tm, nq), lambda i: (i, 0)),
                   pl.BlockSpec((tm, nk2), lambda i: (i, 0)),
                   pl.BlockSpec((nvt, tm), lambda i: (0, i))],
        out_shape=[jax.ShapeDtypeStruct((t, nq), BF16),
                   jax.ShapeDtypeStruct((t, nk2), BF16),
                   jax.ShapeDtypeStruct((nvt, t), BF16)],
        compiler_params=_params(("parallel",)),
    )(x2, wb[:, :nq + nk], jnp.transpose(wb[:, nq + nk:]), cos_t, sin_t)


def _swa_attn_kernel(sink_ref, q_ref, kp_ref, kc_ref, vp_ref, vc_ref, o_ref):
    blk = pl.program_id(1)
    bq = q_ref.shape[0]
    dh = SWA_HEAD_DIM
    slabs = SWA_GROUP // 2
    cols = slabs * bq
    kj = lax.broadcasted_iota(I32, (2 * bq, cols), 0)
    qi = lax.broadcasted_iota(I32, (2 * bq, cols), 1) % bq
    rel = qi + bq - kj
    allowed = (rel >= 0) & (rel < SWA_WINDOW) & ((kj >= bq) | (blk > 0))
    for kh in range(SWA_KV_HEADS):
        base = kh * SWA_GROUP * dh
        qs = jnp.concatenate([q_ref[:, base + s * LANES:base + (s + 1) * LANES] for s in range(slabs)], axis=0)
        vcat = jnp.concatenate([vp_ref[kh * SWA_V_PAD:(kh + 1) * SWA_V_PAD, :],
                                vc_ref[kh * SWA_V_PAD:(kh + 1) * SWA_V_PAD, :]], axis=1)
        for half in range(2):
            ks = (2 * kh + half) * LANES
            kcat = jnp.concatenate([kp_ref[:, ks:ks + LANES], kc_ref[:, ks:ks + LANES]], axis=0)
            heads = [kh * SWA_GROUP + 2 * s + half for s in range(slabs)]
            sink = jnp.concatenate([jnp.full((1, bq), sink_ref[h] * LOG2_E, F32) for h in heads], axis=1)
            logits = lax.dot_general(kcat, qs, (((1,), (1,)), ((), ())), preferred_element_type=F32)
            logits = jnp.where(allowed, logits, NEG_BIG)
            m = jnp.maximum(jnp.max(logits, axis=0, keepdims=True), sink)
            p = jnp.exp2(logits - m).astype(BF16)
            pv = jnp.dot(vcat, p, preferred_element_type=F32)
            inv = 1.0 / (pv[dh:dh + 1, :] + jnp.exp2(sink - m))
            o = (pv[:dh, :] * inv).astype(BF16)
            for s, h in enumerate(heads):
                o_ref[h * dh:(h + 1) * dh, :] = o[:, s * bq:(s + 1) * bq]


def _swa_attn(q, k2, v_t, sinks, batch, seq):
    t, nq = q.shape
    nk2 = k2.shape[1]
    nvt = v_t.shape[0]
    bq = SWA_WINDOW
    nb = seq // bq

    def cur(b, i):
        return (b * nb + i, 0)

    def prev(b, i):
        return (b * nb + jnp.maximum(i - 1, 0), 0)

    def cur_t(b, i):
        return (0, b * nb + i)

    def prev_t(b, i):
        return (0, b * nb + jnp.maximum(i - 1, 0))

    return pl.pallas_call(
        _swa_attn_kernel,
        name="swa_attn",
        grid=(batch, nb),
        in_specs=[pl.BlockSpec(memory_space=pltpu.SMEM),
                  pl.BlockSpec((bq, nq), cur),
                  pl.BlockSpec((bq, nk2), prev),
                  pl.BlockSpec((bq, nk2), cur),
                  pl.BlockSpec((nvt, bq), prev_t),
                  pl.BlockSpec((nvt, bq), cur_t)],
        out_specs=pl.BlockSpec((nq, bq), cur_t),
        out_shape=jax.ShapeDtypeStruct((nq, t), BF16),
        compiler_params=_params(("parallel", "parallel")),
    )(sinks.astype(F32), q, k2, k2, v_t, v_t)


def _proj_ln_kernel(at_ref, w_ref, res_ref, g_ref, b_ref, o_ref, *, alpha):
    mix = lax.dot_general(at_ref[...], w_ref[...], (((0,), (0,)), ((), ())), preferred_element_type=F32)
    o_ref[...] = _layer_norm(alpha * res_ref[...] + mix, g_ref[...], b_ref[...])


def _proj_ln(a_t, w, res, g, b, alpha):
    k, t = a_t.shape
    d = w.shape[1]
    tm = min(t, ROW_TILE)
    return pl.pallas_call(
        functools.partial(_proj_ln_kernel, alpha=alpha),
        name="proj_ln",
        grid=(t // tm,),
        in_specs=[pl.BlockSpec((k, tm), lambda i: (0, i)),
                  pl.BlockSpec((k, d), lambda i: (0, 0)),
                  pl.BlockSpec((tm, d), lambda i: (i, 0)),
                  pl.BlockSpec((1, d), lambda i: (0, 0)),
                  pl.BlockSpec((1, d), lambda i: (0, 0))],
        out_specs=pl.BlockSpec((tm, d), lambda i: (i, 0)),
        out_shape=jax.ShapeDtypeStruct((t, d), F32),
        compiler_params=_params(("parallel",)),
    )(a_t, w.astype(BF16), res, g[None, :], b[None, :])


def _swiglu_cols(xb, w1_ref, w3_ref, w2_ref):
    tf = w1_ref.shape[1]
    chunk = min(tf, SWIGLU_COL_CHUNK)
    out = None
    for c in range(tf // chunk):
        cols = slice(c * chunk, (c + 1) * chunk)
        gate = jnp.dot(xb, w1_ref[:, cols], preferred_element_type=F32)
        up = jnp.dot(xb, w3_ref[:, cols], preferred_element_type=F32)
        h = (jax.nn.silu(gate) * up).astype(BF16)
        part = jnp.dot(h, w2_ref[cols, :], preferred_element_type=F32)
        out = part if out is None else out + part
    return out


def _ffn_dense_kernel(x_ref, w1_ref, w3_ref, w2_ref, g_ref, b_ref, o_ref, xb_ref, acc_ref, *, alpha):
    f = pl.program_id(1)

    @pl.when(f == 0)
    def _():
        xb_ref[...] = x_ref[...].astype(BF16)
        acc_ref[...] = jnp.zeros_like(acc_ref)

    acc_ref[...] += _swiglu_cols(xb_ref[...], w1_ref, w3_ref, w2_ref)

    @pl.when(f == pl.num_programs(1) - 1)
    def _():
        o_ref[...] = _layer_norm(alpha * x_ref[...] + acc_ref[...], g_ref[...], b_ref[...])


def _ffn_tile(f_dim, largest):
    for tf in (largest, 512, 256, 128):
        if f_dim % tf == 0:
            return tf
    return f_dim


def _ffn_dense(x2, w1, w3, w2, g, b, alpha):
    t, d = x2.shape
    f_dim = w1.shape[1]
    tm = min(t, ROW_TILE)
    tf = _ffn_tile(f_dim, FFN_COL_TILE)
    return pl.pallas_call(
        functools.partial(_ffn_dense_kernel, alpha=alpha),
        name="ffn_dense",
        grid=(t // tm, f_dim // tf),
        in_specs=[pl.BlockSpec((tm, d), lambda i, f: (i, 0)),
                  pl.BlockSpec((d, tf), lambda i, f: (0, f)),
                  pl.BlockSpec((d, tf), lambda i, f: (0, f)),
                  pl.BlockSpec((tf, d), lambda i, f: (f, 0)),
                  pl.BlockSpec((1, d), lambda i, f: (0, 0)),
                  pl.BlockSpec((1, d), lambda i, f: (0, 0))],
        out_specs=pl.BlockSpec((tm, d), lambda i, f: (i, 0)),
        out_shape=jax.ShapeDtypeStruct((t, d), F32),
        scratch_shapes=[pltpu.VMEM((tm, d), BF16), pltpu.VMEM((tm, d), F32)],
        compiler_params=_params(("parallel", "arbitrary")),
    )(x2, w1.astype(BF16), w3.astype(BF16), w2.astype(BF16), g[None, :], b[None, :])


def _ple_kernel(x_ref, p_ref, wg_ref, wp_ref, o_ref):
    x = x_ref[...]
    gate = jax.nn.sigmoid(jnp.dot(x.astype(BF16), wg_ref[...], preferred_element_type=F32))
    emb = jnp.dot(p_ref[...].astype(BF16), wp_ref[...], preferred_element_type=F32)
    o_ref[...] = x + gate * emb


def _ple(x2, p2, w_g, w_p):
    t, d = x2.shape
    pd = p2.shape[1]
    tm = min(t, ROW_TILE)
    return pl.pallas_call(
        _ple_kernel,
        name="ple",
        grid=(t // tm,),
        in_specs=[pl.BlockSpec((tm, d), lambda i: (i, 0)),
                  pl.BlockSpec((tm, pd), lambda i: (i, 0)),
                  pl.BlockSpec((d, d), lambda i: (0, 0)),
                  pl.BlockSpec((pd, d), lambda i: (0, 0))],
        out_specs=pl.BlockSpec((tm, d), lambda i: (i, 0)),
        out_shape=jax.ShapeDtypeStruct((t, d), F32),
        compiler_params=_params(("parallel",)),
    )(x2, p2, w_g.astype(BF16), w_p.astype(BF16))


def _dsa_q_kernel(x_ref, w_ref, cos_ref, sin_ref, q_ref):
    xb = x_ref[...].astype(BF16)
    acc = jnp.dot(xb, w_ref[...], preferred_element_type=F32)
    cs = cos_ref[:, :LANES]
    sn = sin_ref[:, :LANES]
    scale = MLA_SCALE * LOG2_E
    for h in range(MLA_HEADS):
        base = h * MLA_QK_PAD
        q_ref[:, base:base + LANES] = (acc[:, base:base + LANES] * scale).astype(BF16)
        a = acc[:, base + LANES:base + 2 * LANES]
        q_ref[:, base + LANES:base + 2 * LANES] = (_rope_lanes(a, cs, sn, ROPE_HALF) * scale).astype(BF16)


def _dsa_q(x2, w_q, cos_t, sin_t):
    t, d = x2.shape
    n = w_q.shape[1]
    tm = min(t, ROW_TILE)
    return pl.pallas_call(
        _dsa_q_kernel,
        name="dsa_q",
        grid=(t // tm,),
        in_specs=[pl.BlockSpec((tm, d), lambda i: (i, 0)),
                  pl.BlockSpec((d, n), lambda i: (0, 0)),
                  pl.BlockSpec((tm, 2 * LANES), lambda i: (i, 0)),
                  pl.BlockSpec((tm, 2 * LANES), lambda i: (i, 0))],
        out_specs=pl.BlockSpec((tm, n), lambda i: (i, 0)),
        out_shape=jax.ShapeDtypeStruct((t, n), BF16),
        compiler_params=_params(("parallel",)),
    )(x2, w_q, cos_t, sin_t)


def _dsa_small_kernel(x_ref, w_ref, cos_ref, sin_ref, kvn_ref, ckv_ref, kr_ref, ki_ref, wi_ref):
    xb = x_ref[...].astype(BF16)
    acc = jnp.dot(xb, w_ref[...], preferred_element_type=F32)
    c = acc[:, :MLA_LATENT]
    ms = jnp.mean(c * c, axis=-1, keepdims=True)
    ckv_ref[...] = (c * lax.rsqrt(ms + RMS_EPS) * kvn_ref[...]).astype(BF16)
    o = MLA_LATENT
    kr_ref[...] = _rope_lanes(acc[:, o:o + LANES], cos_ref[:, :LANES], sin_ref[:, :LANES], ROPE_HALF).astype(BF16)
    ki_ref[...] = _rope_lanes(acc[:, o + LANES:o + 2 * LANES], cos_ref[:, LANES:], sin_ref[:, LANES:],
                              IDX_ROPE // 2).astype(BF16)
    wi_ref[...] = acc[:, o + 2 * LANES:o + 3 * LANES] * (IDX_HEADS ** -0.5)


def _dsa_small(x2, w_small, kv_norm, cos_t, sin_t):
    t, d = x2.shape
    n = w_small.shape[1]
    tm = min(t, ROW_TILE)
    return pl.pallas_call(
        _dsa_small_kernel,
        name="dsa_small",
        grid=(t // tm,),
        in_specs=[pl.BlockSpec((tm, d), lambda i: (i, 0)),
                  pl.BlockSpec((d, n), lambda i: (0, 0)),
                  pl.BlockSpec((tm, 2 * LANES), lambda i: (i, 0)),
                  pl.BlockSpec((tm, 2 * LANES), lambda i: (i, 0)),
                  pl.BlockSpec((1, MLA_LATENT), lambda i: (0, 0))],
        out_specs=[pl.BlockSpec((tm, MLA_LATENT), lambda i: (i, 0)),
                   pl.BlockSpec((tm, LANES), lambda i: (i, 0)),
                   pl.BlockSpec((tm, LANES), lambda i: (i, 0)),
                   pl.BlockSpec((tm, LANES), lambda i: (i, 0))],
        out_shape=[jax.ShapeDtypeStruct((t, MLA_LATENT), BF16),
                   jax.ShapeDtypeStruct((t, LANES), BF16),
                   jax.ShapeDtypeStruct((t, LANES), BF16),
                   jax.ShapeDtypeStruct((t, LANES), F32)],
        compiler_params=_params(("parallel",)),
    )(x2, w_small, cos_t, sin_t, kv_norm[None, :].astype(F32))


def _dsa_qi_kernel(x_ref, w_ref, cos_ref, sin_ref, q_ref):
    xb = x_ref[...].astype(BF16)
    acc = jnp.dot(xb, w_ref[...], preferred_element_type=F32)
    cs = cos_ref[:, LANES:]
    sn = sin_ref[:, LANES:]
    scale = IDX_DIM ** -0.5
    for c in range(acc.shape[1] // LANES):
        a = acc[:, c * LANES:(c + 1) * LANES]
        q_ref[:, c * LANES:(c + 1) * LANES] = (_rope_lanes(a, cs, sn, IDX_ROPE // 2) * scale).astype(BF16)


def _dsa_qi(x2, w_qi, cos_t, sin_t):
    t, d = x2.shape
    n = w_qi.shape[1]
    tm = min(t, ROW_TILE)
    return pl.pallas_call(
        _dsa_qi_kernel,
        name="dsa_qi",
        grid=(t // tm,),
        in_specs=[pl.BlockSpec((tm, d), lambda i: (i, 0)),
                  pl.BlockSpec((d, n), lambda i: (0, 0)),
                  pl.BlockSpec((tm, 2 * LANES), lambda i: (i, 0)),
                  pl.BlockSpec((tm, 2 * LANES), lambda i: (i, 0))],
        out_specs=pl.BlockSpec((tm, n), lambda i: (i, 0)),
        out_shape=jax.ShapeDtypeStruct((t, n), BF16),
        compiler_params=_params(("parallel",)),
    )(x2, w_qi, cos_t, sin_t)


def _dsa_kv_kernel(c_ref, kr_ref, wk_ref, wvt_ref, k_ref, vt_ref):
    c = c_ref[...]
    kn = jnp.dot(c, wk_ref[...], preferred_element_type=F32)
    kr = kr_ref[...]
    for h in range(MLA_HEADS):
        base = h * MLA_QK_PAD
        k_ref[:, base:base + LANES] = kn[:, h * LANES:(h + 1) * LANES].astype(BF16)
        k_ref[:, base + LANES:base + 2 * LANES] = kr
    vt = lax.dot_general(wvt_ref[...], c, (((1,), (1,)), ((), ())), preferred_element_type=F32)
    ones = jnp.ones((MLA_V_PAD - MLA_V, c.shape[0]), BF16)
    for h in range(MLA_HEADS):
        vt_ref[h * MLA_V_PAD:h * MLA_V_PAD + MLA_V, :] = vt[h * MLA_V:(h + 1) * MLA_V, :].astype(BF16)
        vt_ref[h * MLA_V_PAD + MLA_V:(h + 1) * MLA_V_PAD, :] = ones


def _dsa_kv(ckv, kr, w_uk_all, w_uv_t):
    t, c_dim = ckv.shape
    nk = MLA_HEADS * MLA_QK_PAD
    nv = w_uv_t.shape[0]
    nv_pad = MLA_HEADS * MLA_V_PAD
    tm = min(t, ROW_TILE)
    return pl.pallas_call(
        _dsa_kv_kernel,
        name="dsa_kv",
        grid=(t // tm,),
        in_specs=[pl.BlockSpec((tm, c_dim), lambda i: (i, 0)),
                  pl.BlockSpec((tm, LANES), lambda i: (i, 0)),
                  pl.BlockSpec((c_dim, MLA_HEADS * MLA_NOPE), lambda i: (0, 0)),
                  pl.BlockSpec((nv, c_dim), lambda i: (0, 0))],
        out_specs=[pl.BlockSpec((tm, nk), lambda i: (i, 0)),
                   pl.BlockSpec((nv_pad, tm), lambda i: (0, i))],
        out_shape=[jax.ShapeDtypeStruct((t, nk), BF16),
                   jax.ShapeDtypeStruct((nv_pad, t), BF16)],
        compiler_params=_params(("parallel",)),
    )(ckv, kr, w_uk_all, w_uv_t)


def _float_sort_key(s):
    bits = lax.bitcast_convert_type(s, I32)
    return bits ^ ((bits >> 31) & jnp.int32(0x7FFFFFFF))


def _dsa_select_kernel(k_ref, q_ref, w_ref, bias_ref, key_ref, tie_ref, *, sc, cc, unroll, n_sel):
    j = pl.program_id(1)
    s_len, tq = key_ref.shape
    n_chunks = s_len // sc
    q_lo = j * tq
    n_need = (q_lo + tq + sc * unroll - 1) // (sc * unroll) * unroll
    qt = q_ref[...]
    w_row = w_ref[...]
    q_pos = q_lo + lax.broadcasted_iota(I32, (sc, tq), 1)
    k_off = lax.broadcasted_iota(I32, (sc, tq), 0)
    min_key = jnp.int32(-2 ** 31)

    def score_chunks(i, carry):
        for u in range(unroll):
            row0 = pl.multiple_of((i * unroll + u) * sc, sc)
            kc = k_ref[pl.ds(row0, sc), :]
            d = jnp.dot(kc, qt, preferred_element_type=F32)
            s = jnp.zeros((sc, tq), F32)
            for h in range(IDX_HEADS):
                s = s + jnp.maximum(d[:, h * tq:(h + 1) * tq], 0.0) * w_row[:, h * tq:(h + 1) * tq]
            causal = (row0 + k_off) <= q_pos
            key_ref[pl.ds(row0, sc), :] = jnp.where(causal, _float_sort_key(s), min_key)
        return carry

    lax.fori_loop(0, n_need // unroll, score_chunks, 0)

    n_cnt = (n_need * sc + cc - 1) // cc

    def fill_chunk(c, carry):
        row0 = pl.multiple_of(c * sc, sc)
        key_ref[pl.ds(row0, sc), :] = jnp.full((sc, tq), min_key, I32)
        return carry

    lax.fori_loop(n_need, n_cnt * (cc // sc), fill_chunk, 0)

    def count_ge(cand):
        def body(c, cnt):
            row0 = pl.multiple_of(c * cc, cc)
            hit = jnp.where(key_ref[pl.ds(row0, cc), :] >= cand, 1, 0).astype(I32)
            return cnt + jnp.sum(hit.reshape(cc // 32, 32, tq), axis=0)

        cnt = lax.fori_loop(0, n_cnt, body, jnp.zeros((32, tq), I32))
        return jnp.sum(cnt, axis=0, keepdims=True)

    zero = jnp.zeros((1, tq), I32)
    c0 = count_ge(zero)
    thr = jnp.where(c0 >= n_sel, zero, jnp.full((1, tq), min_key, I32))
    cnt_thr = jnp.where(c0 >= n_sel, c0, 0)

    def bit_step(it, carry):
        thr, cnt_thr = carry
        cand = thr + (jnp.int32(1) << (30 - it))
        cnt = count_ge(cand)
        take = cnt >= n_sel
        return jnp.where(take, cand, thr), jnp.where(take, cnt, cnt_thr)

    thr, cnt_thr = lax.fori_loop(0, 31, bit_step, (thr, cnt_thr))

    excess = cnt_thr - n_sel
    tie_ref[0:1, :] = jnp.full((1, tq), s_len, I32)

    @pl.when(jnp.max(excess) > 0)
    def _():
        def count_tie_below(x):
            def body(c, cnt):
                row0 = pl.multiple_of(c * cc, cc)
                idx = row0 + lax.broadcasted_iota(I32, (cc, tq), 0)
                hit = jnp.where((key_ref[pl.ds(row0, cc), :] == thr) & (idx < x), 1, 0).astype(I32)
                return cnt + jnp.sum(hit.reshape(cc // 32, 32, tq), axis=0)

            cnt = lax.fori_loop(0, n_cnt, body, jnp.zeros((32, tq), I32))
            return jnp.sum(cnt, axis=0, keepdims=True)

        n_tie = count_tie_below(jnp.full((1, tq), s_len, I32))
        need = n_sel - (cnt_thr - n_tie)
        n_bits = s_len.bit_length()

        def idx_step(it, x):
            cand = x + (jnp.int32(1) << (n_bits - 1 - it))
            return jnp.where(count_tie_below(cand) <= need, cand, x)

        x = lax.fori_loop(0, n_bits, idx_step, zero)
        tie_ref[0:1, :] = jnp.where(excess > 0, x, s_len)

    tie_below = tie_ref[0:1, :]
    thr_low = jnp.maximum(thr, min_key + 1)

    def write_chunk(c, carry):
        row0 = pl.multiple_of(c * sc, sc)
        thr_here = thr_low + jnp.where(k_off >= tie_below - row0, 1, 0)
        keep = key_ref[pl.ds(row0, sc), :] >= thr_here
        bias_ref[pl.ds(row0, sc), :] = jnp.where(keep, 0.0, NEG_BIG).astype(F32)
        return carry

    def mask_chunk(c, carry):
        row0 = pl.multiple_of(c * sc, sc)
        bias_ref[pl.ds(row0, sc), :] = jnp.full((sc, tq), NEG_BIG, F32)
        return carry

    lax.fori_loop(0, n_need, write_chunk, 0)
    lax.fori_loop(n_need, n_chunks, mask_chunk, 0)


def _dsa_select(k_i, q_i_t, w_rows, batch, seq, n_sel):
    tq = LANES
    nq = seq // tq
    sc = min(seq, SELECT_SCORE_ROWS)
    cc = min(seq, SELECT_COUNT_ROWS)
    return pl.pallas_call(
        functools.partial(_dsa_select_kernel, sc=sc, cc=cc, unroll=max(u for u in (4, 2, 1) if (seq // sc) % u == 0), n_sel=n_sel),
        name="dsa_select",
        grid=(batch, nq),
        in_specs=[pl.BlockSpec((None, seq, IDX_DIM), lambda b, j: (b, 0, 0)),
                  pl.BlockSpec((None, None, IDX_DIM, IDX_HEADS * tq), lambda b, j: (b, j, 0, 0)),
                  pl.BlockSpec((None, None, 1, IDX_HEADS * tq), lambda b, j: (b, j, 0, 0))],
        out_specs=pl.BlockSpec((None, seq, tq), lambda b, j: (b, 0, j)),
        out_shape=jax.ShapeDtypeStruct((batch, seq, seq), F32),
        scratch_shapes=[pltpu.VMEM((seq, tq), I32), pltpu.VMEM((8, tq), I32)],
        compiler_params=_params(("parallel", "parallel")),
    )(k_i, q_i_t, w_rows)


def _dsa_attn_kernel(qt_ref, k_ref, vta_ref, vtb_ref, bias_ref, o_ref, m_ref, a_ref, *head_refs, sc):
    s_refs = head_refs[:MLA_HEADS]
    acc_refs = head_refs[MLA_HEADS:]
    j = pl.program_id(1)
    g = pl.program_id(2)
    tq = qt_ref.shape[1]
    last_pair = (j * tq + tq - 1) // (2 * sc)

    @pl.when(g == 0)
    def _():
        for h in range(MLA_HEADS):
            acc_refs[h][...] = jnp.zeros_like(acc_refs[h])
            s_refs[h][1] = jnp.full((sc, tq), NEG_BIG, F32)
        m_ref[...] = jnp.full_like(m_ref, M_INIT)
        a_ref[...] = jnp.ones_like(a_ref)

    def finish(slot, vt_ref, m_now, a_now):
        for h in range(MLA_HEADS):
            p = jnp.exp2(s_refs[h][slot] - m_now[h:h + 1, :]).astype(BF16)
            pv = jnp.dot(vt_ref[h * MLA_V_PAD:(h + 1) * MLA_V_PAD, :], p, preferred_element_type=F32)
            acc_refs[h][...] = a_now[h:h + 1, :] * acc_refs[h][...] + pv

    def score(slot, half, m_now):
        rows = pl.ds(half * sc, sc)
        chunk_max = []
        for h in range(MLA_HEADS):
            qh = qt_ref[h * MLA_QK_PAD:(h + 1) * MLA_QK_PAD, :]
            kh = k_ref[rows, h * MLA_QK_PAD:(h + 1) * MLA_QK_PAD]
            s = jnp.dot(kh, qh, preferred_element_type=F32) + bias_ref[rows, :]
            s_refs[h][slot] = s
            chunk_max.append(jnp.max(s, axis=0, keepdims=True))
        m_new = jnp.maximum(m_now, jnp.concatenate(chunk_max, axis=0))
        return m_new, jnp.exp2(m_now - m_new)

    @pl.when(g <= last_pair)
    def _():
        m0 = m_ref[...]
        finish(1, vta_ref, m0, a_ref[...])
        m1, a1 = score(0, 0, m0)
        finish(0, vtb_ref, m1, a1)
        m2, a2 = score(1, 1, m1)
        m_ref[...] = m2
        a_ref[...] = a2

    @pl.when(g == last_pair + 1)
    def _():
        finish(1, vta_ref, m_ref[...], a_ref[...])
        for h in range(MLA_HEADS):
            inv = 1.0 / acc_refs[h][MLA_V:MLA_V + 1, :]
            o_ref[h * MLA_V:(h + 1) * MLA_V, :] = (acc_refs[h][:MLA_V, :] * inv).astype(BF16)


def _dsa_attn(q_cat_t, k_cat, v_t, bias, batch, seq):
    nqk, t = q_cat_t.shape
    nv_pad = v_t.shape[0]
    nv = MLA_HEADS * MLA_V
    tq = min(seq, ATTN_QUERY_TILE)
    sc = min(ATTN_KEY_CHUNK, tq // 2)
    nq = seq // tq
    nc = seq // sc
    npair = nc // 2

    def last_pair(j):
        return (j * tq + tq - 1) // (2 * sc)

    return pl.pallas_call(
        functools.partial(_dsa_attn_kernel, sc=sc),
        name="dsa_attn",
        grid=(batch, nq, npair + 1),
        in_specs=[pl.BlockSpec((nqk, tq), lambda b, j, g: (0, b * nq + j)),
                  pl.BlockSpec((2 * sc, nqk), lambda b, j, g: (b * npair + jnp.minimum(g, last_pair(j)), 0)),
                  pl.BlockSpec((nv_pad, sc), lambda b, j, g: (0, b * nc + jnp.clip(2 * g - 1, 0, 2 * last_pair(j) + 1))),
                  pl.BlockSpec((nv_pad, sc), lambda b, j, g: (0, b * nc + jnp.minimum(2 * g, 2 * last_pair(j)))),
                  pl.BlockSpec((None, 2 * sc, tq), lambda b, j, g: (b, jnp.minimum(g, last_pair(j)), j))],
        out_specs=pl.BlockSpec((nv, tq), lambda b, j, g: (0, b * nq + j)),
        out_shape=jax.ShapeDtypeStruct((nv, t), BF16),
        scratch_shapes=([pltpu.VMEM((MLA_HEADS, tq), F32), pltpu.VMEM((MLA_HEADS, tq), F32)]
                        + [pltpu.VMEM((2, sc, tq), F32)] * MLA_HEADS
                        + [pltpu.VMEM((MLA_V_PAD, tq), F32)] * MLA_HEADS),
        compiler_params=_params(("parallel", "parallel", "arbitrary")),
    )(q_cat_t, k_cat, v_t, v_t, bias)


def _dsa_mixer_out(x2, cos_t, sin_t, w_in, kv_norm, w_uk, w_uv, batch, seq):
    t, d = x2.shape
    nq = MLA_HEADS * (MLA_NOPE + MLA_ROPE)
    o_c = nq
    o_kr = o_c + MLA_LATENT
    o_qi = o_kr + MLA_ROPE
    o_ki = o_qi + IDX_HEADS * IDX_DIM
    o_wi = o_ki + IDX_DIM
    wb = w_in.astype(BF16)
    wq = wb[:, :nq].reshape(d, MLA_HEADS, MLA_NOPE + MLA_ROPE)
    wq = jnp.pad(wq, ((0, 0), (0, 0), (0, MLA_QK_PAD - MLA_NOPE - MLA_ROPE))).reshape(d, MLA_HEADS * MLA_QK_PAD)
    pad_to_lanes = lambda w: jnp.pad(w, ((0, 0), (0, LANES - w.shape[1])))
    w_small = jnp.concatenate([wb[:, o_c:o_kr], pad_to_lanes(wb[:, o_kr:o_qi]), pad_to_lanes(wb[:, o_ki:o_wi]),
                               pad_to_lanes(wb[:, o_wi:])], axis=1)
    w_qi = wb[:, o_qi:o_ki]
    w_uk_all = jnp.transpose(w_uk, (1, 0, 2)).reshape(MLA_LATENT, MLA_HEADS * MLA_NOPE).astype(BF16)
    w_uv_t = jnp.transpose(w_uv, (0, 2, 1)).reshape(MLA_HEADS * MLA_V, MLA_LATENT).astype(BF16)

    q_cat = _dsa_q(x2, wq, cos_t, sin_t)
    ckv, kr, ki, wi = _dsa_small(x2, w_small, kv_norm, cos_t, sin_t)
    qi = _dsa_qi(x2, w_qi, cos_t, sin_t)
    k_cat, v_t = _dsa_kv(ckv, kr, w_uk_all, w_uv_t)

    tq = LANES
    nqb = seq // tq
    k_i = ki[:, :IDX_DIM].reshape(batch, seq, IDX_DIM)
    q_i_t = jnp.transpose(qi.reshape(batch, nqb, tq, IDX_HEADS, IDX_DIM), (0, 1, 4, 3, 2))
    q_i_t = q_i_t.reshape(batch, nqb, IDX_DIM, IDX_HEADS * tq)
    w_rows = jnp.transpose(wi[:, :IDX_HEADS].reshape(batch, nqb, tq, IDX_HEADS), (0, 1, 3, 2))
    w_rows = w_rows.reshape(batch, nqb, 1, IDX_HEADS * tq)
    n_sel = min(DSA_TOPK, seq // 4)
    bias = _dsa_select(k_i, q_i_t, w_rows, batch, seq, n_sel)
    o_t = _dsa_attn(jnp.transpose(q_cat), k_cat, v_t, bias, batch, seq)
    return o_t


def _router_kernel(x_ref, w_ref, idx_ref, gate_ref):
    logits = jnp.dot(x_ref[...], w_ref[...], precision=lax.Precision.HIGHEST, preferred_element_type=F32)
    lane = lax.broadcasted_iota(I32, logits.shape, 1)
    lg = jnp.where(lane < MOE_EXPERTS, logits, -jnp.inf)
    v1 = jnp.max(lg, axis=-1, keepdims=True)
    i1 = jnp.min(jnp.where(lg == v1, lane, LANES), axis=-1, keepdims=True)
    lg2 = jnp.where(lane == i1, -jnp.inf, lg)
    v2 = jnp.max(lg2, axis=-1, keepdims=True)
    i2 = jnp.min(jnp.where(lg2 == v2, lane, LANES), axis=-1, keepdims=True)
    e2 = jnp.exp(v2 - v1)
    den = 1.0 + e2
    idx_ref[...] = jnp.where(lane == 0, i1, jnp.where(lane == 1, i2, 0))
    gate_ref[...] = jnp.where(lane == 0, 1.0 / den, jnp.where(lane == 1, e2 / den, 0.0))


def _router(x2, w_router):
    t, d = x2.shape
    wr = jnp.pad(w_router.astype(F32), ((0, 0), (0, LANES - w_router.shape[1])))
    tm = min(t, ROW_TILE)
    return pl.pallas_call(
        _router_kernel,
        name="moe_router",
        grid=(t // tm,),
        in_specs=[pl.BlockSpec((tm, d), lambda i: (i, 0)),
                  pl.BlockSpec((d, LANES), lambda i: (0, 0))],
        out_specs=[pl.BlockSpec((tm, LANES), lambda i: (i, 0)),
                   pl.BlockSpec((tm, LANES), lambda i: (i, 0))],
        out_shape=[jax.ShapeDtypeStruct((t, LANES), I32),
                   jax.ShapeDtypeStruct((t, LANES), F32)],
        compiler_params=_params(("parallel",)),
    )(x2, wr)


def _row_copy(src_hbm, src_row, dst, dst_row, sem):
    return pltpu.make_async_copy(src_hbm.at[pl.ds(src_row, 1)], dst.at[pl.ds(dst_row, 1)], sem)


def _moe_ffn_kernel(be_ref, nu_ref, tok_ref, tok_next_ref, x_hbm, w1_ref, w3_ref, w2_ref, o_ref,
                    xg_ref, xb_ref, acc_ref, sem):
    i = pl.program_id(0)
    f = pl.program_id(1)
    nf = pl.num_programs(1)
    n_used = nu_ref[0]
    used = i < n_used
    tm = xb_ref.shape[0]
    slot = i % 2

    def start_gather(ids_ref, buf):
        def body(r, carry):
            _row_copy(x_hbm, ids_ref[r], xg_ref.at[buf], r, sem.at[buf]).start()
            return carry

        lax.fori_loop(0, tm, body, 0, unroll=8)

    @pl.when(used & (f == 0))
    def _():
        @pl.when(i == 0)
        def _():
            start_gather(tok_ref, 0)

        pltpu.make_async_copy(x_hbm.at[pl.ds(0, tm)], xg_ref.at[slot], sem.at[slot]).wait()

        @pl.when(i + 1 < n_used)
        def _():
            start_gather(tok_next_ref, 1 - slot)

        xb_ref[...] = xg_ref[slot].astype(BF16)
        acc_ref[...] = jnp.zeros_like(acc_ref)

    @pl.when(used)
    def _():
        acc_ref[...] += _swiglu_cols(xb_ref[...], w1_ref, w3_ref, w2_ref)

    @pl.when(used & (f == nf - 1))
    def _():
        o_ref[...] = acc_ref[...]

    @pl.when(jnp.logical_not(used) & (f == nf - 1))
    def _():
        o_ref[...] = jnp.zeros_like(o_ref)


def _moe_ffn(x2, slot_tok, w1, w3, w2, blk_expert, n_used, tm):
    n_slots = slot_tok.shape[0]
    d = x2.shape[1]
    f_dim = w1.shape[2]
    tf = _ffn_tile(f_dim, MOE_COL_TILE)
    nf = f_dim // tf
    nt = n_slots // tm

    def f_sel(i, f, nu):
        return jnp.where(i < nu[0], f, nf - 1)

    grid_spec = pltpu.PrefetchScalarGridSpec(
        num_scalar_prefetch=2,
        grid=(nt, nf),
        in_specs=[pl.BlockSpec((tm,), lambda i, f, be, nu: (i,), memory_space=pltpu.SMEM),
                  pl.BlockSpec((tm,), lambda i, f, be, nu: (jnp.minimum(i + 1, nt - 1),), memory_space=pltpu.SMEM),
                  pl.BlockSpec(memory_space=pl.ANY),
                  pl.BlockSpec((None, d, tf), lambda i, f, be, nu: (be[i], 0, f_sel(i, f, nu))),
                  pl.BlockSpec((None, d, tf), lambda i, f, be, nu: (be[i], 0, f_sel(i, f, nu))),
                  pl.BlockSpec((None, tf, d), lambda i, f, be, nu: (be[i], f_sel(i, f, nu), 0))],
        out_specs=pl.BlockSpec((tm, d), lambda i, f, be, nu: (i, 0)),
        scratch_shapes=[pltpu.VMEM((2, tm, d), F32), pltpu.VMEM((tm, d), BF16), pltpu.VMEM((tm, d), F32),
                        pltpu.SemaphoreType.DMA((2,))],
    )
    return pl.pallas_call(
        _moe_ffn_kernel,
        name="moe_ffn",
        grid_spec=grid_spec,
        out_shape=jax.ShapeDtypeStruct((n_slots, d), F32),
        compiler_params=_params(("arbitrary", "arbitrary")),
    )(blk_expert, n_used, slot_tok, slot_tok, x2, w1, w3, w2)


def _moe_combine_kernel(dest_ref, dest_next_ref, y_hbm, x_ref, gate_ref, g_ref, b_ref, o_ref, y_ref, sem, *,
                        alpha, tb):
    i = pl.program_id(0)
    slot = i % 2

    def start_gather(ids_ref, buf):
        def body(tok, carry):
            for k in range(MOE_TOP_K):
                _row_copy(y_hbm, ids_ref[MOE_TOP_K * tok + k], y_ref.at[buf, k], tok, sem.at[buf]).start()
            return carry

        lax.fori_loop(0, tb, body, 0, unroll=4)

    @pl.when(i == 0)
    def _():
        start_gather(dest_ref, 0)

    for k in range(MOE_TOP_K):
        pltpu.make_async_copy(y_hbm.at[pl.ds(0, tb)], y_ref.at[slot, k], sem.at[slot]).wait()

    @pl.when(i + 1 < pl.num_programs(0))
    def _():
        start_gather(dest_next_ref, 1 - slot)

    gates = gate_ref[...]
    ff = gates[:, 0:1] * y_ref[slot, 0] + gates[:, 1:2] * y_ref[slot, 1]
    o_ref[...] = _layer_norm(alpha * x_ref[...] + ff, g_ref[...], b_ref[...])


def _moe_combine(x2, y, dest, gates, g, b, alpha):
    t, d = x2.shape
    tb = min(t, COMBINE_ROW_TILE)
    nt = t // tb
    return pl.pallas_call(
        functools.partial(_moe_combine_kernel, alpha=alpha, tb=tb),
        name="moe_combine",
        grid=(nt,),
        in_specs=[pl.BlockSpec((tb * MOE_TOP_K,), lambda i: (i,), memory_space=pltpu.SMEM),
                  pl.BlockSpec((tb * MOE_TOP_K,), lambda i: (jnp.minimum(i + 1, nt - 1),), memory_space=pltpu.SMEM),
                  pl.BlockSpec(memory_space=pl.ANY),
                  pl.BlockSpec((tb, d), lambda i: (i, 0)),
                  pl.BlockSpec((tb, LANES), lambda i: (i, 0)),
                  pl.BlockSpec((1, d), lambda i: (0, 0)),
                  pl.BlockSpec((1, d), lambda i: (0, 0))],
        out_specs=pl.BlockSpec((tb, d), lambda i: (i, 0)),
        out_shape=jax.ShapeDtypeStruct((t, d), F32),
        scratch_shapes=[pltpu.VMEM((2, MOE_TOP_K, tb, d), F32), pltpu.SemaphoreType.DMA((2,))],
        compiler_params=_params(("arbitrary",)),
    )(dest, dest, y, x2, gates, g[None, :], b[None, :])


def _moe_layer(x2, w_router, w1, w3, w2, g, b, alpha):
    t, d = x2.shape
    n = t * MOE_TOP_K
    tm = min(ROW_TILE, t)
    idx_pad, gates = _router(x2, w_router)
    e_flat = idx_pad[:, :MOE_TOP_K].reshape(n)
    flat_ids = jnp.arange(n, dtype=I32)
    _, order = lax.sort((e_flat, flat_ids), num_keys=1, is_stable=True)
    _, inv_order = lax.sort((order, flat_ids), num_keys=1)
    experts = jnp.arange(MOE_EXPERTS, dtype=I32)
    counts = jnp.sum((e_flat[:, None] == experts[None, :]).astype(I32), axis=0)
    starts = jnp.cumsum(counts) - counts
    padded = (counts + tm - 1) // tm * tm
    pends = jnp.cumsum(padded)
    pstarts = pends - padded
    n_tiles = (n + MOE_EXPERTS * (tm - 1) + tm - 1) // tm
    n_used = (pends[-1] // tm).astype(I32)
    blk_start = jnp.arange(n_tiles, dtype=I32) * tm
    blk_expert = jnp.minimum(jnp.sum((pends[None, :] <= blk_start[:, None]).astype(I32), axis=1), MOE_EXPERTS - 1)
    blk_expert = jnp.where(jnp.arange(n_tiles) < n_used, blk_expert, blk_expert[jnp.maximum(n_used - 1, 0)])
    n_slots = n_tiles * tm
    slot = jnp.arange(n_slots, dtype=I32)
    slot_e = jnp.repeat(blk_expert, tm)
    shift = pstarts - starts
    order_tok = jnp.pad(order // MOE_TOP_K, (0, n_slots - n))
    slot_tok = jnp.zeros((n_slots,), I32)
    for e in range(MOE_EXPERTS):
        in_group = (slot_e == e) & (slot - pstarts[e] < counts[e]) & (slot < n_used * tm)
        slot_tok = jnp.where(in_group, jnp.roll(order_tok, shift[e]), slot_tok)
    dest = (inv_order + jnp.sum(jnp.where(e_flat[:, None] == experts[None, :], shift[None, :], 0), axis=1)).astype(I32)

    y = _moe_ffn(x2, slot_tok, w1.astype(BF16), w3.astype(BF16), w2.astype(BF16), blk_expert.astype(I32),
                 n_used.reshape(1), tm)
    return _moe_combine(x2, y, dest, gates, g, b, alpha)


def kernel(x, p, positions, swa_w_qkv, swa_sinks, swa_w_o, dense_w1, dense_w3, dense_w2, dsa_w_in, dsa_kv_norm,
           dsa_w_uk, dsa_w_uv, dsa_w_o, moe_router, moe_w1, moe_w3, moe_w2, ln_g, ln_b, ple_w_p, ple_w_g):
    batch, seq, d = x.shape
    depth = ln_g.shape[0]
    alpha = (2.0 * depth) ** 0.25
    t = batch * seq
    assert seq % (2 * LANES) == 0 and d % LANES == 0, (seq, d)
    assert t % min(t, ROPE_ROW_TILE) == 0, t
    x2 = x.reshape(t, d)
    cos_t, sin_t = _rope_tables(positions)
    for i in range(depth):
        j = i // 2
        if i % 2 == 0:
            q, k2, v_t = _swa_qkv(x2, swa_w_qkv[j], cos_t, sin_t)
            o = _swa_attn(q, k2, v_t, swa_sinks[j], batch, seq)
            x2 = _proj_ln(o, swa_w_o[j], x2, ln_g[i, 0], ln_b[i, 0], alpha)
            x2 = _ffn_dense(x2, dense_w1[j], dense_w3[j], dense_w2[j], ln_g[i, 1], ln_b[i, 1], alpha)
        else:
            o = _dsa_mixer_out(x2, cos_t, sin_t, dsa_w_in[j], dsa_kv_norm[j], dsa_w_uk[j], dsa_w_uv[j], batch, seq)
            x2 = _proj_ln(o, dsa_w_o[j], x2, ln_g[i, 0], ln_b[i, 0], alpha)
            x2 = _moe_layer(x2, moe_router[j], moe_w1[j], moe_w3[j], moe_w2[j], ln_g[i, 1], ln_b[i, 1], alpha)
        x2 = _ple(x2, p[i].reshape(t, -1), ple_w_g[i], ple_w_p[i])
    return x2.reshape(batch, seq, d)
```
